```python
import jax, jax.numpy as jnp
from jax import lax
import numpy as np

D_MODEL = 2048
BATCH = 8
SEQ = 2048
DEPTH = 1
DEC_BATCH = 128
DEC_SEQ = 8
PAST_LEN = 2048
PAGE_SIZE = 128

H_A = 8
DH_A = D_MODEL // 16
W_A = H_A * DH_A
H_R = 8
DK_R = D_MODEL // 16
DV_R = D_MODEL // 16
W_R = H_R * DV_R
D_FF = ((8 * D_MODEL // 3 + 255) // 256) * 256
CONV_W = 3
Q_BLOCK = 128
RET_CHUNK = 128
ROPE_BASE = 10000.0
LN_EPS = 1e-5
GN_EPS = 1e-5
ALPHA = (2.0 * DEPTH) ** 0.25
BETA = (8.0 * DEPTH) ** -0.25
IN_SIZES = (W_A, W_A, W_A, H_A, H_R * DK_R, H_R * DK_R, W_R, W_R, D_MODEL, D_MODEL)
N_IN = sum(IN_SIZES)

kernel_name = 'hybrid_fox_retnet_convffn_step'


def _ln(x, g, b):
    xf = x.astype(jnp.float32)
    mu = jnp.mean(xf, -1, keepdims=True)
    var = jnp.mean(jnp.square(xf - mu), -1, keepdims=True)
    y = (xf - mu) * lax.rsqrt(var + LN_EPS) * g.astype(jnp.float32) + b.astype(jnp.float32)
    return y.astype(x.dtype)


def _rope(x, pos):
    half = x.shape[-1] // 2
    inv = ROPE_BASE ** (-jnp.arange(half, dtype=jnp.float32) / half)
    ang = pos.astype(jnp.float32)[:, None] * inv[None, :]
    cos = jnp.cos(ang)[None, :, None, :]
    sin = jnp.sin(ang)[None, :, None, :]
    xf = x.astype(jnp.float32)
    x1, x2 = xf[..., :half], xf[..., half:]
    return jnp.concatenate([x1 * cos - x2 * sin, x1 * sin + x2 * cos], -1).astype(x.dtype)


def _fox_attend(q, k, v, cq, ck, q0):
    nq, nk = q.shape[1], k.shape[1]
    s = jnp.einsum('bqhd,bkhd->bhqk', q, k).astype(jnp.float32) * (q.shape[-1] ** -0.5)
    s = s + jnp.transpose(cq, (0, 2, 1))[:, :, :, None] - jnp.transpose(ck, (0, 2, 1))[:, :, None, :]
    causal = (q0 + jnp.arange(nq))[:, None] >= jnp.arange(nk)[None, :]
    p = jax.nn.softmax(jnp.where(causal, s, -jnp.inf), axis=-1)
    return jnp.einsum('bhqk,bkhd->bqhd', p.astype(v.dtype), v)


def _forgetting_attention(q, k, v, logf, past):
    if past is None:
        c = jnp.cumsum(logf, axis=1)
        T = q.shape[1]
        qb = min(Q_BLOCK, T)
        outs = [_fox_attend(q[:, s:s + qb], k[:, :s + qb], v[:, :s + qb], c[:, s:s + qb], c[:, :s + qb], s)
                for s in range(0, T, qb)]
        return jnp.concatenate(outs, axis=1)
    kp, vp, lfp = past
    P = kp.shape[1]
    k_all = jnp.concatenate([kp.astype(k.dtype), k], axis=1)
    v_all = jnp.concatenate([vp.astype(v.dtype), v], axis=1)
    c = jnp.cumsum(jnp.concatenate([lfp.astype(jnp.float32), logf], axis=1), axis=1)
    return _fox_attend(q, k_all, v_all, c[:, P:], c, P)


def _ret_chunk(S, q, k, v, lg):
    C = q.shape[1]
    idx = jnp.arange(C, dtype=jnp.float32)
    diff = idx[:, None] - idx[None, :]
    dmat = jnp.where(diff[None] >= 0, jnp.exp(diff[None] * lg[:, None, None]), 0.0)
    att = jnp.einsum('bihd,bjhd->bhij', q, k) * dmat[None]
    inner = jnp.einsum('bhij,bjhe->bihe', att, v)
    cross = jnp.einsum('bihd,bhde->bihe', q, S) * jnp.exp((idx + 1.0)[:, None] * lg[None, :])[None, :, :, None]
    w_in = jnp.exp((C - 1.0 - idx)[:, None] * lg[None, :])
    S_new = jnp.exp(C * lg)[None, :, None, None] * S + jnp.einsum('bjhd,bjhe,jh->bhde', k, v, w_in)
    return S_new, inner + cross


def _retention(q, k, v, S0):
    lg = jnp.log(1.0 - jnp.exp2(-5.0 - jnp.arange(H_R, dtype=jnp.float32)))
    B, T = q.shape[0], q.shape[1]
    C = RET_CHUNK if T % RET_CHUNK == 0 else T
    n = T // C

    def to_chunks(a):
        return jnp.swapaxes(a.astype(jnp.float32).reshape((B, n, C) + a.shape[2:]), 0, 1)

    def step(S, qkv):
        return _ret_chunk(S, qkv[0], qkv[1], qkv[2], lg)

    S, o = lax.scan(step, S0.astype(jnp.float32), (to_chunks(q), to_chunks(k), to_chunks(v)))
    return jnp.swapaxes(o, 0, 1).reshape(B, T, H_R, DV_R), S


def _layer(x, cond, t0, past, ret_state, conv_state, w_ada, b_ada, w_in, b_f, w_pa, w_pb, w_o,
           ln1_g, ln1_b, w_up, conv_w, conv_b, w_down, ln2_g, ln2_b):
    B, T, _ = x.shape
    mod = jax.nn.silu(cond) @ w_ada + b_ada
    sh1, sc1, g1, sh2, sc2, g2 = [m[:, None, :] for m in jnp.split(mod, 6, axis=-1)]
    u = x * (1 + sc1) + sh1
    proj = u @ w_in
    qa, ka, va, fa, qr, kr, vr, gr, ga, gb = jnp.split(proj, np.cumsum(IN_SIZES)[:-1].tolist(), axis=-1)
    qa = qa.reshape(B, T, H_A, DH_A)
    ka = ka.reshape(B, T, H_A, DH_A)
    va = va.reshape(B, T, H_A, DH_A)
    logf = jax.nn.log_sigmoid(fa.astype(jnp.float32) + b_f.astype(jnp.float32))
    oa = _forgetting_attention(qa, ka, va, logf, past).reshape(B, T, W_A)
    pos = t0 + jnp.arange(T)
    qr = _rope(qr.reshape(B, T, H_R, DK_R), pos)
    kr = _rope(kr.reshape(B, T, H_R, DK_R), pos) * (DK_R ** -0.5)
    orr, ret_new = _retention(qr, kr, vr.reshape(B, T, H_R, DV_R), ret_state)
    mu = jnp.mean(orr, -1, keepdims=True)
    var = jnp.mean(jnp.square(orr - mu), -1, keepdims=True)
    orr = (orr - mu) * lax.rsqrt(var + GN_EPS)
    ob = (orr.reshape(B, T, W_R) * jax.nn.silu(gr.astype(jnp.float32))).astype(x.dtype)
    merged = jax.nn.sigmoid(ga) * (oa @ w_pa) + jax.nn.sigmoid(gb) * (ob @ w_pb)
    x = _ln(ALPHA * x + g1 * (merged @ w_o), ln1_g, ln1_b)
    u2 = x * (1 + sc2) + sh2
    a, bv = jnp.split(u2 @ w_up, 2, axis=-1)
    a_ext = jnp.concatenate([conv_state.astype(a.dtype), a], axis=1)
    conv = conv_b + sum(a_ext[:, w:w + T] * conv_w[w] for w in range(CONV_W))
    h = jax.nn.gelu(conv) * bv
    x = _ln(ALPHA * x + g2 * (h @ w_down), ln2_g, ln2_b)
    return x, (ka, va, logf, ret_new, a_ext[:, a_ext.shape[1] - (CONV_W - 1):])


def setup_inputs(seed: int = 0) -> dict:
    key = jax.random.key(seed)
    ks = jax.random.split(key, 32)
    f32 = jnp.float32
    n_pages = PAST_LEN // PAGE_SIZE
    n_used = DEC_BATCH * n_pages
    n_pool = n_used + max(1, n_used // 4)

    def nrm(k, shape, scale=1.0):
        return jax.random.normal(k, shape, f32) * scale

    x_prompt = nrm(ks[0], (BATCH, SEQ, D_MODEL))
    x_sample = nrm(ks[1], (DEC_BATCH, DEC_SEQ, D_MODEL))
    cache_k = nrm(ks[2], (DEPTH, n_pool, PAGE_SIZE, H_A, DH_A))
    cache_v = nrm(ks[3], (DEPTH, n_pool, PAGE_SIZE, H_A, DH_A))
    cache_logf = jax.nn.log_sigmoid(3.0 + nrm(ks[4], (DEPTH, n_pool, PAGE_SIZE, H_A)))
    state_ret = nrm(ks[5], (DEPTH, DEC_BATCH, H_R, DK_R, DV_R), 0.3)
    state_conv = nrm(ks[6], (DEPTH, DEC_BATCH, CONV_W - 1, D_FF))
    page_table = jax.random.permutation(ks[7], n_pool)[:n_used].reshape(DEC_BATCH, n_pages).astype(jnp.int32)
    c_prompt = nrm(ks[8], (BATCH, D_MODEL))
    c_sample = nrm(ks[9], (DEC_BATCH, D_MODEL))
    w_ada = nrm(ks[10], (DEPTH, D_MODEL, 6 * D_MODEL), 0.5 * D_MODEL ** -0.5)
    b_ada = nrm(ks[11], (DEPTH, 6 * D_MODEL), 0.02)
    w_in = nrm(ks[12], (DEPTH, D_MODEL, N_IN), D_MODEL ** -0.5)
    b_f = 3.0 + nrm(ks[13], (DEPTH, H_A), 0.1)
    w_pa = nrm(ks[14], (DEPTH, W_A, D_MODEL), W_A ** -0.5)
    w_pb = nrm(ks[15], (DEPTH, W_R, D_MODEL), W_R ** -0.5)
    w_o = nrm(ks[16], (DEPTH, D_MODEL, D_MODEL), BETA * D_MODEL ** -0.5)
    ln1_g = 1.0 + nrm(ks[17], (DEPTH, D_MODEL), 0.02)
    ln1_b = nrm(ks[18], (DEPTH, D_MODEL), 0.02)
    w_up = nrm(ks[19], (DEPTH, D_MODEL, 2 * D_FF), D_MODEL ** -0.5)
    conv_w = nrm(ks[20], (DEPTH, CONV_W, D_FF), CONV_W ** -0.5)
    conv_b = nrm(ks[21], (DEPTH, D_FF), 0.02)
    w_down = nrm(ks[22], (DEPTH, D_FF, D_MODEL), BETA * D_FF ** -0.5)
    ln2_g = 1.0 + nrm(ks[23], (DEPTH, D_MODEL), 0.02)
    ln2_b = nrm(ks[24], (DEPTH, D_MODEL), 0.02)
    return {'x_prompt': x_prompt, 'x_sample': x_sample, 'cache_k': cache_k, 'cache_v': cache_v,
            'cache_logf': cache_logf, 'state_ret': state_ret, 'state_conv': state_conv,
            'page_table': page_table, 'c_prompt': c_prompt, 'c_sample': c_sample,
            'w_ada': w_ada, 'b_ada': b_ada, 'w_in': w_in, 'b_f': b_f, 'w_pa': w_pa, 'w_pb': w_pb,
            'w_o': w_o, 'ln1_g': ln1_g, 'ln1_b': ln1_b, 'w_up': w_up, 'conv_w': conv_w,
            'conv_b': conv_b, 'w_down': w_down, 'ln2_g': ln2_g, 'ln2_b': ln2_b}


def reference(x_prompt, x_sample, cache_k, cache_v, cache_logf, state_ret, state_conv, page_table,
              c_prompt, c_sample, w_ada, b_ada, w_in, b_f, w_pa, w_pb, w_o, ln1_g, ln1_b,
              w_up, conv_w, conv_b, w_down, ln2_g, ln2_b):
    B = x_prompt.shape[0]
    Bs = x_sample.shape[0]
    P = page_table.shape[1] * cache_k.shape[2]
    hp, hs = x_prompt, x_sample
    kp_l, vp_l, lp_l, rp_l, cp_l = [], [], [], [], []
    ks_l, vs_l, ls_l, rs_l, cs_l = [], [], [], [], []
    for l in range(DEPTH):
        w = (w_ada[l], b_ada[l], w_in[l], b_f[l], w_pa[l], w_pb[l], w_o[l], ln1_g[l], ln1_b[l],
             w_up[l], conv_w[l], conv_b[l], w_down[l], ln2_g[l], ln2_b[l])
        ret0 = jnp.zeros((B, H_R, DK_R, DV_R), jnp.float32)
        conv0 = jnp.zeros((B, CONV_W - 1, D_FF), hp.dtype)
        hp, (k1, v1, lf1, r1, cv1) = _layer(hp, c_prompt, 0, None, ret0, conv0, *w)
        past = (cache_k[l][page_table].reshape(Bs, P, H_A, DH_A),
                cache_v[l][page_table].reshape(Bs, P, H_A, DH_A),
                cache_logf[l][page_table].reshape(Bs, P, H_A))
        hs, (k2, v2, lf2, r2, cv2) = _layer(hs, c_sample, P, past, state_ret[l], state_conv[l], *w)
        kp_l.append(k1); vp_l.append(v1); lp_l.append(lf1); rp_l.append(r1); cp_l.append(cv1)
        ks_l.append(k2); vs_l.append(v2); ls_l.append(lf2); rs_l.append(r2); cs_l.append(cv2)
    y_prompt, y_sample = hp, hs
    k_prompt, v_prompt, logf_prompt = jnp.stack(kp_l), jnp.stack(vp_l), jnp.stack(lp_l)
    ret_prompt, conv_prompt = jnp.stack(rp_l), jnp.stack(cp_l)
    k_sample, v_sample, logf_sample = jnp.stack(ks_l), jnp.stack(vs_l), jnp.stack(ls_l)
    ret_sample, conv_sample = jnp.stack(rs_l), jnp.stack(cs_l)
    return (y_prompt, y_sample, k_prompt, v_prompt, logf_prompt, ret_prompt, conv_prompt,
            k_sample, v_sample, logf_sample, ret_sample, conv_sample)
```

```python
import functools
import math

import jax
import jax.numpy as jnp
from jax import lax
from jax.experimental import pallas as pl
from jax.experimental.pallas import tpu as pltpu

F32 = jnp.float32
BF16 = jnp.bfloat16

LANES = 128
SUBLANES = 8
VMEM_LIMIT = 56 * 1024 * 1024

ROPE_BASE = 10000.0
LN_EPS = 1e-5
GN_EPS = 1e-5
NEG_BIG = -1e30

_NT = (((1,), (1,)), ((), ()))
_TN = (((0,), (0,)), ((), ()))


def _params(*sem):
    return pltpu.CompilerParams(dimension_semantics=sem, vmem_limit_bytes=VMEM_LIMIT)


def _layer_norm(r, g, b):
    mu = jnp.mean(r, axis=-1, keepdims=True)
    d = r - mu
    var = jnp.mean(d * d, axis=-1, keepdims=True)
    return d * lax.rsqrt(var + LN_EPS) * g + b


def _ada_kernel(c_ref, w_ref, b_ref, o_ref):
    c = c_ref[...]
    s = (c * jax.nn.sigmoid(c)).astype(BF16)
    o_ref[...] = jnp.dot(s, w_ref[...].astype(BF16), preferred_element_type=F32) + b_ref[...]


def _ada(c_all, w_ada, b_ada):
    rows, d = c_all.shape
    n = w_ada.shape[1]
    tn = 512
    return pl.pallas_call(
        _ada_kernel,
        grid=(n // tn,),
        in_specs=[pl.BlockSpec((rows, d), lambda j: (0, 0)),
                  pl.BlockSpec((d, tn), lambda j: (0, j)),
                  pl.BlockSpec((1, tn), lambda j: (0, j))],
        out_specs=pl.BlockSpec((rows, tn), lambda j: (0, j)),
        out_shape=jax.ShapeDtypeStruct((rows, n), F32),
        compiler_params=_params("arbitrary"),
        name="ada_mod",
    )(c_all, w_ada, b_ada.reshape(1, n))


_G_QA, _G_QR, _G_KR, _G_VR = 0, 1, 2, 3
_G_F32_FIRST = 4
_N_GROUPS = 11
_BF_QA, _BF_QR, _BF_KR, _BF_VR = 0, 1, 2, 3
_F32_GA, _F32_GB, _F32_KA, _F32_VA, _F32_GR = 0, 2, 4, 5, 6


def _lane_cumsum(x, carry):
    rows, width = x.shape
    lane = lax.broadcasted_iota(jnp.int32, (rows, LANES), 1)
    out = []
    for c in range(width // LANES):
        v = x[:, c * LANES:(c + 1) * LANES]
        s = 1
        while s < LANES:
            v = v + jnp.where(lane >= s, pltpu.roll(v, s, axis=1), 0.0)
            s *= 2
        v = v + carry
        carry = jnp.broadcast_to(v[:, LANES - 1:LANES], (rows, LANES))
        out.append(v)
    return jnp.concatenate(out, axis=1), carry


def _inproj_kernel(x_ref, sc_ref, sh_ref, w_ref, wf_ref, bf_ref, cos_ref, sin_ref,
                   obf_ref, of32_ref, lf_ref, c_ref, u_sc, carry_sc,
                   *, tm, tiles_per_seq, qa_scale, kr_scale, n_heads):
    i = pl.program_id(0)
    j = pl.program_id(1)

    @pl.when(j == 0)
    def _():
        x = x_ref[...]
        u = (x * (1.0 + sc_ref[...]) + sh_ref[...]).reshape(tm, x.shape[-1]).astype(BF16)
        u_sc[...] = u
        z = lax.dot_general(wf_ref[...], u, _NT, preferred_element_type=F32) + bf_ref[...]
        lf = jnp.minimum(z, 0.0) - jnp.log1p(jnp.exp(-jnp.abs(z)))
        lf_ref[0] = lf
        first = (i % tiles_per_seq) == 0
        prev = jnp.where(first, 0.0, carry_sc[...])
        c, last = _lane_cumsum(lf, prev)
        c_ref[0] = c
        carry_sc[...] = last

    acc = jnp.dot(u_sc[...], w_ref[0], preferred_element_type=F32)

    @pl.when(j == _G_QA)
    def _():
        obf_ref[...] = (acc * qa_scale).astype(BF16)

    def rope(scale):
        cos = cos_ref[...]
        sin = sin_ref[...]
        for h in range(n_heads):
            sl = slice(h * LANES, (h + 1) * LANES)
            a = acc[:, sl]
            r = a * cos + pltpu.roll(a, LANES // 2, axis=1) * sin
            if scale is not None:
                r = r * scale
            obf_ref[:, sl] = r.astype(BF16)

    @pl.when(j == _G_QR)
    def _():
        rope(None)

    @pl.when(j == _G_KR)
    def _():
        rope(kr_scale)

    @pl.when(j == _G_VR)
    def _():
        obf_ref[...] = acc.astype(BF16)

    @pl.when(j >= _G_F32_FIRST)
    def _():
        of32_ref[...] = acc


def _inproj(x3, mod3, mod_group0, w_main, wf_t, bf_col, cos_t, sin_t, *, tm, seq_tiled):
    g, r, d = x3.shape
    m = g * r
    n_tiles = m // tm
    if seq_tiled:
        tps = r // tm
        gb = 1
        x_spec = pl.BlockSpec((1, tm, d), lambda i, j: (i // tps, i % tps, 0))
        mod_idx = lambda k: (lambda i, j: (mod_group0 + i // tps, 0, k))
        tab_spec = pl.BlockSpec((tm, LANES), lambda i, j: (i % tps, 0))
    else:
        tps = 1
        gb = tm // r
        x_spec = pl.BlockSpec((gb, r, d), lambda i, j: (i, 0, 0))
        mod_idx = lambda k: (lambda i, j: (mod_group0 // gb + i, 0, k))
        tab_spec = pl.BlockSpec((tm, LANES), lambda i, j: (0, 0))
    n_bf = _G_F32_FIRST
    kern = functools.partial(_inproj_kernel, tm=tm, tiles_per_seq=tps,
                             qa_scale=LANES ** -0.5, kr_scale=LANES ** -0.5, n_heads=1024 // LANES)
    return pl.pallas_call(
        kern,
        grid=(n_tiles, _N_GROUPS),
        in_specs=[x_spec,
                  pl.BlockSpec((gb, 1, d), mod_idx(1)),
                  pl.BlockSpec((gb, 1, d), mod_idx(0)),
                  pl.BlockSpec((1, d, 1024), lambda i, j: (j, 0, 0)),
                  pl.BlockSpec((16, d), lambda i, j: (0, 0)),
                  pl.BlockSpec((16, 1), lambda i, j: (0, 0)),
                  tab_spec, tab_spec],
        out_specs=[pl.BlockSpec((tm, 1024), lambda i, j: (i, jnp.minimum(j, n_bf - 1))),
                   pl.BlockSpec((tm, 1024), lambda i, j: (i, jnp.maximum(j - n_bf, 0))),
                   pl.BlockSpec((1, 16, tm), lambda i, j: (i, 0, 0)),
                   pl.BlockSpec((1, 16, tm), lambda i, j: (i, 0, 0))],
        out_shape=[jax.ShapeDtypeStruct((m, 1024 * n_bf), BF16),
                   jax.ShapeDtypeStruct((m, 1024 * (_N_GROUPS - n_bf)), F32),
                   jax.ShapeDtypeStruct((n_tiles, 16, tm), F32),
                   jax.ShapeDtypeStruct((n_tiles, 16, tm), F32)],
        scratch_shapes=[pltpu.VMEM((tm, d), BF16), pltpu.VMEM((16, LANES), F32)],
        compiler_params=_params("arbitrary", "arbitrary"),
        name="inproj",
    )(x3, mod3, mod3, w_main, wf_t, bf_col, cos_t, sin_t)


def _fox_prefill_kernel(q_ref, k_ref, v_ref, c_ref, o_ref, *, tq):
    h = pl.program_id(1)
    qi = pl.program_id(2)
    q = q_ref[0]
    dh = q.shape[-1]

    def chunk(kc, carry, diagonal):
        m, l, acc = carry
        start = pl.multiple_of(kc * tq, tq)
        kk = k_ref[0, pl.ds(start, tq), :].astype(BF16)
        vv = v_ref[0, pl.ds(start, tq), :].astype(BF16)
        s = lax.dot_general(q, kk, _NT, preferred_element_type=F32)
        s = s - c_ref[0, kc, pl.ds(h, 1), :]
        if diagonal:
            row = lax.broadcasted_iota(jnp.int32, (tq, tq), 0)
            col = lax.broadcasted_iota(jnp.int32, (tq, tq), 1)
            s = jnp.where(row >= col, s, NEG_BIG)
        m_new = jnp.maximum(m, jnp.max(s, axis=-1, keepdims=True))
        alpha = jnp.exp(m - m_new)
        p = jnp.exp(s - m_new)
        l = alpha * l + jnp.sum(p, axis=-1, keepdims=True)
        acc = alpha * acc + jnp.dot(p.astype(BF16), vv, preferred_element_type=F32)
        return m_new, l, acc

    init = (jnp.full((tq, 1), NEG_BIG, F32), jnp.zeros((tq, 1), F32), jnp.zeros((tq, dh), F32))
    carry = lax.fori_loop(0, qi, lambda kc, c: chunk(kc, c, False), init)
    _, l, acc = chunk(qi, carry, True)
    o_ref[0] = (acc / l).astype(BF16)


def _fox_prefill(pbf3, pf32_3, c4, *, n_heads, tq):
    b, t, _ = pbf3.shape
    return pl.pallas_call(
        functools.partial(_fox_prefill_kernel, tq=tq),
        grid=(b, n_heads, t // tq),
        in_specs=[pl.BlockSpec((1, tq, LANES), lambda bi, h, qi: (bi, qi, _BF_QA * 8 + h)),
                  pl.BlockSpec((1, t, LANES), lambda bi, h, qi: (bi, 0, _F32_KA * 8 + h)),
                  pl.BlockSpec((1, t, LANES), lambda bi, h, qi: (bi, 0, _F32_VA * 8 + h)),
                  pl.BlockSpec((1, t // tq, 16, tq), lambda bi, h, qi: (bi, 0, 0, 0))],
        out_specs=pl.BlockSpec((1, tq, LANES), lambda bi, h, qi: (bi, qi, h)),
        out_shape=jax.ShapeDtypeStruct((b, t, n_heads * LANES), BF16),
        compiler_params=_params("arbitrary", "arbitrary", "arbitrary"),
        name="fox_prefill",
    )(pbf3, pf32_3, pf32_3, c4)


def _periodic_tail(v, lane, n_heads):
    y = jnp.where(lane >= LANES - n_heads, v, 0.0)
    s = n_heads
    while s < LANES:
        y = y + pltpu.roll(y, LANES - s, axis=1)
        s *= 2
    return y


def _fox_decode_kernel(pt_ref, q_ref, kn_ref, vn_ref, lfn_ref, *rest, n_pages, n_heads, n_new):
    del pt_ref
    k_refs = rest[:n_pages]
    v_refs = rest[n_pages:2 * n_pages]
    lf_refs = rest[2 * n_pages:3 * n_pages]
    o_ref = rest[3 * n_pages]
    lf_sc = rest[3 * n_pages + 1]

    page_w = lf_refs[0].shape[-1]
    ppr = n_pages // SUBLANES
    n_chunks = ppr * page_w // LANES
    rows_q = n_new * n_heads

    for p in range(n_pages):
        r, part = divmod(p, ppr)
        lf_sc[r:r + 1, part * page_w:(part + 1) * page_w] = lf_refs[p][0]
    lane = lax.broadcasted_iota(jnp.int32, (SUBLANES, LANES), 1)
    sub = lax.broadcasted_iota(jnp.int32, (SUBLANES, LANES), 0)
    chunks = []
    carry = jnp.zeros((SUBLANES, LANES), F32)
    for c in range(n_chunks):
        v = lf_sc[:, c * LANES:(c + 1) * LANES]
        s = n_heads
        while s < LANES:
            v = v + jnp.where(lane >= s, pltpu.roll(v, s, axis=1), 0.0)
            s *= 2
        v = v + carry
        carry = _periodic_tail(v, lane, n_heads)
        chunks.append(v)
    inc = carry
    s = 1
    while s < SUBLANES:
        inc = inc + jnp.where(sub >= s, pltpu.roll(inc, s, axis=0), 0.0)
        s *= 2
    exc = jnp.where(sub >= 1, pltpu.roll(inc, 1, axis=0), 0.0)
    chunks = [v + exc for v in chunks]
    past_total = inc[SUBLANES - 1:SUBLANES, :]

    q = q_ref[0]
    dh = q.shape[-1]
    row_h = lax.broadcasted_iota(jnp.int32, (rows_q, page_w), 0) % n_heads
    col_h = lax.broadcasted_iota(jnp.int32, (rows_q, page_w), 1) % n_heads
    same_head = row_h == col_h

    m = jnp.full((rows_q, 1), NEG_BIG, F32)
    l = jnp.zeros((rows_q, 1), F32)
    acc = jnp.zeros((rows_q, dh), F32)
    cpp = page_w // LANES
    for p in range(n_pages):
        r, part = divmod(p, ppr)
        ck = jnp.concatenate([chunks[part * cpp + c][r:r + 1, :] for c in range(cpp)], axis=1)
        kk = k_refs[p][0].astype(BF16)
        vv = v_refs[p][0].astype(BF16)
        s = lax.dot_general(q, kk, _NT, preferred_element_type=F32)
        s = jnp.where(same_head, s - ck, NEG_BIG)
        m_new = jnp.maximum(m, jnp.max(s, axis=-1, keepdims=True))
        alpha = jnp.exp(m - m_new)
        pr = jnp.exp(s - m_new)
        l = alpha * l + jnp.sum(pr, axis=-1, keepdims=True)
        acc = alpha * acc + jnp.dot(pr.astype(BF16), vv, preferred_element_type=F32)
        m = m_new

    lane1 = lax.broadcasted_iota(jnp.int32, (1, LANES), 1)
    cn = lfn_ref[0]
    s = n_heads
    while s < rows_q:
        cn = cn + jnp.where(lane1 >= s, pltpu.roll(cn, s, axis=1), 0.0)
        s *= 2
    cn = cn + past_total
    kk = kn_ref[0].astype(BF16)
    vv = vn_ref[0].astype(BF16)
    s = lax.dot_general(q, kk, _NT, preferred_element_type=F32)
    row = lax.broadcasted_iota(jnp.int32, (rows_q, rows_q), 0)
    col = lax.broadcasted_iota(jnp.int32, (rows_q, rows_q), 1)
    ok = ((row % n_heads) == (col % n_heads)) & (col <= row)
    s = jnp.where(ok, s - cn[:, :rows_q], NEG_BIG)
    m_new = jnp.maximum(m, jnp.max(s, axis=-1, keepdims=True))
    alpha = jnp.exp(m - m_new)
    pr = jnp.exp(s - m_new)
    l = alpha * l + jnp.sum(pr, axis=-1, keepdims=True)
    acc = alpha * acc + jnp.dot(pr.astype(BF16), vv, preferred_element_type=F32)
    o_ref[0] = (acc / l).astype(BF16)


def _page_index(b, pt_ref, *, p, n_pages):
    return (pt_ref[b * n_pages + p], 0, 0)


def _fox_decode(page_table, q3, kn3, vn3, lfn3, ck3, cv3, clf3, *, n_heads, n_new):
    bs, n_pages = page_table.shape
    rows_q, dh = q3.shape[1], q3.shape[2]
    page_rows = ck3.shape[1]
    page_w = clf3.shape[2]

    def same(b, pt):
        return (b, 0, 0)

    def page_specs(shape):
        return [pl.BlockSpec(shape, functools.partial(_page_index, p=p, n_pages=n_pages))
                for p in range(n_pages)]

    grid_spec = pltpu.PrefetchScalarGridSpec(
        num_scalar_prefetch=1,
        grid=(bs,),
        in_specs=[pl.BlockSpec((1, rows_q, dh), same),
                  pl.BlockSpec((1, rows_q, dh), same),
                  pl.BlockSpec((1, rows_q, dh), same),
                  pl.BlockSpec((1, 1, LANES), same)]
                 + page_specs((1, page_rows, dh)) + page_specs((1, page_rows, dh))
                 + page_specs((1, 1, page_w)),
        out_specs=pl.BlockSpec((1, rows_q, dh), same),
        scratch_shapes=[pltpu.VMEM((SUBLANES, n_pages // SUBLANES * page_w), F32)],
    )
    return pl.pallas_call(
        functools.partial(_fox_decode_kernel, n_pages=n_pages, n_heads=n_heads, n_new=n_new),
        grid_spec=grid_spec,
        out_shape=jax.ShapeDtypeStruct((bs, rows_q, dh), BF16),
        compiler_params=_params("arbitrary"),
        name="fox_decode",
    )(page_table.reshape(-1), q3, kn3, vn3, lfn3,
      *([ck3] * n_pages), *([cv3] * n_pages), *([clf3] * n_pages))


def _ret_kernel(*refs, n_heads, has_state):
    if has_state:
        (q_ref, k_ref, v_ref, g_ref, dmat_ref, qdec_ref, kdec_ref, sdec_ref, s0_ref,
         o_ref, sout_ref, s_sc) = refs
    else:
        (q_ref, k_ref, v_ref, g_ref, dmat_ref, qdec_ref, kdec_ref, sdec_ref,
         o_ref, sout_ref, s_sc) = refs
    c = pl.program_id(1)

    @pl.when(c == 0)
    def _():
        if has_state:
            s_sc[...] = s0_ref[0]
        else:
            s_sc[...] = jnp.zeros_like(s_sc)

    for h in range(n_heads):
        sl = slice(h * LANES, (h + 1) * LANES)
        q = q_ref[0, :, sl]
        k = k_ref[0, :, sl]
        v = v_ref[0, :, sl]
        st = s_sc[h]
        att = lax.dot_general(q, k, _NT, preferred_element_type=F32) * dmat_ref[h]
        inner = jnp.dot(att.astype(BF16), v, preferred_element_type=F32)
        cross = jnp.dot(q, st.astype(BF16), preferred_element_type=F32) * qdec_ref[h]
        o = inner + cross
        kw = (k.astype(F32) * kdec_ref[h]).astype(BF16)
        s_sc[h] = sdec_ref[h] * st + lax.dot_general(kw, v, _TN, preferred_element_type=F32)
        mu = jnp.mean(o, axis=-1, keepdims=True)
        d = o - mu
        var = jnp.mean(d * d, axis=-1, keepdims=True)
        g = g_ref[0, :, sl]
        o_ref[0, :, sl] = (d * lax.rsqrt(var + GN_EPS) * (g * jax.nn.sigmoid(g))).astype(BF16)

    @pl.when(c == pl.num_programs(1) - 1)
    def _():
        sout_ref[0] = s_sc[...]


def _retention_tables(n_heads, chunk, dk):
    lg = jnp.log(1.0 - jnp.exp2(-5.0 - jnp.arange(n_heads, dtype=F32)))
    idx = jnp.arange(chunk, dtype=F32)
    diff = idx[:, None] - idx[None, :]
    dmat = jnp.where(diff[None] >= 0, jnp.exp(diff[None] * lg[:, None, None]), 0.0)
    qdec = jnp.exp((idx + 1.0)[None, :] * lg[:, None])
    kdec = jnp.exp((chunk - 1.0 - idx)[None, :] * lg[:, None])
    sdec = jnp.exp(chunk * lg)
    bc = lambda a: jnp.broadcast_to(a[:, :, None], a.shape + (dk,))
    return dmat, bc(qdec), bc(kdec), jnp.broadcast_to(sdec[:, None, None], (n_heads, 1, dk))


def _retention(pbf3, pf32_3, state, *, n_heads, chunk):
    b, t, _ = pbf3.shape
    w = n_heads * LANES
    n_chunks = t // chunk
    tabs = _retention_tables(n_heads, chunk, LANES)
    has_state = state is not None
    const3 = lambda bi, c: (0, 0, 0)
    in_specs = [pl.BlockSpec((1, chunk, w), lambda bi, c: (bi, c, _BF_QR)),
                pl.BlockSpec((1, chunk, w), lambda bi, c: (bi, c, _BF_KR)),
                pl.BlockSpec((1, chunk, w), lambda bi, c: (bi, c, _BF_VR)),
                pl.BlockSpec((1, chunk, w), lambda bi, c: (bi, c, _F32_GR)),
                pl.BlockSpec((n_heads, chunk, chunk), const3),
                pl.BlockSpec((n_heads, chunk, LANES), const3),
                pl.BlockSpec((n_heads, chunk, LANES), const3),
                pl.BlockSpec((n_heads, 1, LANES), const3)]
    args = [pbf3, pbf3, pbf3, pf32_3, *tabs]
    if has_state:
        in_specs.append(pl.BlockSpec((1, n_heads, LANES, LANES), lambda bi, c: (bi, 0, 0, 0)))
        args.append(state)
    return pl.pallas_call(
        functools.partial(_ret_kernel, n_heads=n_heads, has_state=has_state),
        grid=(b, n_chunks),
        in_specs=in_specs,
        out_specs=[pl.BlockSpec((1, chunk, w), lambda bi, c: (bi, c, 0)),
                   pl.BlockSpec((1, n_heads, LANES, LANES), lambda bi, c: (bi, 0, 0, 0))],
        out_shape=[jax.ShapeDtypeStruct((b, t, w), BF16),
                   jax.ShapeDtypeStruct((b, n_heads, LANES, LANES), F32)],
        scratch_shapes=[pltpu.VMEM((n_heads, LANES, LANES), F32)],
        compiler_params=_params("arbitrary", "arbitrary"),
        name="retention",
    )(*args)


def _merge_kernel(oa_ref, ob_ref, ga_ref, gb_ref, x_ref, g1_ref, wpa_ref, wpb_ref, wo_ref,
                  lng_ref, lnb_ref, o_ref, *, alpha):
    a = jnp.dot(oa_ref[...], wpa_ref[...], preferred_element_type=F32)
    b = jnp.dot(ob_ref[...], wpb_ref[...], preferred_element_type=F32)
    merged = jax.nn.sigmoid(ga_ref[...]) * a + jax.nn.sigmoid(gb_ref[...]) * b
    y = jnp.dot(merged.astype(BF16), wo_ref[...], preferred_element_type=F32)
    x = x_ref[...]
    r = alpha * x + g1_ref[...] * y.reshape(x.shape)
    o_ref[...] = _layer_norm(r, lng_ref[...], lnb_ref[...])


def _row_tiling(x3, mod_group0, tm, n_grid_axes):
    g, r, d = x3.shape
    if r >= tm:
        tps = r // tm
        gb = 1
        if n_grid_axes == 1:
            x_map = lambda i: (i // tps, i % tps, 0)
            mod_idx = lambda k: (lambda i: (mod_group0 + i // tps, 0, k))
        else:
            x_map = lambda i, j: (i // tps, i % tps, 0)
            mod_idx = lambda k: (lambda i, j: (mod_group0 + i // tps, 0, k))
        x_spec = pl.BlockSpec((1, tm, d), x_map)
    else:
        tps = 1
        gb = tm // r
        if n_grid_axes == 1:
            x_map = lambda i: (i, 0, 0)
            mod_idx = lambda k: (lambda i: (mod_group0 // gb + i, 0, k))
        else:
            x_map = lambda i, j: (i, 0, 0)
            mod_idx = lambda k: (lambda i, j: (mod_group0 // gb + i, 0, k))
        x_spec = pl.BlockSpec((gb, r, d), x_map)
    mod_spec = lambda k: pl.BlockSpec((gb, 1, d), mod_idx(k))
    return x_spec, mod_spec, tps, gb


def _merge(oa2, ob2, pf32, x3, mod3, mod_group0, wpa, wpb, wo, lng, lnb, *, tm, alpha):
    g, r, d = x3.shape
    m = g * r
    wa = oa2.shape[1]
    x_spec, mod_spec, _, _ = _row_tiling(x3, mod_group0, tm, 1)
    const = lambda i: (0, 0)
    return pl.pallas_call(
        functools.partial(_merge_kernel, alpha=alpha),
        grid=(m // tm,),
        in_specs=[pl.BlockSpec((tm, wa), lambda i: (i, 0)),
                  pl.BlockSpec((tm, wa), lambda i: (i, 0)),
                  pl.BlockSpec((tm, d), lambda i: (i, _F32_GA // 2)),
                  pl.BlockSpec((tm, d), lambda i: (i, _F32_GB // 2)),
                  x_spec,
                  mod_spec(2),
                  pl.BlockSpec(wpa.shape, const, pipeline_mode=pl.Buffered(1)),
                  pl.BlockSpec(wpb.shape, const, pipeline_mode=pl.Buffered(1)),
                  pl.BlockSpec(wo.shape, const, pipeline_mode=pl.Buffered(1)),
                  pl.BlockSpec((1, d), const),
                  pl.BlockSpec((1, d), const)],
        out_specs=x_spec,
        out_shape=jax.ShapeDtypeStruct(x3.shape, F32),
        compiler_params=_params("arbitrary"),
        name="merge_outproj",
    )(oa2, ob2, pf32, pf32, x3, mod3, wpa, wpb, wo, lng, lnb)


def _ffn_kernel(*refs, tm, tiles_per_seq, alpha, with_state, conv_w):
    if with_state:
        (x_ref, sc_ref, sh_ref, g2_ref, wa_ref, wb_ref, wd_ref, cw_ref, cb_ref, lng_ref, lnb_ref,
         st_ref, o_ref, tail_ref, u_sc, acc_sc, abuf) = refs
    else:
        (x_ref, sc_ref, sh_ref, g2_ref, wa_ref, wb_ref, wd_ref, cw_ref, cb_ref, lng_ref, lnb_ref,
         o_ref, tail_ref, u_sc, acc_sc, abuf, carry_sc) = refs
    i = pl.program_id(0)
    j = pl.program_id(1)
    hist = conv_w - 1

    @pl.when(j == 0)
    def _():
        x = x_ref[...]
        u_sc[...] = (x * (1.0 + sc_ref[...]) + sh_ref[...]).reshape(tm, x.shape[-1]).astype(BF16)
        acc_sc[...] = jnp.zeros_like(acc_sc)

    u = u_sc[...]
    a = jnp.dot(u, wa_ref[...], preferred_element_type=F32)
    b = jnp.dot(u, wb_ref[...], preferred_element_type=F32)
    tf = a.shape[-1]
    cw = cw_ref[...]
    if with_state:
        gb = tm // SUBLANES
        a3 = a.reshape(gb, SUBLANES, tf)
        abuf[:, SUBLANES:2 * SUBLANES, :] = a3
        abuf[:, SUBLANES - hist:SUBLANES, :] = st_ref[...]
        shifted = [abuf[:, SUBLANES - hist + w:2 * SUBLANES - hist + w, :].reshape(tm, tf)
                   for w in range(hist)]
        tail_ref[...] = a3[:, SUBLANES - hist:, :]
    else:
        first = (i % tiles_per_seq) == 0
        abuf[0:SUBLANES, :] = jnp.where(first, 0.0, carry_sc[j])
        abuf[SUBLANES:SUBLANES + tm, :] = a
        shifted = [abuf[SUBLANES - hist + w:SUBLANES - hist + w + tm, :] for w in range(hist)]
        carry_sc[j] = a[tm - SUBLANES:, :]
        tail_ref[0] = a[tm - hist:, :]
    conv = a * cw[hist:hist + 1, :]
    for w in range(hist):
        conv = conv + shifted[w] * cw[w:w + 1, :]
    conv = conv + cb_ref[...]
    hg = (jax.nn.gelu(conv) * b).astype(BF16)
    acc_sc[...] += jnp.dot(hg, wd_ref[...], preferred_element_type=F32)

    @pl.when(j == pl.num_programs(1) - 1)
    def _():
        x = x_ref[...]
        r = alpha * x + g2_ref[...] * acc_sc[...].reshape(x.shape)
        o_ref[...] = _layer_norm(r, lng_ref[...], lnb_ref[...])


def _ffn(x3, mod3, mod_group0, wup_a, wup_b, wdown, conv_w, conv_b, lng, lnb, conv_state,
         *, tm, tf, alpha):
    g, r, d = x3.shape
    m = g * r
    dff = wup_a.shape[1]
    nj = dff // tf
    cwid = conv_w.shape[0]
    hist = cwid - 1
    x_spec, mod_spec, tps, gb = _row_tiling(x3, mod_group0, tm, 2)
    with_state = conv_state is not None
    const = lambda i, j: (0, 0)
    in_specs = [x_spec, mod_spec(4), mod_spec(3), mod_spec(5),
                pl.BlockSpec((d, tf), lambda i, j: (0, j)),
                pl.BlockSpec((d, tf), lambda i, j: (0, j)),
                pl.BlockSpec((tf, d), lambda i, j: (j, 0)),
                pl.BlockSpec((cwid, tf), lambda i, j: (0, j)),
                pl.BlockSpec((1, tf), lambda i, j: (0, j)),
                pl.BlockSpec((1, d), const),
                pl.BlockSpec((1, d), const)]
    args = [x3, mod3, mod3, mod3, wup_a, wup_b, wdown, conv_w, conv_b.reshape(1, dff), lng, lnb]
    scratch = [pltpu.VMEM((tm, d), BF16), pltpu.VMEM((tm, d), F32)]
    if with_state:
        assert r == SUBLANES
        in_specs.append(pl.BlockSpec((gb, hist, tf), lambda i, j: (i, 0, j)))
        args.append(conv_state)
        tail_spec = pl.BlockSpec((gb, hist, tf), lambda i, j: (i, 0, j))
        scratch.append(pltpu.VMEM((gb, 2 * SUBLANES, tf), F32))
    else:
        tail_spec = pl.BlockSpec((1, hist, tf), lambda i, j: (i, 0, j))
        scratch += [pltpu.VMEM((tm + SUBLANES, tf), F32), pltpu.VMEM((nj, SUBLANES, tf), F32)]
    n_tails = g if with_state else m // tm
    kern = functools.partial(_ffn_kernel, tm=tm, tiles_per_seq=tps, alpha=alpha,
                             with_state=with_state, conv_w=cwid)
    y, tails = pl.pallas_call(
        kern,
        grid=(m // tm, nj),
        in_specs=in_specs,
        out_specs=[x_spec, tail_spec],
        out_shape=[jax.ShapeDtypeStruct(x3.shape, F32),
                   jax.ShapeDtypeStruct((n_tails, hist, dff), F32)],
        scratch_shapes=scratch,
        compiler_params=_params("arbitrary", "arbitrary"),
        name="convffn",
    )(*args)
    if not with_state:
        tails = tails.reshape(g, tps, hist, dff)[:, tps - 1]
    return y, tails


def _rope_tables(pos, half):
    inv = ROPE_BASE ** (-jnp.arange(half, dtype=F32) / half)
    ang = pos.astype(F32)[:, None] * inv[None, :]
    cos, sin = jnp.cos(ang), jnp.sin(ang)
    return jnp.concatenate([cos, cos], axis=-1), jnp.concatenate([-sin, sin], axis=-1)


def kernel(x_prompt, x_sample, cache_k, cache_v, cache_logf, state_ret, state_conv, page_table,
           c_prompt, c_sample, w_ada, b_ada, w_in, b_f, w_pa, w_pb, w_o, ln1_g, ln1_b,
           w_up, conv_w, conv_b, w_down, ln2_g, ln2_b):
    depth = w_ada.shape[0]
    b, t, d = x_prompt.shape
    bs, ts, _ = x_sample.shape
    n_pool, page_size, n_heads, dh = cache_k.shape[1:]
    n_pages = page_table.shape[1]
    past = n_pages * page_size
    wa = n_heads * dh
    dff = w_down.shape[1]
    alpha = (2.0 * depth) ** 0.25
    assert wa == 1024 and dh == LANES and ts == SUBLANES and n_pages % SUBLANES == 0
    assert w_in.shape[2] == 7 * wa + n_heads + 2 * d and d == 2 * wa

    tm_p, tm_s = 1024, bs * ts
    tq = 512
    ret_chunk = 128 if t % 128 == 0 else t

    cos_p, sin_p = _rope_tables(jnp.arange(t), dh // 2)
    cos_s, sin_s = _rope_tables(past + jnp.arange(ts), dh // 2)
    cos_s, sin_s = jnp.tile(cos_s, (tm_s // ts, 1)), jnp.tile(sin_s, (tm_s // ts, 1))

    n_mod = -(-(bs + b) // 16) * 16
    c_all = jnp.concatenate([c_sample, c_prompt, jnp.zeros((n_mod - bs - b, d), F32)], axis=0)

    hp, hs = x_prompt, x_sample
    outs = {k: [] for k in ("kp", "vp", "lp", "rp", "cp", "ks", "vs", "ls", "rs", "cs")}
    for l in range(depth):
        mod3 = _ada(c_all, w_ada[l], b_ada[l]).reshape(n_mod, 1, 6 * d)

        wi = w_in[l]
        o = 0
        cols = {}
        for name, width in (("qa", wa), ("ka", wa), ("va", wa), ("fa", n_heads), ("qr", wa),
                            ("kr", wa), ("vr", wa), ("gr", wa), ("ga", d), ("gb", d)):
            cols[name] = wi[:, o:o + width]
            o += width
        groups = [cols["qa"], cols["qr"], cols["kr"], cols["vr"],
                  cols["ga"][:, :wa], cols["ga"][:, wa:], cols["gb"][:, :wa], cols["gb"][:, wa:],
                  cols["ka"], cols["va"], cols["gr"]]
        w_main = jnp.stack(groups, axis=0).astype(BF16)
        wf_t = jnp.zeros((16, d), F32).at[:n_heads].set(cols["fa"].T).astype(BF16)
        bf_col = jnp.zeros((16, 1), F32).at[:n_heads, 0].set(b_f[l].astype(F32))
        wpa, wpb, wo = w_pa[l].astype(BF16), w_pb[l].astype(BF16), w_o[l].astype(BF16)
        wup = w_up[l]
        wup_a, wup_b = wup[:, :dff].astype(BF16), wup[:, dff:].astype(BF16)
        wdown = w_down[l].astype(BF16)
        lng1, lnb1 = ln1_g[l].reshape(1, d), ln1_b[l].reshape(1, d)
        lng2, lnb2 = ln2_g[l].reshape(1, d), ln2_b[l].reshape(1, d)

        pbf, pf32, lf_t, c_t = _inproj(hp, mod3, bs, w_main, wf_t, bf_col, cos_p, sin_p,
                                       tm=tm_p, seq_tiled=True)
        pbf3 = pbf.reshape(b, t, -1)
        pf32_3 = pf32.reshape(b, t, -1)
        nt = t // tm_p
        c4 = (c_t.reshape(b, nt, 16, tm_p // tq, tq).transpose(0, 1, 3, 2, 4)
              .reshape(b, t // tq, 16, tq))
        oa = _fox_prefill(pbf3, pf32_3, c4, n_heads=n_heads, tq=tq)
        ob, ret_p = _retention(pbf3, pf32_3, None, n_heads=n_heads, chunk=ret_chunk)
        x1 = _merge(oa.reshape(b * t, wa), ob.reshape(b * t, wa), pf32, hp, mod3, bs,
                    wpa, wpb, wo, lng1, lnb1, tm=256, alpha=alpha)
        hp, conv_p = _ffn(x1, mod3, bs, wup_a, wup_b, wdown, conv_w[l], conv_b[l], lng2, lnb2,
                          None, tm=512, tf=512, alpha=alpha)
        outs["kp"].append(pf32_3[:, :, _F32_KA * wa:(_F32_KA + 1) * wa].reshape(b, t, n_heads, dh))
        outs["vp"].append(pf32_3[:, :, _F32_VA * wa:(_F32_VA + 1) * wa].reshape(b, t, n_heads, dh))
        lf_p = lf_t[:, :n_heads, :].reshape(b, nt, n_heads, tm_p)
        outs["lp"].append(lf_p.transpose(0, 1, 3, 2).reshape(b, t, n_heads))
        outs["rp"].append(ret_p)
        outs["cp"].append(conv_p)

        sbf, sf32, lfs_t, _ = _inproj(hs, mod3, 0, w_main, wf_t, bf_col, cos_s, sin_s,
                                      tm=tm_s, seq_tiled=False)
        rows_q = ts * n_heads
        q3 = sbf[:, _BF_QA * wa:(_BF_QA + 1) * wa].reshape(bs, rows_q, dh)
        k_new = sf32[:, _F32_KA * wa:(_F32_KA + 1) * wa].reshape(bs, ts, n_heads, dh)
        v_new = sf32[:, _F32_VA * wa:(_F32_VA + 1) * wa].reshape(bs, ts, n_heads, dh)
        lf_s = lfs_t[0, :n_heads, :].T.reshape(bs, ts, n_heads)
        lfn3 = jnp.pad(lf_s.reshape(bs, 1, rows_q), ((0, 0), (0, 0), (0, LANES - rows_q)))
        ck3 = cache_k[l].reshape(n_pool, page_size * n_heads, dh)
        cv3 = cache_v[l].reshape(n_pool, page_size * n_heads, dh)
        clf3 = cache_logf[l].astype(F32).reshape(n_pool, 1, page_size * n_heads)
        oa_s = _fox_decode(page_table, q3, k_new.reshape(bs, rows_q, dh),
                           v_new.reshape(bs, rows_q, dh), lfn3, ck3, cv3, clf3,
                           n_heads=n_heads, n_new=ts)
        sbf3 = sbf.reshape(bs, ts, -1)
        sf32_3 = sf32.reshape(bs, ts, -1)
        ob_s, ret_s = _retention(sbf3, sf32_3, state_ret[l], n_heads=n_heads, chunk=ts)
        x1s = _merge(oa_s.reshape(bs * ts, wa), ob_s.reshape(bs * ts, wa), sf32, hs, mod3, 0,
                     wpa, wpb, wo, lng1, lnb1, tm=256, alpha=alpha)
        hs, conv_s = _ffn(x1s, mod3, 0, wup_a, wup_b, wdown, conv_w[l], conv_b[l], lng2, lnb2,
                          state_conv[l], tm=512, tf=512, alpha=alpha)
        outs["ks"].append(k_new)
        outs["vs"].append(v_new)
        outs["ls"].append(lf_s)
        outs["rs"].append(ret_s)
        outs["cs"].append(conv_s)

    st = lambda name: jnp.stack(outs[name])
    return (hp, hs, st("kp"), st("vp"), st("lp"), st("rp"), st("cp"),
            st("ks"), st("vs"), st("ls"), st("rs"), st("cs"))
```

```python
import functools
import math

import jax
import jax.numpy as jnp
from jax import lax
from jax.experimental import pallas as pl
from jax.experimental.pallas import tpu as pltpu

F32 = jnp.float32
BF16 = jnp.bfloat16

LANES = 128
SUBLANES = 8
VMEM_LIMIT = 56 * 1024 * 1024

ROPE_BASE = 10000.0
LN_EPS = 1e-5
GN_EPS = 1e-5
NEG_BIG = -1e30

_NT = (((1,), (1,)), ((), ()))
_TN = (((0,), (0,)), ((), ()))


def _params(*sem):
    return pltpu.CompilerParams(dimension_semantics=sem, vmem_limit_bytes=VMEM_LIMIT)


def _layer_norm(r, g, b):
    mu = jnp.mean(r, axis=-1, keepdims=True)
    d = r - mu
    var = jnp.mean(d * d, axis=-1, keepdims=True)
    return d * lax.rsqrt(var + LN_EPS) * g + b


def _ada_kernel(c_ref, w_ref, b_ref, o_ref):
    c = c_ref[...]
    s = (c * jax.nn.sigmoid(c)).astype(BF16)
    o_ref[...] = jnp.dot(s, w_ref[...].astype(BF16), preferred_element_type=F32) + b_ref[...]


def _ada(c_all, w_ada, b_ada):
    rows, d = c_all.shape
    n = w_ada.shape[1]
    tn = 512
    return pl.pallas_call(
        _ada_kernel,
        grid=(n // tn,),
        in_specs=[pl.BlockSpec((rows, d), lambda j: (0, 0)),
                  pl.BlockSpec((d, tn), lambda j: (0, j)),
                  pl.BlockSpec((1, tn), lambda j: (0, j))],
        out_specs=pl.BlockSpec((rows, tn), lambda j: (0, j)),
        out_shape=jax.ShapeDtypeStruct((rows, n), F32),
        compiler_params=_params("arbitrary"),
        name="ada_mod",
    )(c_all, w_ada, b_ada.reshape(1, n))


(_G_QA, _G_KA, _G_VA, _G_QR, _G_KR, _G_VR, _G_GA0, _G_GA1, _G_GB0, _G_GB1, _G_GR) = range(11)
_N_GROUPS = 11
_W_BLOCK_OF_GR = 6
_BF_QA, _BF_QR, _BF_KR, _BF_VR = 0, 1, 2, 3
_F32_KA, _F32_VA, _F32_GA, _F32_GB, _F32_GR = 0, 1, 2, 4, 6


def _w_block(j):
    return jnp.where(j < _G_GA0, j, jnp.where(j < _G_GR, j + 1, _W_BLOCK_OF_GR))


def _bf_block(j):
    return jnp.clip(j - (_G_QR - _BF_QR), _BF_QA, _BF_VR)


def _f32_block(j):
    early = jnp.maximum(j - _G_KA, 0)
    late = jnp.clip(j - (_G_GA0 - _F32_GA), _F32_GA, _F32_GR)
    return jnp.where(j <= _G_VA, early, late)


def _repack_kernel(w_ref, o_ref, *, fa0, n_f):
    w = w_ref[...]
    o_ref[:, :fa0] = w[:, :fa0].astype(BF16)
    o_ref[:, fa0:] = w[:, fa0 + n_f:].astype(BF16)


def _repack_w_in(wi, fa0, n_f):
    d, n = wi.shape
    tr = 128
    return pl.pallas_call(
        functools.partial(_repack_kernel, fa0=fa0, n_f=n_f),
        grid=(d // tr,),
        in_specs=[pl.BlockSpec((tr, n), lambda i: (i, 0))],
        out_specs=pl.BlockSpec((tr, n - n_f), lambda i: (i, 0)),
        out_shape=jax.ShapeDtypeStruct((d, n - n_f), BF16),
        compiler_params=_params("arbitrary"),
        name="repack_w_in",
    )(wi)


def _lane_cumsum(x, carry):
    rows, width = x.shape
    lane = lax.broadcasted_iota(jnp.int32, (rows, LANES), 1)
    out = []
    for c in range(width // LANES):
        v = x[:, c * LANES:(c + 1) * LANES]
        s = 1
        while s < LANES:
            v = v + jnp.where(lane >= s, pltpu.roll(v, s, axis=1), 0.0)
            s *= 2
        out.append(v + carry)
        carry = carry + jnp.broadcast_to(v[:, LANES - 1:LANES], (rows, LANES))
    return jnp.concatenate(out, axis=1), carry


def _inproj_kernel(x_ref, sc_ref, sh_ref, w_ref, wf_ref, bf_ref, cos_ref, sin_ref,
                   obf_ref, of32_ref, lf_ref, c_ref, u_sc, carry_sc,
                   *, tm, tiles_per_seq, qa_scale, kr_scale, n_heads):
    i = pl.program_id(0)
    j = pl.program_id(1)

    @pl.when(j == 0)
    def _():
        x = x_ref[...]
        u = (x * (1.0 + sc_ref[...]) + sh_ref[...]).reshape(tm, x.shape[-1]).astype(BF16)
        u_sc[...] = u
        z = lax.dot_general(wf_ref[...].astype(BF16), u, _NT,
                            preferred_element_type=F32) + bf_ref[...]
        lf = jnp.minimum(z, 0.0) - jnp.log1p(jnp.exp(-jnp.abs(z)))
        lf_ref[0] = lf
        first = (i % tiles_per_seq) == 0
        prev = jnp.where(first, 0.0, carry_sc[...])
        c, last = _lane_cumsum(lf, prev)
        c_ref[0] = c
        carry_sc[...] = last

    def project():
        return jnp.dot(u_sc[...], w_ref[...], preferred_element_type=F32)

    @pl.when(j == _G_QA)
    def _():
        obf_ref[...] = (project() * qa_scale).astype(BF16)

    def rope(scale):
        acc = project()
        cos = cos_ref[...]
        sin = sin_ref[...]
        for h in range(n_heads):
            sl = slice(h * LANES, (h + 1) * LANES)
            a = acc[:, sl]
            r = a * cos + pltpu.roll(a, LANES // 2, axis=1) * sin
            if scale is not None:
                r = r * scale
            obf_ref[:, sl] = r.astype(BF16)

    @pl.when(j == _G_QR)
    def _():
        rope(None)

    @pl.when(j == _G_KR)
    def _():
        rope(kr_scale)

    @pl.when(j == _G_VR)
    def _():
        obf_ref[...] = project().astype(BF16)

    @pl.when((j == _G_KA) | (j == _G_VA) | (j >= _G_GA0))
    def _():
        of32_ref[...] = project()


def _inproj(x3, mod3, mod_group0, w_main, wf_t, bf_col, cos_t, sin_t, *, tm, seq_tiled):
    g, r, d = x3.shape
    m = g * r
    n_tiles = m // tm
    if seq_tiled:
        tps = r // tm
        gb = 1
        x_spec = pl.BlockSpec((1, tm, d), lambda i, j: (i // tps, i % tps, 0))
        mod_idx = lambda k: (lambda i, j: (mod_group0 + i // tps, 0, k))
        tab_spec = pl.BlockSpec((tm, LANES), lambda i, j: (i % tps, 0))
    else:
        tps = 1
        gb = tm // r
        x_spec = pl.BlockSpec((gb, r, d), lambda i, j: (i, 0, 0))
        mod_idx = lambda k: (lambda i, j: (mod_group0 // gb + i, 0, k))
        tab_spec = pl.BlockSpec((tm, LANES), lambda i, j: (0, 0))
    n_bf = _BF_VR + 1
    kern = functools.partial(_inproj_kernel, tm=tm, tiles_per_seq=tps,
                             qa_scale=LANES ** -0.5, kr_scale=LANES ** -0.5, n_heads=1024 // LANES)
    return pl.pallas_call(
        kern,
        grid=(n_tiles, _N_GROUPS),
        in_specs=[x_spec,
                  pl.BlockSpec((gb, 1, d), mod_idx(1)),
                  pl.BlockSpec((gb, 1, d), mod_idx(0)),
                  pl.BlockSpec((d, 1024), lambda i, j: (0, _w_block(j))),
                  pl.BlockSpec((16, d), lambda i, j: (0, 0)),
                  pl.BlockSpec((16, 1), lambda i, j: (0, 0)),
                  tab_spec, tab_spec],
        out_specs=[pl.BlockSpec((tm, 1024), lambda i, j: (i, _bf_block(j))),
                   pl.BlockSpec((tm, 1024), lambda i, j: (i, _f32_block(j))),
                   pl.BlockSpec((1, 16, tm), lambda i, j: (i, 0, 0)),
                   pl.BlockSpec((1, 16, tm), lambda i, j: (i, 0, 0))],
        out_shape=[jax.ShapeDtypeStruct((m, 1024 * n_bf), BF16),
                   jax.ShapeDtypeStruct((m, 1024 * (_N_GROUPS - n_bf)), F32),
                   jax.ShapeDtypeStruct((n_tiles, 16, tm), F32),
                   jax.ShapeDtypeStruct((n_tiles, 16, tm), F32)],
        scratch_shapes=[pltpu.VMEM((tm, d), BF16), pltpu.VMEM((16, LANES), F32)],
        compiler_params=_params("arbitrary", "arbitrary"),
        name="inproj",
    )(x3, mod3, mod3, w_main, wf_t, bf_col, cos_t, sin_t)


def _fox_prefill_kernel(q_ref, k_ref, v_ref, c_ref, o_ref, *, tq, hpb):
    hg = pl.program_id(1)
    qi = pl.program_id(2)
    dh = LANES

    def chunk(kc, carry, diagonal):
        start = pl.multiple_of(kc * tq, tq)
        out = []
        for hh in range(hpb):
            m, l, acc = carry[hh]
            sl = slice(hh * dh, (hh + 1) * dh)
            kk = k_ref[0, pl.ds(start, tq), sl].astype(BF16)
            vv = v_ref[0, pl.ds(start, tq), sl].astype(BF16)
            s = lax.dot_general(q_ref[0, :, sl], kk, _NT, preferred_element_type=F32)
            s = s - c_ref[0, kc, pl.ds(hg * hpb + hh, 1), :]
            if diagonal:
                row = lax.broadcasted_iota(jnp.int32, (tq, tq), 0)
                col = lax.broadcasted_iota(jnp.int32, (tq, tq), 1)
                s = jnp.where(row >= col, s, NEG_BIG)
            m_new = jnp.maximum(m, jnp.max(s, axis=-1, keepdims=True))
            alpha = jnp.exp(m - m_new)
            p = jnp.exp(s - m_new)
            l = alpha * l + jnp.sum(p, axis=-1, keepdims=True)
            acc = alpha * acc + jnp.dot(p.astype(BF16), vv, preferred_element_type=F32)
            out.append((m_new, l, acc))
        return tuple(out)

    init = tuple((jnp.full((tq, 1), NEG_BIG, F32), jnp.zeros((tq, 1), F32),
                  jnp.zeros((tq, dh), F32)) for _ in range(hpb))
    carry = lax.fori_loop(0, qi, lambda kc, c: chunk(kc, c, False), init)
    carry = chunk(qi, carry, True)
    for hh in range(hpb):
        _, l, acc = carry[hh]
        o_ref[0, :, hh * dh:(hh + 1) * dh] = (acc / l).astype(BF16)


def _fox_prefill(pbf3, pf32_3, c4, *, n_heads, tq, hpb):
    b, t, _ = pbf3.shape
    w = hpb * LANES
    nb = n_heads // hpb
    return pl.pallas_call(
        functools.partial(_fox_prefill_kernel, tq=tq, hpb=hpb),
        grid=(b, nb, t // tq),
        in_specs=[pl.BlockSpec((1, tq, w), lambda bi, h, qi: (bi, qi, _BF_QA * nb + h)),
                  pl.BlockSpec((1, t, w), lambda bi, h, qi: (bi, 0, _F32_KA * nb + h)),
                  pl.BlockSpec((1, t, w), lambda bi, h, qi: (bi, 0, _F32_VA * nb + h)),
                  pl.BlockSpec((1, t // tq, 16, tq), lambda bi, h, qi: (bi, 0, 0, 0))],
        out_specs=pl.BlockSpec((1, tq, w), lambda bi, h, qi: (bi, qi, h)),
        out_shape=jax.ShapeDtypeStruct((b, t, n_heads * LANES), BF16),
        compiler_params=_params("arbitrary", "arbitrary", "arbitrary"),
        name="fox_prefill",
    )(pbf3, pf32_3, pf32_3, c4)


def _periodic_tail(v, lane, n_heads):
    y = jnp.where(lane >= LANES - n_heads, v, 0.0)
    s = n_heads
    while s < LANES:
        y = y + pltpu.roll(y, LANES - s, axis=1)
        s *= 2
    return y


def _page_copies(pt_ref, ck_hbm, cv_hbm, clf_hbm, kbuf, vbuf, lfbuf, sems, seq, slot, n_pages):
    copies = []
    for p in range(n_pages):
        page = pt_ref[seq * n_pages + p]
        copies.append(pltpu.make_async_copy(ck_hbm.at[page], kbuf.at[slot, p], sems.at[slot, 0]))
        copies.append(pltpu.make_async_copy(cv_hbm.at[page], vbuf.at[slot, p], sems.at[slot, 1]))
        copies.append(pltpu.make_async_copy(clf_hbm.at[page], lfbuf.at[slot, p], sems.at[slot, 2]))
    return copies


def _fox_decode_kernel(pt_ref, q_ref, kn_ref, vn_ref, lfn_ref, ck_hbm, cv_hbm, clf_hbm, o_ref,
                       kbuf, vbuf, lfbuf, sems, lf_sc, s_sc, *, n_pages, n_heads, n_new):
    b = pl.program_id(0)
    slot = b % 2
    copies = functools.partial(_page_copies, pt_ref, ck_hbm, cv_hbm, clf_hbm, kbuf, vbuf, lfbuf,
                               sems, n_pages=n_pages)

    @pl.when(b == 0)
    def _():
        for cp in copies(seq=0, slot=0):
            cp.start()

    @pl.when(b + 1 < pl.num_programs(0))
    def _():
        for cp in copies(seq=b + 1, slot=1 - slot):
            cp.start()

    for cp in copies(seq=b, slot=slot):
        cp.wait()

    k_refs = [kbuf.at[slot, p] for p in range(n_pages)]
    v_refs = [vbuf.at[slot, p] for p in range(n_pages)]
    lf_refs = [lfbuf.at[slot, p] for p in range(n_pages)]

    page_w = lfbuf.shape[-1]
    ppr = n_pages // SUBLANES
    n_chunks = ppr * page_w // LANES
    rows_q = n_new * n_heads

    for p in range(n_pages):
        r, part = divmod(p, ppr)
        lf_sc[r:r + 1, part * page_w:(part + 1) * page_w] = lf_refs[p][...]
    lane = lax.broadcasted_iota(jnp.int32, (SUBLANES, LANES), 1)
    sub = lax.broadcasted_iota(jnp.int32, (SUBLANES, LANES), 0)
    chunks = []
    carry = jnp.zeros((SUBLANES, LANES), F32)
    for c in range(n_chunks):
        v = lf_sc[:, c * LANES:(c + 1) * LANES]
        s = n_heads
        while s < LANES:
            v = v + jnp.where(lane >= s, pltpu.roll(v, s, axis=1), 0.0)
            s *= 2
        chunks.append(v + carry)
        carry = carry + _periodic_tail(v, lane, n_heads)
    inc = carry
    s = 1
    while s < SUBLANES:
        inc = inc + jnp.where(sub >= s, pltpu.roll(inc, s, axis=0), 0.0)
        s *= 2
    exc = jnp.where(sub >= 1, pltpu.roll(inc, 1, axis=0), 0.0)
    chunks = [v + exc for v in chunks]
    past_total = inc[SUBLANES - 1:SUBLANES, :]

    q = q_ref[0]
    dh = q.shape[-1]
    row_h = lax.broadcasted_iota(jnp.int32, (rows_q, page_w), 0) % n_heads
    col_h = lax.broadcasted_iota(jnp.int32, (rows_q, page_w), 1) % n_heads
    same_head = row_h == col_h

    cpp = page_w // LANES
    mx = jnp.full((rows_q, LANES), NEG_BIG, F32)
    for p in range(n_pages):
        r, part = divmod(p, ppr)
        ck = jnp.concatenate([chunks[part * cpp + c][r:r + 1, :] for c in range(cpp)], axis=1)
        kk = k_refs[p][...].astype(BF16)
        s = lax.dot_general(q, kk, _NT, preferred_element_type=F32)
        s = jnp.where(same_head, s - ck, NEG_BIG)
        s_sc[p] = s
        for c in range(cpp):
            mx = jnp.maximum(mx, s[:, c * LANES:(c + 1) * LANES])

    lane1 = lax.broadcasted_iota(jnp.int32, (1, LANES), 1)
    cn = lfn_ref[0]
    s = n_heads
    while s < rows_q:
        cn = cn + jnp.where(lane1 >= s, pltpu.roll(cn, s, axis=1), 0.0)
        s *= 2
    cn = cn + past_total
    s_new = lax.dot_general(q, kn_ref[0].astype(BF16), _NT, preferred_element_type=F32)
    row = lax.broadcasted_iota(jnp.int32, (rows_q, rows_q), 0)
    col = lax.broadcasted_iota(jnp.int32, (rows_q, rows_q), 1)
    ok = ((row % n_heads) == (col % n_heads)) & (col <= row)
    s_new = jnp.where(ok, s_new - cn[:, :rows_q], NEG_BIG)
    m = jnp.maximum(jnp.max(mx, axis=-1, keepdims=True), jnp.max(s_new, axis=-1, keepdims=True))

    lsum = jnp.zeros((rows_q, LANES), F32)
    acc = jnp.zeros((rows_q, dh), F32)
    for p in range(n_pages):
        pr = jnp.exp(s_sc[p] - m)
        for c in range(cpp):
            lsum = lsum + pr[:, c * LANES:(c + 1) * LANES]
        acc = acc + jnp.dot(pr.astype(BF16), v_refs[p][...].astype(BF16),
                            preferred_element_type=F32)
    pr = jnp.exp(s_new - m)
    l = jnp.sum(lsum, axis=-1, keepdims=True) + jnp.sum(pr, axis=-1, keepdims=True)
    acc = acc + jnp.dot(pr.astype(BF16), vn_ref[0].astype(BF16), preferred_element_type=F32)
    o_ref[0] = (acc / l).astype(BF16)


def _fox_decode(page_table, q3, kn3, vn3, lfn3, ck3, cv3, clf3, *, n_heads, n_new):
    bs, n_pages = page_table.shape
    rows_q, dh = q3.shape[1], q3.shape[2]
    page_rows = ck3.shape[1]
    page_w = clf3.shape[2]

    def same(b, pt):
        return (b, 0, 0)

    hbm = pl.BlockSpec(memory_space=pl.ANY)
    grid_spec = pltpu.PrefetchScalarGridSpec(
        num_scalar_prefetch=1,
        grid=(bs,),
        in_specs=[pl.BlockSpec((1, rows_q, dh), same),
                  pl.BlockSpec((1, rows_q, dh), same),
                  pl.BlockSpec((1, rows_q, dh), same),
                  pl.BlockSpec((1, 1, LANES), same),
                  hbm, hbm, hbm],
        out_specs=pl.BlockSpec((1, rows_q, dh), same),
        scratch_shapes=[pltpu.VMEM((2, n_pages, page_rows, dh), F32),
                        pltpu.VMEM((2, n_pages, page_rows, dh), F32),
                        pltpu.VMEM((2, n_pages, 1, page_w), F32),
                        pltpu.SemaphoreType.DMA((2, 3)),
                        pltpu.VMEM((SUBLANES, n_pages // SUBLANES * page_w), F32),
                        pltpu.VMEM((n_pages, rows_q, page_w), F32)],
    )
    return pl.pallas_call(
        functools.partial(_fox_decode_kernel, n_pages=n_pages, n_heads=n_heads, n_new=n_new),
        grid_spec=grid_spec,
        out_shape=jax.ShapeDtypeStruct((bs, rows_q, dh), BF16),
        compiler_params=_params("arbitrary"),
        name="fox_decode",
    )(page_table.reshape(-1), q3, kn3, vn3, lfn3, ck3, cv3, clf3)


def _ret_kernel(*refs, n_heads, has_state):
    if has_state:
        (q_ref, k_ref, v_ref, g_ref, dmat_ref, qdec_ref, kdec_ref, sdec_ref, s0_ref,
         o_ref, sout_ref, s_sc) = refs
    else:
        (q_ref, k_ref, v_ref, g_ref, dmat_ref, qdec_ref, kdec_ref, sdec_ref,
         o_ref, sout_ref, s_sc) = refs
    c = pl.program_id(1)

    @pl.when(c == 0)
    def _():
        if has_state:
            s_sc[...] = s0_ref[0]
        else:
            s_sc[...] = jnp.zeros_like(s_sc)

    for h in range(n_heads):
        sl = slice(h * LANES, (h + 1) * LANES)
        q = q_ref[0, :, sl]
        k = k_ref[0, :, sl]
        v = v_ref[0, :, sl]
        st = s_sc[h]
        att = lax.dot_general(q, k, _NT, preferred_element_type=F32) * dmat_ref[h]
        inner = jnp.dot(att.astype(BF16), v, preferred_element_type=F32)
        cross = jnp.dot(q, st.astype(BF16), preferred_element_type=F32) * qdec_ref[h]
        o = inner + cross
        kw = (k.astype(F32) * kdec_ref[h]).astype(BF16)
        s_sc[h] = sdec_ref[h] * st + lax.dot_general(kw, v, _TN, preferred_element_type=F32)
        mu = jnp.mean(o, axis=-1, keepdims=True)
        d = o - mu
        var = jnp.mean(d * d, axis=-1, keepdims=True)
        g = g_ref[0, :, sl]
        o_ref[0, :, sl] = (d * lax.rsqrt(var + GN_EPS) * (g * jax.nn.sigmoid(g))).astype(BF16)

    @pl.when(c == pl.num_programs(1) - 1)
    def _():
        sout_ref[0] = s_sc[...]


def _retention_tables(n_heads, chunk, dk):
    lg = jnp.log(1.0 - jnp.exp2(-5.0 - jnp.arange(n_heads, dtype=F32)))
    idx = jnp.arange(chunk, dtype=F32)
    diff = idx[:, None] - idx[None, :]
    dmat = jnp.where(diff[None] >= 0, jnp.exp(diff[None] * lg[:, None, None]), 0.0)
    qdec = jnp.exp((idx + 1.0)[None, :] * lg[:, None])
    kdec = jnp.exp((chunk - 1.0 - idx)[None, :] * lg[:, None])
    sdec = jnp.exp(chunk * lg)
    bc = lambda a: jnp.broadcast_to(a[:, :, None], a.shape + (dk,))
    return dmat, bc(qdec), bc(kdec), jnp.broadcast_to(sdec[:, None, None], (n_heads, 1, dk))


def _retention(pbf3, pf32_3, state, *, n_heads, chunk):
    b, t, _ = pbf3.shape
    w = n_heads * LANES
    n_chunks = t // chunk
    tabs = _retention_tables(n_heads, chunk, LANES)
    has_state = state is not None
    const3 = lambda bi, c: (0, 0, 0)
    in_specs = [pl.BlockSpec((1, chunk, w), lambda bi, c: (bi, c, _BF_QR)),
                pl.BlockSpec((1, chunk, w), lambda bi, c: (bi, c, _BF_KR)),
                pl.BlockSpec((1, chunk, w), lambda bi, c: (bi, c, _BF_VR)),
                pl.BlockSpec((1, chunk, w), lambda bi, c: (bi, c, _F32_GR)),
                pl.BlockSpec((n_heads, chunk, chunk), const3),
                pl.BlockSpec((n_heads, chunk, LANES), const3),
                pl.BlockSpec((n_heads, chunk, LANES), const3),
                pl.BlockSpec((n_heads, 1, LANES), const3)]
    args = [pbf3, pbf3, pbf3, pf32_3, *tabs]
    if has_state:
        in_specs.append(pl.BlockSpec((1, n_heads, LANES, LANES), lambda bi, c: (bi, 0, 0, 0)))
        args.append(state)
    return pl.pallas_call(
        functools.partial(_ret_kernel, n_heads=n_heads, has_state=has_state),
        grid=(b, n_chunks),
        in_specs=in_specs,
        out_specs=[pl.BlockSpec((1, chunk, w), lambda bi, c: (bi, c, 0)),
                   pl.BlockSpec((1, n_heads, LANES, LANES), lambda bi, c: (bi, 0, 0, 0))],
        out_shape=[jax.ShapeDtypeStruct((b, t, w), BF16),
                   jax.ShapeDtypeStruct((b, n_heads, LANES, LANES), F32)],
        scratch_shapes=[pltpu.VMEM((n_heads, LANES, LANES), F32)],
        compiler_params=_params("arbitrary", "arbitrary"),
        name="retention",
    )(*args)


def _merge_kernel(oa_ref, ob_ref, ga_ref, gb_ref, x_ref, g1_ref, wpa_ref, wpb_ref, wo_ref,
                  lng_ref, lnb_ref, o_ref, *, alpha):
    a = jnp.dot(oa_ref[...], wpa_ref[...], preferred_element_type=F32)
    b = jnp.dot(ob_ref[...], wpb_ref[...], preferred_element_type=F32)
    merged = jax.nn.sigmoid(ga_ref[...]) * a + jax.nn.sigmoid(gb_ref[...]) * b
    y = jnp.dot(merged.astype(BF16), wo_ref[...], preferred_element_type=F32)
    x = x_ref[...]
    r = alpha * x + g1_ref[...] * y.reshape(x.shape)
    o_ref[...] = _layer_norm(r, lng_ref[...], lnb_ref[...])


def _row_tiling(x3, mod_group0, tm, n_grid_axes):
    g, r, d = x3.shape
    if r >= tm:
        tps = r // tm
        gb = 1
        if n_grid_axes == 1:
            x_map = lambda i: (i // tps, i % tps, 0)
            mod_idx = lambda k: (lambda i: (mod_group0 + i // tps, 0, k))
        else:
            x_map = lambda i, j: (i // tps, i % tps, 0)
            mod_idx = lambda k: (lambda i, j: (mod_group0 + i // tps, 0, k))
        x_spec = pl.BlockSpec((1, tm, d), x_map)
    else:
        tps = 1
        gb = tm // r
        if n_grid_axes == 1:
            x_map = lambda i: (i, 0, 0)
            mod_idx = lambda k: (lambda i: (mod_group0 // gb + i, 0, k))
        else:
            x_map = lambda i, j: (i, 0, 0)
            mod_idx = lambda k: (lambda i, j: (mod_group0 // gb + i, 0, k))
        x_spec = pl.BlockSpec((gb, r, d), x_map)
    mod_spec = lambda k: pl.BlockSpec((gb, 1, d), mod_idx(k))
    return x_spec, mod_spec, tps, gb


def _merge(oa2, ob2, pf32, x3, mod3, mod_group0, wpa, wpb, wo, lng, lnb, *, tm, alpha):
    g, r, d = x3.shape
    m = g * r
    wa = oa2.shape[1]
    x_spec, mod_spec, _, _ = _row_tiling(x3, mod_group0, tm, 1)
    const = lambda i: (0, 0)
    return pl.pallas_call(
        functools.partial(_merge_kernel, alpha=alpha),
        grid=(m // tm,),
        in_specs=[pl.BlockSpec((tm, wa), lambda i: (i, 0)),
                  pl.BlockSpec((tm, wa), lambda i: (i, 0)),
                  pl.BlockSpec((tm, d), lambda i: (i, _F32_GA // 2)),
                  pl.BlockSpec((tm, d), lambda i: (i, _F32_GB // 2)),
                  x_spec,
                  mod_spec(2),
                  pl.BlockSpec(wpa.shape, const, pipeline_mode=pl.Buffered(1)),
                  pl.BlockSpec(wpb.shape, const, pipeline_mode=pl.Buffered(1)),
                  pl.BlockSpec(wo.shape, const, pipeline_mode=pl.Buffered(1)),
                  pl.BlockSpec((1, d), const),
                  pl.BlockSpec((1, d), const)],
        out_specs=x_spec,
        out_shape=jax.ShapeDtypeStruct(x3.shape, F32),
        compiler_params=_params("arbitrary"),
        name="merge_outproj",
    )(oa2, ob2, pf32, pf32, x3, mod3, wpa, wpb, wo, lng, lnb)


def _ffn_kernel(*refs, tm, tiles_per_seq, alpha, with_state, conv_w):
    if with_state:
        (x_ref, sc_ref, sh_ref, g2_ref, wa_ref, wb_ref, wd_ref, cw_ref, cb_ref, lng_ref, lnb_ref,
         st_ref, o_ref, tail_ref, u_sc, acc_sc, abuf) = refs
    else:
        (x_ref, sc_ref, sh_ref, g2_ref, wa_ref, wb_ref, wd_ref, cw_ref, cb_ref, lng_ref, lnb_ref,
         o_ref, tail_ref, u_sc, acc_sc, abuf, carry_sc) = refs
    i = pl.program_id(0)
    j = pl.program_id(1)
    hist = conv_w - 1

    @pl.when(j == 0)
    def _():
        x = x_ref[...]
        u_sc[...] = (x * (1.0 + sc_ref[...]) + sh_ref[...]).reshape(tm, x.shape[-1]).astype(BF16)
        acc_sc[...] = jnp.zeros_like(acc_sc)

    u = u_sc[...]
    a = jnp.dot(u, wa_ref[...], preferred_element_type=F32)
    b = jnp.dot(u, wb_ref[...], preferred_element_type=F32)
    tf = a.shape[-1]
    cw = cw_ref[...]
    if with_state:
        gb = tm // SUBLANES
        a3 = a.reshape(gb, SUBLANES, tf)
        abuf[:, SUBLANES:2 * SUBLANES, :] = a3
        abuf[:, SUBLANES - hist:SUBLANES, :] = st_ref[...]
        shifted = [abuf[:, SUBLANES - hist + w:2 * SUBLANES - hist + w, :].reshape(tm, tf)
                   for w in range(hist)]
        tail_ref[...] = a3[:, SUBLANES - hist:, :]
    else:
        first = (i % tiles_per_seq) == 0
        abuf[0:SUBLANES, :] = jnp.where(first, 0.0, carry_sc[j])
        abuf[SUBLANES:SUBLANES + tm, :] = a
        shifted = [abuf[SUBLANES - hist + w:SUBLANES - hist + w + tm, :] for w in range(hist)]
        carry_sc[j] = a[tm - SUBLANES:, :]
        tail_ref[0] = a[tm - hist:, :]
    conv = a * cw[hist:hist + 1, :]
    for w in range(hist):
        conv = conv + shifted[w] * cw[w:w + 1, :]
    conv = conv + cb_ref[...]
    hg = (jax.nn.gelu(conv) * b).astype(BF16)
    acc_sc[...] += jnp.dot(hg, wd_ref[...], preferred_element_type=F32)

    @pl.when(j == pl.num_programs(1) - 1)
    def _():
        x = x_ref[...]
        r = alpha * x + g2_ref[...] * acc_sc[...].reshape(x.shape)
        o_ref[...] = _layer_norm(r, lng_ref[...], lnb_ref[...])


def _ffn(x3, mod3, mod_group0, wup, wdown, conv_w, conv_b, lng, lnb, conv_state,
         *, tm, tf, alpha):
    g, r, d = x3.shape
    m = g * r
    dff = wdown.shape[0]
    nj = dff // tf
    cwid = conv_w.shape[0]
    hist = cwid - 1
    x_spec, mod_spec, tps, gb = _row_tiling(x3, mod_group0, tm, 2)
    with_state = conv_state is not None
    const = lambda i, j: (0, 0)
    in_specs = [x_spec, mod_spec(4), mod_spec(3), mod_spec(5),
                pl.BlockSpec((d, tf), lambda i, j: (0, j)),
                pl.BlockSpec((d, tf), lambda i, j: (0, nj + j)),
                pl.BlockSpec((tf, d), lambda i, j: (j, 0)),
                pl.BlockSpec((cwid, tf), lambda i, j: (0, j)),
                pl.BlockSpec((1, tf), lambda i, j: (0, j)),
                pl.BlockSpec((1, d), const),
                pl.BlockSpec((1, d), const)]
    args = [x3, mod3, mod3, mod3, wup, wup, wdown, conv_w, conv_b.reshape(1, dff), lng, lnb]
    scratch = [pltpu.VMEM((tm, d), BF16), pltpu.VMEM((tm, d), F32)]
    if with_state:
        assert r == SUBLANES
        in_specs.append(pl.BlockSpec((gb, hist, tf), lambda i, j: (i, 0, j)))
        args.append(conv_state)
        tail_spec = pl.BlockSpec((gb, hist, tf), lambda i, j: (i, 0, j))
        scratch.append(pltpu.VMEM((gb, 2 * SUBLANES, tf), F32))
    else:
        tail_spec = pl.BlockSpec((1, hist, tf), lambda i, j: (i, 0, j))
        scratch += [pltpu.VMEM((tm + SUBLANES, tf), F32), pltpu.VMEM((nj, SUBLANES, tf), F32)]
    n_tails = g if with_state else m // tm
    kern = functools.partial(_ffn_kernel, tm=tm, tiles_per_seq=tps, alpha=alpha,
                             with_state=with_state, conv_w=cwid)
    y, tails = pl.pallas_call(
        kern,
        grid=(m // tm, nj),
        in_specs=in_specs,
        out_specs=[x_spec, tail_spec],
        out_shape=[jax.ShapeDtypeStruct(x3.shape, F32),
                   jax.ShapeDtypeStruct((n_tails, hist, dff), F32)],
        scratch_shapes=scratch,
        compiler_params=_params("arbitrary", "arbitrary"),
        name="convffn",
    )(*args)
    if not with_state:
        tails = tails.reshape(g, tps, hist, dff)[:, tps - 1]
    return y, tails


def _rope_tables(pos, half):
    inv = ROPE_BASE ** (-jnp.arange(half, dtype=F32) / half)
    ang = pos.astype(F32)[:, None] * inv[None, :]
    cos, sin = jnp.cos(ang), jnp.sin(ang)
    return jnp.concatenate([cos, cos], axis=-1), jnp.concatenate([-sin, sin], axis=-1)


def kernel(x_prompt, x_sample, cache_k, cache_v, cache_logf, state_ret, state_conv, page_table,
           c_prompt, c_sample, w_ada, b_ada, w_in, b_f, w_pa, w_pb, w_o, ln1_g, ln1_b,
           w_up, conv_w, conv_b, w_down, ln2_g, ln2_b):
    depth = w_ada.shape[0]
    b, t, d = x_prompt.shape
    bs, ts, _ = x_sample.shape
    n_pool, page_size, n_heads, dh = cache_k.shape[1:]
    n_pages = page_table.shape[1]
    past = n_pages * page_size
    wa = n_heads * dh
    dff = w_down.shape[1]
    alpha = (2.0 * depth) ** 0.25
    assert wa == 1024 and dh == LANES and ts == SUBLANES and n_pages % SUBLANES == 0
    assert w_in.shape[2] == 7 * wa + n_heads + 2 * d and d == 2 * wa

    tm_p, tm_s = 1024, bs * ts
    tq = 512
    ret_chunk = 128 if t % 128 == 0 else t

    cos_p, sin_p = _rope_tables(jnp.arange(t), dh // 2)
    cos_s, sin_s = _rope_tables(past + jnp.arange(ts), dh // 2)
    cos_s, sin_s = jnp.tile(cos_s, (tm_s // ts, 1)), jnp.tile(sin_s, (tm_s // ts, 1))

    n_mod = -(-(bs + b) // 16) * 16
    c_all = jnp.concatenate([c_sample, c_prompt, jnp.zeros((n_mod - bs - b, d), F32)], axis=0)

    hp, hs = x_prompt, x_sample
    outs = {k: [] for k in ("kp", "vp", "lp", "rp", "cp", "ks", "vs", "ls", "rs", "cs")}
    for l in range(depth):
        mod3 = _ada(c_all, w_ada[l], b_ada[l]).reshape(n_mod, 1, 6 * d)

        wi = w_in[l]
        fa0 = 3 * wa
        w_main = _repack_w_in(wi, fa0, n_heads)
        wf_t = jnp.zeros((16, d), F32).at[:n_heads].set(wi[:, fa0:fa0 + n_heads].T)
        bf_col = jnp.zeros((16, 1), F32).at[:n_heads, 0].set(b_f[l].astype(F32))
        wpa, wpb, wo = w_pa[l].astype(BF16), w_pb[l].astype(BF16), w_o[l].astype(BF16)
        wup = w_up[l].astype(BF16)
        wdown = w_down[l].astype(BF16)
        lng1, lnb1 = ln1_g[l].reshape(1, d), ln1_b[l].reshape(1, d)
        lng2, lnb2 = ln2_g[l].reshape(1, d), ln2_b[l].reshape(1, d)

        pbf, pf32, lf_t, c_t = _inproj(hp, mod3, bs, w_main, wf_t, bf_col, cos_p, sin_p,
                                       tm=tm_p, seq_tiled=True)
        pbf3 = pbf.reshape(b, t, -1)
        pf32_3 = pf32.reshape(b, t, -1)
        nt = t // tm_p
        c4 = (c_t.reshape(b, nt, 16, tm_p // tq, tq).transpose(0, 1, 3, 2, 4)
              .reshape(b, t // tq, 16, tq))
        oa = _fox_prefill(pbf3, pf32_3, c4, n_heads=n_heads, tq=tq, hpb=2)
        ob, ret_p = _retention(pbf3, pf32_3, None, n_heads=n_heads, chunk=ret_chunk)
        x1 = _merge(oa.reshape(b * t, wa), ob.reshape(b * t, wa), pf32, hp, mod3, bs,
                    wpa, wpb, wo, lng1, lnb1, tm=256, alpha=alpha)
        hp, conv_p = _ffn(x1, mod3, bs, wup, wdown, conv_w[l], conv_b[l], lng2, lnb2,
                          None, tm=512, tf=512, alpha=alpha)
        outs["kp"].append(pf32_3[:, :, _F32_KA * wa:(_F32_KA + 1) * wa].reshape(b, t, n_heads, dh))
        outs["vp"].append(pf32_3[:, :, _F32_VA * wa:(_F32_VA + 1) * wa].reshape(b, t, n_heads, dh))
        lf_p = lf_t[:, :n_heads, :].reshape(b, nt, n_heads, tm_p)
        outs["lp"].append(lf_p.transpose(0, 1, 3, 2).reshape(b, t, n_heads))
        outs["rp"].append(ret_p)
        outs["cp"].append(conv_p)

        sbf, sf32, lfs_t, _ = _inproj(hs, mod3, 0, w_main, wf_t, bf_col, cos_s, sin_s,
                                      tm=tm_s, seq_tiled=False)
        rows_q = ts * n_heads
        q3 = sbf[:, _BF_QA * wa:(_BF_QA + 1) * wa].reshape(bs, rows_q, dh)
        k_new = sf32[:, _F32_KA * wa:(_F32_KA + 1) * wa].reshape(bs, ts, n_heads, dh)
        v_new = sf32[:, _F32_VA * wa:(_F32_VA + 1) * wa].reshape(bs, ts, n_heads, dh)
        lf_s = lfs_t[0, :n_heads, :].T.reshape(bs, ts, n_heads)
        lfn3 = jnp.pad(lf_s.reshape(bs, 1, rows_q), ((0, 0), (0, 0), (0, LANES - rows_q)))
        ck3 = cache_k[l].reshape(n_pool, page_size * n_heads, dh)
        cv3 = cache_v[l].reshape(n_pool, page_size * n_heads, dh)
        clf3 = cache_logf[l].astype(F32).reshape(n_pool, 1, page_size * n_heads)
        oa_s = _fox_decode(page_table, q3, k_new.reshape(bs, rows_q, dh),
                           v_new.reshape(bs, rows_q, dh), lfn3, ck3, cv3, clf3,
                           n_heads=n_heads, n_new=ts)
        sbf3 = sbf.reshape(bs, ts, -1)
        sf32_3 = sf32.reshape(bs, ts, -1)
        ob_s, ret_s = _retention(sbf3, sf32_3, state_ret[l], n_heads=n_heads, chunk=ts)
        x1s = _merge(oa_s.reshape(bs * ts, wa), ob_s.reshape(bs * ts, wa), sf32, hs, mod3, 0,
                     wpa, wpb, wo, lng1, lnb1, tm=256, alpha=alpha)
        hs, conv_s = _ffn(x1s, mod3, 0, wup, wdown, conv_w[l], conv_b[l], lng2, lnb2,
                          state_conv[l], tm=512, tf=512, alpha=alpha)
        outs["ks"].append(k_new)
        outs["vs"].append(v_new)
        outs["ls"].append(lf_s)
        outs["rs"].append(ret_s)
        outs["cs"].append(conv_s)

    st = lambda name: jnp.stack(outs[name])
    return (hp, hs, st("kp"), st("vp"), st("lp"), st("rp"), st("cp"),
            st("ks"), st("vs"), st("ls"), st("rs"), st("cs"))
```

```python
import functools
import math

import jax
import jax.numpy as jnp
from jax import lax
from jax.experimental import pallas as pl
from jax.experimental.pallas import tpu as pltpu

F32 = jnp.float32
BF16 = jnp.bfloat16

LANES = 128
SUBLANES = 8
VMEM_LIMIT = 56 * 1024 * 1024

ROPE_BASE = 10000.0
LN_EPS = 1e-5
GN_EPS = 1e-5
NEG_BIG = -1e30

_NT = (((1,), (1,)), ((), ()))
_TN = (((0,), (0,)), ((), ()))


def _params(*sem):
    return pltpu.CompilerParams(dimension_semantics=sem, vmem_limit_bytes=VMEM_LIMIT)


def _layer_norm(r, g, b):
    mu = jnp.mean(r, axis=-1, keepdims=True)
    d = r - mu
    var = jnp.mean(d * d, axis=-1, keepdims=True)
    return d * lax.rsqrt(var + LN_EPS) * g + b


def _ada_kernel(c_ref, w_ref, b_ref, o_ref):
    c = c_ref[...]
    s = (c * jax.nn.sigmoid(c)).astype(BF16)
    o_ref[...] = jnp.dot(s, w_ref[...].astype(BF16), preferred_element_type=F32) + b_ref[...]


def _ada(c_all, w_ada, b_ada):
    rows, d = c_all.shape
    n = w_ada.shape[1]
    tn = 512
    return pl.pallas_call(
        _ada_kernel,
        grid=(n // tn,),
        in_specs=[pl.BlockSpec((rows, d), lambda j: (0, 0)),
                  pl.BlockSpec((d, tn), lambda j: (0, j)),
                  pl.BlockSpec((1, tn), lambda j: (0, j))],
        out_specs=pl.BlockSpec((rows, tn), lambda j: (0, j)),
        out_shape=jax.ShapeDtypeStruct((rows, n), F32),
        compiler_params=_params("arbitrary"),
        name="ada_mod",
    )(c_all, w_ada, b_ada.reshape(1, n))


(_G_QA, _G_KA, _G_VA, _G_QR, _G_KR, _G_VR, _G_GA0, _G_GA1, _G_GB0, _G_GB1, _G_GR) = range(11)
_N_GROUPS = 11
_W_BLOCK_OF_GR = 6
_BF_QA, _BF_QR, _BF_KR, _BF_VR = 0, 1, 2, 3
_F32_KA, _F32_VA, _F32_GA, _F32_GB, _F32_GR = 0, 1, 2, 4, 6


def _w_block(j):
    return jnp.where(j < _G_GA0, j, jnp.where(j < _G_GR, j + 1, _W_BLOCK_OF_GR))


def _bf_block(j):
    return jnp.clip(j - (_G_QR - _BF_QR), _BF_QA, _BF_VR)


def _f32_block(j):
    early = jnp.maximum(j - _G_KA, 0)
    late = jnp.clip(j - (_G_GA0 - _F32_GA), _F32_GA, _F32_GR)
    return jnp.where(j <= _G_VA, early, late)


def _repack_kernel(w_ref, o_ref, *, fa0, n_f):
    w = w_ref[0]
    o_ref[:, :fa0] = w[:, :fa0].astype(BF16)
    o_ref[:, fa0:] = w[:, fa0 + n_f:].astype(BF16)


def _repack_w_in(w_in, layer, fa0, n_f):
    _, d, n = w_in.shape
    tr = 128
    return pl.pallas_call(
        functools.partial(_repack_kernel, fa0=fa0, n_f=n_f),
        grid=(d // tr,),
        in_specs=[pl.BlockSpec((1, tr, n), lambda i: (layer, i, 0))],
        out_specs=pl.BlockSpec((tr, n - n_f), lambda i: (i, 0)),
        out_shape=jax.ShapeDtypeStruct((d, n - n_f), BF16),
        compiler_params=_params("arbitrary"),
        name="repack_w_in",
    )(w_in)


def _lane_cumsum(x, carry):
    rows, width = x.shape
    lane = lax.broadcasted_iota(jnp.int32, (rows, LANES), 1)
    out = []
    for c in range(width // LANES):
        v = x[:, c * LANES:(c + 1) * LANES]
        s = 1
        while s < LANES:
            v = v + jnp.where(lane >= s, pltpu.roll(v, s, axis=1), 0.0)
            s *= 2
        out.append(v + carry)
        carry = carry + jnp.broadcast_to(v[:, LANES - 1:LANES], (rows, LANES))
    return jnp.concatenate(out, axis=1), carry


def _inproj_kernel(x_ref, sc_ref, sh_ref, w_ref, wf_ref, bf_ref, cos_ref, sin_ref,
                   obf_ref, of32_ref, lf_ref, c_ref, u_sc, carry_sc,
                   *, tm, tiles_per_seq, qa_scale, kr_scale, n_heads):
    i = pl.program_id(0)
    j = pl.program_id(1)

    @pl.when(j == 0)
    def _():
        x = x_ref[...]
        u = (x * (1.0 + sc_ref[...]) + sh_ref[...]).reshape(tm, x.shape[-1]).astype(BF16)
        u_sc[...] = u
        z = lax.dot_general(wf_ref[...].astype(BF16), u, _NT,
                            preferred_element_type=F32) + bf_ref[...]
        lf = jnp.minimum(z, 0.0) - jnp.log1p(jnp.exp(-jnp.abs(z)))
        lf_ref[0] = lf
        first = (i % tiles_per_seq) == 0
        prev = jnp.where(first, 0.0, carry_sc[...])
        c, last = _lane_cumsum(lf, prev)
        c_ref[0] = c
        carry_sc[...] = last

    def project():
        return jnp.dot(u_sc[...], w_ref[...], preferred_element_type=F32)

    @pl.when(j == _G_QA)
    def _():
        obf_ref[...] = (project() * qa_scale).astype(BF16)

    def rope(scale):
        acc = project()
        cos = cos_ref[...]
        sin = sin_ref[...]
        for h in range(n_heads):
            sl = slice(h * LANES, (h + 1) * LANES)
            a = acc[:, sl]
            r = a * cos + pltpu.roll(a, LANES // 2, axis=1) * sin
            if scale is not None:
                r = r * scale
            obf_ref[:, sl] = r.astype(BF16)

    @pl.when(j == _G_QR)
    def _():
        rope(None)

    @pl.when(j == _G_KR)
    def _():
        rope(kr_scale)

    @pl.when(j == _G_VR)
    def _():
        obf_ref[...] = project().astype(BF16)

    @pl.when((j == _G_KA) | (j == _G_VA) | (j >= _G_GA0))
    def _():
        of32_ref[...] = project()


def _inproj(x3, mod3, mod_group0, w_main, wf_t, bf_col, cos_t, sin_t, *, tm, seq_tiled):
    g, r, d = x3.shape
    m = g * r
    n_tiles = m // tm
    if seq_tiled:
        tps = r // tm
        gb = 1
        x_spec = pl.BlockSpec((1, tm, d), lambda i, j: (i // tps, i % tps, 0))
        mod_idx = lambda k: (lambda i, j: (mod_group0 + i // tps, 0, k))
        tab_spec = pl.BlockSpec((tm, LANES), lambda i, j: (i % tps, 0))
    else:
        tps = 1
        gb = tm // r
        x_spec = pl.BlockSpec((gb, r, d), lambda i, j: (i, 0, 0))
        mod_idx = lambda k: (lambda i, j: (mod_group0 // gb + i, 0, k))
        tab_spec = pl.BlockSpec((tm, LANES), lambda i, j: (0, 0))
    n_bf = _BF_VR + 1
    kern = functools.partial(_inproj_kernel, tm=tm, tiles_per_seq=tps,
                             qa_scale=LANES ** -0.5, kr_scale=LANES ** -0.5, n_heads=1024 // LANES)
    return pl.pallas_call(
        kern,
        grid=(n_tiles, _N_GROUPS),
        in_specs=[x_spec,
                  pl.BlockSpec((gb, 1, d), mod_idx(1)),
                  pl.BlockSpec((gb, 1, d), mod_idx(0)),
                  pl.BlockSpec((d, 1024), lambda i, j: (0, _w_block(j))),
                  pl.BlockSpec((16, d), lambda i, j: (0, 0)),
                  pl.BlockSpec((16, 1), lambda i, j: (0, 0)),
                  tab_spec, tab_spec],
        out_specs=[pl.BlockSpec((tm, 1024), lambda i, j: (i, _bf_block(j))),
                   pl.BlockSpec((tm, 1024), lambda i, j: (i, _f32_block(j))),
                   pl.BlockSpec((1, 16, tm), lambda i, j: (i, 0, 0)),
                   pl.BlockSpec((1, 16, tm), lambda i, j: (i, 0, 0))],
        out_shape=[jax.ShapeDtypeStruct((m, 1024 * n_bf), BF16),
                   jax.ShapeDtypeStruct((m, 1024 * (_N_GROUPS - n_bf)), F32),
                   jax.ShapeDtypeStruct((n_tiles, 16, tm), F32),
                   jax.ShapeDtypeStruct((n_tiles, 16, tm), F32)],
        scratch_shapes=[pltpu.VMEM((tm, d), BF16), pltpu.VMEM((16, LANES), F32)],
        compiler_params=_params("arbitrary", "arbitrary"),
        name="inproj",
    )(x3, mod3, mod3, w_main, wf_t, bf_col, cos_t, sin_t)


def _fox_prefill_kernel(q_ref, k_ref, v_ref, c_ref, o_ref, *, tq, hpb):
    hg = pl.program_id(1)
    qi = pl.program_id(2)
    dh = LANES

    def chunk(kc, carry, diagonal):
        start = pl.multiple_of(kc * tq, tq)
        out = []
        for hh in range(hpb):
            m, l, acc = carry[hh]
            sl = slice(hh * dh, (hh + 1) * dh)
            kk = k_ref[0, pl.ds(start, tq), sl].astype(BF16)
            vv = v_ref[0, pl.ds(start, tq), sl].astype(BF16)
            s = lax.dot_general(q_ref[0, :, sl], kk, _NT, preferred_element_type=F32)
            s = s - c_ref[0, kc, pl.ds(hg * hpb + hh, 1), :]
            if diagonal:
                row = lax.broadcasted_iota(jnp.int32, (tq, tq), 0)
                col = lax.broadcasted_iota(jnp.int32, (tq, tq), 1)
                s = jnp.where(row >= col, s, NEG_BIG)
            m_new = jnp.maximum(m, jnp.max(s, axis=-1, keepdims=True))
            alpha = jnp.exp(m - m_new)
            p = jnp.exp(s - m_new)
            l = alpha * l + jnp.sum(p, axis=-1, keepdims=True)
            acc = alpha * acc + jnp.dot(p.astype(BF16), vv, preferred_element_type=F32)
            out.append((m_new, l, acc))
        return tuple(out)

    init = tuple((jnp.full((tq, 1), NEG_BIG, F32), jnp.zeros((tq, 1), F32),
                  jnp.zeros((tq, dh), F32)) for _ in range(hpb))
    carry = lax.fori_loop(0, qi, lambda kc, c: chunk(kc, c, False), init)
    carry = chunk(qi, carry, True)
    for hh in range(hpb):
        _, l, acc = carry[hh]
        o_ref[0, :, hh * dh:(hh + 1) * dh] = (acc / l).astype(BF16)


def _fox_prefill(pbf3, pf32_3, c4, *, n_heads, tq, hpb):
    b, t, _ = pbf3.shape
    w = hpb * LANES
    nb = n_heads // hpb
    return pl.pallas_call(
        functools.partial(_fox_prefill_kernel, tq=tq, hpb=hpb),
        grid=(b, nb, t // tq),
        in_specs=[pl.BlockSpec((1, tq, w), lambda bi, h, qi: (bi, qi, _BF_QA * nb + h)),
                  pl.BlockSpec((1, t, w), lambda bi, h, qi: (bi, 0, _F32_KA * nb + h)),
                  pl.BlockSpec((1, t, w), lambda bi, h, qi: (bi, 0, _F32_VA * nb + h)),
                  pl.BlockSpec((1, t // tq, 16, tq), lambda bi, h, qi: (bi, 0, 0, 0))],
        out_specs=pl.BlockSpec((1, tq, w), lambda bi, h, qi: (bi, qi, h)),
        out_shape=jax.ShapeDtypeStruct((b, t, n_heads * LANES), BF16),
        compiler_params=_params("arbitrary", "arbitrary", "arbitrary"),
        name="fox_prefill",
    )(pbf3, pf32_3, pf32_3, c4)


def _periodic_tail(v, lane, n_heads):
    y = jnp.where(lane >= LANES - n_heads, v, 0.0)
    s = n_heads
    while s < LANES:
        y = y + pltpu.roll(y, LANES - s, axis=1)
        s *= 2
    return y


def _page_copies(pt_ref, ck_hbm, cv_hbm, clf_hbm, kbuf, vbuf, lfbuf, sems, seq, slot, n_pages):
    copies = []
    for p in range(n_pages):
        page = pt_ref[seq * n_pages + p]
        copies.append(pltpu.make_async_copy(ck_hbm.at[page], kbuf.at[slot, p], sems.at[slot, 0]))
        copies.append(pltpu.make_async_copy(cv_hbm.at[page], vbuf.at[slot, p], sems.at[slot, 1]))
        copies.append(pltpu.make_async_copy(clf_hbm.at[page], lfbuf.at[slot, p], sems.at[slot, 2]))
    return copies


def _fox_decode_kernel(pt_ref, q_ref, kn_ref, vn_ref, lfn_ref, ck_hbm, cv_hbm, clf_hbm, o_ref,
                       kbuf, vbuf, lfbuf, sems, lf_sc, s_sc, *, n_pages, n_heads, n_new):
    b = pl.program_id(0)
    slot = b % 2
    copies = functools.partial(_page_copies, pt_ref, ck_hbm, cv_hbm, clf_hbm, kbuf, vbuf, lfbuf,
                               sems, n_pages=n_pages)

    @pl.when(b == 0)
    def _():
        for cp in copies(seq=0, slot=0):
            cp.start()

    @pl.when(b + 1 < pl.num_programs(0))
    def _():
        for cp in copies(seq=b + 1, slot=1 - slot):
            cp.start()

    for cp in copies(seq=b, slot=slot):
        cp.wait()

    k_refs = [kbuf.at[slot, p] for p in range(n_pages)]
    v_refs = [vbuf.at[slot, p] for p in range(n_pages)]
    lf_refs = [lfbuf.at[slot, p] for p in range(n_pages)]

    page_w = lfbuf.shape[-1]
    ppr = n_pages // SUBLANES
    n_chunks = ppr * page_w // LANES
    rows_q = n_new * n_heads

    for p in range(n_pages):
        r, part = divmod(p, ppr)
        lf_sc[r:r + 1, part * page_w:(part + 1) * page_w] = lf_refs[p][...]
    lane = lax.broadcasted_iota(jnp.int32, (SUBLANES, LANES), 1)
    sub = lax.broadcasted_iota(jnp.int32, (SUBLANES, LANES), 0)
    chunks = []
    carry = jnp.zeros((SUBLANES, LANES), F32)
    for c in range(n_chunks):
        v = lf_sc[:, c * LANES:(c + 1) * LANES]
        s = n_heads
        while s < LANES:
            v = v + jnp.where(lane >= s, pltpu.roll(v, s, axis=1), 0.0)
            s *= 2
        chunks.append(v + carry)
        carry = carry + _periodic_tail(v, lane, n_heads)
    inc = carry
    s = 1
    while s < SUBLANES:
        inc = inc + jnp.where(sub >= s, pltpu.roll(inc, s, axis=0), 0.0)
        s *= 2
    exc = jnp.where(sub >= 1, pltpu.roll(inc, 1, axis=0), 0.0)
    chunks = [v + exc for v in chunks]
    past_total = inc[SUBLANES - 1:SUBLANES, :]

    q = q_ref[0]
    dh = q.shape[-1]
    row_h = lax.broadcasted_iota(jnp.int32, (rows_q, page_w), 0) % n_heads
    col_h = lax.broadcasted_iota(jnp.int32, (rows_q, page_w), 1) % n_heads
    same_head = row_h == col_h

    cpp = page_w // LANES
    mx = jnp.full((rows_q, LANES), NEG_BIG, F32)
    for p in range(n_pages):
        r, part = divmod(p, ppr)
        ck = jnp.concatenate([chunks[part * cpp + c][r:r + 1, :] for c in range(cpp)], axis=1)
        kk = k_refs[p][...].astype(BF16)
        s = lax.dot_general(q, kk, _NT, preferred_element_type=F32)
        s = jnp.where(same_head, s - ck, NEG_BIG)
        s_sc[p] = s
        for c in range(cpp):
            mx = jnp.maximum(mx, s[:, c * LANES:(c + 1) * LANES])

    lane1 = lax.broadcasted_iota(jnp.int32, (1, LANES), 1)
    cn = lfn_ref[0]
    s = n_heads
    while s < rows_q:
        cn = cn + jnp.where(lane1 >= s, pltpu.roll(cn, s, axis=1), 0.0)
        s *= 2
    cn = cn + past_total
    s_new = lax.dot_general(q, kn_ref[0].astype(BF16), _NT, preferred_element_type=F32)
    row = lax.broadcasted_iota(jnp.int32, (rows_q, rows_q), 0)
    col = lax.broadcasted_iota(jnp.int32, (rows_q, rows_q), 1)
    ok = ((row % n_heads) == (col % n_heads)) & (col <= row)
    s_new = jnp.where(ok, s_new - cn[:, :rows_q], NEG_BIG)
    m = jnp.maximum(jnp.max(mx, axis=-1, keepdims=True), jnp.max(s_new, axis=-1, keepdims=True))

    lsum = jnp.zeros((rows_q, LANES), F32)
    acc = jnp.zeros((rows_q, dh), F32)
    for p in range(n_pages):
        pr = jnp.exp(s_sc[p] - m)
        for c in range(cpp):
            lsum = lsum + pr[:, c * LANES:(c + 1) * LANES]
        acc = acc + jnp.dot(pr.astype(BF16), v_refs[p][...].astype(BF16),
                            preferred_element_type=F32)
    pr = jnp.exp(s_new - m)
    l = jnp.sum(lsum, axis=-1, keepdims=True) + jnp.sum(pr, axis=-1, keepdims=True)
    acc = acc + jnp.dot(pr.astype(BF16), vn_ref[0].astype(BF16), preferred_element_type=F32)
    o_ref[0] = (acc / l).astype(BF16)


def _fox_decode(page_table, q3, kn3, vn3, lfn3, ck3, cv3, clf3, *, n_heads, n_new):
    bs, n_pages = page_table.shape
    rows_q, dh = q3.shape[1], q3.shape[2]
    page_rows = ck3.shape[1]
    page_w = clf3.shape[2]

    def same(b, pt):
        return (b, 0, 0)

    hbm = pl.BlockSpec(memory_space=pl.ANY)
    grid_spec = pltpu.PrefetchScalarGridSpec(
        num_scalar_prefetch=1,
        grid=(bs,),
        in_specs=[pl.BlockSpec((1, rows_q, dh), same),
                  pl.BlockSpec((1, rows_q, dh), same),
                  pl.BlockSpec((1, rows_q, dh), same),
                  pl.BlockSpec((1, 1, LANES), same),
                  hbm, hbm, hbm],
        out_specs=pl.BlockSpec((1, rows_q, dh), same),
        scratch_shapes=[pltpu.VMEM((2, n_pages, page_rows, dh), F32),
                        pltpu.VMEM((2, n_pages, page_rows, dh), F32),
                        pltpu.VMEM((2, n_pages, 1, page_w), F32),
                        pltpu.SemaphoreType.DMA((2, 3)),
                        pltpu.VMEM((SUBLANES, n_pages // SUBLANES * page_w), F32),
                        pltpu.VMEM((n_pages, rows_q, page_w), F32)],
    )
    return pl.pallas_call(
        functools.partial(_fox_decode_kernel, n_pages=n_pages, n_heads=n_heads, n_new=n_new),
        grid_spec=grid_spec,
        out_shape=jax.ShapeDtypeStruct((bs, rows_q, dh), BF16),
        compiler_params=_params("arbitrary"),
        name="fox_decode",
    )(page_table.reshape(-1), q3, kn3, vn3, lfn3, ck3, cv3, clf3)


def _ret_kernel(*refs, n_heads, has_state, bpb):
    if has_state:
        (q_ref, k_ref, v_ref, g_ref, dmat_ref, qdec_ref, kdec_ref, sdec_ref, s0_ref,
         o_ref, sout_ref, s_sc) = refs
    else:
        (q_ref, k_ref, v_ref, g_ref, dmat_ref, qdec_ref, kdec_ref, sdec_ref,
         o_ref, sout_ref, s_sc) = refs
    c = pl.program_id(1)

    @pl.when(c == 0)
    def _():
        if has_state:
            s_sc[...] = s0_ref[...]
        else:
            s_sc[...] = jnp.zeros_like(s_sc)

    for bb in range(bpb):
        for h in range(n_heads):
            sl = slice(h * LANES, (h + 1) * LANES)
            q = q_ref[bb, :, sl]
            k = k_ref[bb, :, sl]
            v = v_ref[bb, :, sl]
            st = s_sc[bb, h]
            att = lax.dot_general(q, k, _NT, preferred_element_type=F32) * dmat_ref[h]
            inner = jnp.dot(att.astype(BF16), v, preferred_element_type=F32)
            cross = jnp.dot(q, st.astype(BF16), preferred_element_type=F32) * qdec_ref[h]
            o = inner + cross
            kw = (k.astype(F32) * kdec_ref[h]).astype(BF16)
            s_sc[bb, h] = sdec_ref[h] * st + lax.dot_general(kw, v, _TN,
                                                            preferred_element_type=F32)
            mu = jnp.mean(o, axis=-1, keepdims=True)
            d = o - mu
            var = jnp.mean(d * d, axis=-1, keepdims=True)
            g = g_ref[bb, :, sl]
            o_ref[bb, :, sl] = (d * lax.rsqrt(var + GN_EPS)
                                * (g * jax.nn.sigmoid(g))).astype(BF16)

    @pl.when(c == pl.num_programs(1) - 1)
    def _():
        sout_ref[...] = s_sc[...]


def _retention_tables(n_heads, chunk, dk):
    lg = jnp.log(1.0 - jnp.exp2(-5.0 - jnp.arange(n_heads, dtype=F32)))
    idx = jnp.arange(chunk, dtype=F32)
    diff = idx[:, None] - idx[None, :]
    dmat = jnp.where(diff[None] >= 0, jnp.exp(diff[None] * lg[:, None, None]), 0.0)
    qdec = jnp.exp((idx + 1.0)[None, :] * lg[:, None])
    kdec = jnp.exp((chunk - 1.0 - idx)[None, :] * lg[:, None])
    sdec = jnp.exp(chunk * lg)
    bc = lambda a: jnp.broadcast_to(a[:, :, None], a.shape + (dk,))
    return dmat, bc(qdec), bc(kdec), jnp.broadcast_to(sdec[:, None, None], (n_heads, 1, dk))


def _retention(pbf3, pf32_3, state, *, n_heads, chunk, bpb):
    b, t, _ = pbf3.shape
    w = n_heads * LANES
    n_chunks = t // chunk
    tabs = _retention_tables(n_heads, chunk, LANES)
    has_state = state is not None
    const3 = lambda bi, c: (0, 0, 0)
    in_specs = [pl.BlockSpec((bpb, chunk, w), lambda bi, c: (bi, c, _BF_QR)),
                pl.BlockSpec((bpb, chunk, w), lambda bi, c: (bi, c, _BF_KR)),
                pl.BlockSpec((bpb, chunk, w), lambda bi, c: (bi, c, _BF_VR)),
                pl.BlockSpec((bpb, chunk, w), lambda bi, c: (bi, c, _F32_GR)),
                pl.BlockSpec((n_heads, chunk, chunk), const3),
                pl.BlockSpec((n_heads, chunk, LANES), const3),
                pl.BlockSpec((n_heads, chunk, LANES), const3),
                pl.BlockSpec((n_heads, 1, LANES), const3)]
    args = [pbf3, pbf3, pbf3, pf32_3, *tabs]
    state_spec = pl.BlockSpec((bpb, n_heads, LANES, LANES), lambda bi, c: (bi, 0, 0, 0))
    if has_state:
        in_specs.append(state_spec)
        args.append(state)
    return pl.pallas_call(
        functools.partial(_ret_kernel, n_heads=n_heads, has_state=has_state, bpb=bpb),
        grid=(b // bpb, n_chunks),
        in_specs=in_specs,
        out_specs=[pl.BlockSpec((bpb, chunk, w), lambda bi, c: (bi, c, 0)), state_spec],
        out_shape=[jax.ShapeDtypeStruct((b, t, w), BF16),
                   jax.ShapeDtypeStruct((b, n_heads, LANES, LANES), F32)],
        scratch_shapes=[pltpu.VMEM((bpb, n_heads, LANES, LANES), F32)],
        compiler_params=_params("arbitrary", "arbitrary"),
        name="retention",
    )(*args)


def _merge_kernel(oa_ref, ob_ref, ga_ref, gb_ref, x_ref, g1_ref, wpa_ref, wpb_ref, wo_ref,
                  lng_ref, lnb_ref, o_ref, *, alpha):
    a = jnp.dot(oa_ref[...], wpa_ref[...], preferred_element_type=F32)
    b = jnp.dot(ob_ref[...], wpb_ref[...], preferred_element_type=F32)
    merged = jax.nn.sigmoid(ga_ref[...]) * a + jax.nn.sigmoid(gb_ref[...]) * b
    y = jnp.dot(merged.astype(BF16), wo_ref[...], preferred_element_type=F32)
    x = x_ref[...]
    r = alpha * x + g1_ref[...] * y.reshape(x.shape)
    o_ref[...] = _layer_norm(r, lng_ref[...], lnb_ref[...])


def _row_tiling(x3, mod_group0, tm, n_grid_axes):
    g, r, d = x3.shape
    if r >= tm:
        tps = r // tm
        gb = 1
        if n_grid_axes == 1:
            x_map = lambda i: (i // tps, i % tps, 0)
            mod_idx = lambda k: (lambda i: (mod_group0 + i // tps, 0, k))
        else:
            x_map = lambda i, j: (i // tps, i % tps, 0)
            mod_idx = lambda k: (lambda i, j: (mod_group0 + i // tps, 0, k))
        x_spec = pl.BlockSpec((1, tm, d), x_map)
    else:
        tps = 1
        gb = tm // r
        if n_grid_axes == 1:
            x_map = lambda i: (i, 0, 0)
            mod_idx = lambda k: (lambda i: (mod_group0 // gb + i, 0, k))
        else:
            x_map = lambda i, j: (i, 0, 0)
            mod_idx = lambda k: (lambda i, j: (mod_group0 // gb + i, 0, k))
        x_spec = pl.BlockSpec((gb, r, d), x_map)
    mod_spec = lambda k: pl.BlockSpec((gb, 1, d), mod_idx(k))
    return x_spec, mod_spec, tps, gb


def _merge(oa2, ob2, pf32, x3, mod3, mod_group0, wpa, wpb, wo, lng, lnb, *, tm, alpha):
    g, r, d = x3.shape
    m = g * r
    wa = oa2.shape[1]
    x_spec, mod_spec, _, _ = _row_tiling(x3, mod_group0, tm, 1)
    const = lambda i: (0, 0)
    return pl.pallas_call(
        functools.partial(_merge_kernel, alpha=alpha),
        grid=(m // tm,),
        in_specs=[pl.BlockSpec((tm, wa), lambda i: (i, 0)),
                  pl.BlockSpec((tm, wa), lambda i: (i, 0)),
                  pl.BlockSpec((tm, d), lambda i: (i, _F32_GA // 2)),
                  pl.BlockSpec((tm, d), lambda i: (i, _F32_GB // 2)),
                  x_spec,
                  mod_spec(2),
                  pl.BlockSpec(wpa.shape, const, pipeline_mode=pl.Buffered(1)),
                  pl.BlockSpec(wpb.shape, const, pipeline_mode=pl.Buffered(1)),
                  pl.BlockSpec(wo.shape, const, pipeline_mode=pl.Buffered(1)),
                  pl.BlockSpec((1, d), const),
                  pl.BlockSpec((1, d), const)],
        out_specs=x_spec,
        out_shape=jax.ShapeDtypeStruct(x3.shape, F32),
        compiler_params=_params("arbitrary"),
        name="merge_outproj",
    )(oa2, ob2, pf32, pf32, x3, mod3, wpa, wpb, wo, lng, lnb)


def _ffn_kernel(*refs, tm, tiles_per_seq, alpha, with_state, conv_w):
    if with_state:
        (x_ref, sc_ref, sh_ref, g2_ref, wa_ref, wb_ref, wd_ref, cw_ref, cb_ref, lng_ref, lnb_ref,
         st_ref, o_ref, tail_ref, u_sc, abuf) = refs
    else:
        (x_ref, sc_ref, sh_ref, g2_ref, wa_ref, wb_ref, wd_ref, cw_ref, cb_ref, lng_ref, lnb_ref,
         o_ref, tail_ref, u_sc, abuf, carry_sc) = refs
    i = pl.program_id(0)
    j = pl.program_id(1)
    hist = conv_w - 1

    @pl.when(j == 0)
    def _():
        x = x_ref[...]
        u_sc[...] = (x * (1.0 + sc_ref[...]) + sh_ref[...]).reshape(tm, x.shape[-1]).astype(BF16)
        o_ref[...] = jnp.zeros_like(o_ref)

    u = u_sc[...]
    a = jnp.dot(u, wa_ref[...], preferred_element_type=F32)
    b = jnp.dot(u, wb_ref[...], preferred_element_type=F32)
    tf = a.shape[-1]
    cw = cw_ref[...]
    if with_state:
        gb = tm // SUBLANES
        a3 = a.reshape(gb, SUBLANES, tf)
        abuf[:, SUBLANES:2 * SUBLANES, :] = a3
        abuf[:, SUBLANES - hist:SUBLANES, :] = st_ref[...]
        shifted = [abuf[:, SUBLANES - hist + w:2 * SUBLANES - hist + w, :].reshape(tm, tf)
                   for w in range(hist)]
        tail_ref[...] = a3[:, SUBLANES - hist:, :]
    else:
        first = (i % tiles_per_seq) == 0
        abuf[0:SUBLANES, :] = jnp.where(first, 0.0, carry_sc[j])
        abuf[SUBLANES:SUBLANES + tm, :] = a
        shifted = [abuf[SUBLANES - hist + w:SUBLANES - hist + w + tm, :] for w in range(hist)]
        carry_sc[j] = a[tm - SUBLANES:, :]
        tail_ref[0] = a[tm - hist:, :]
    conv = a * cw[hist:hist + 1, :]
    for w in range(hist):
        conv = conv + shifted[w] * cw[w:w + 1, :]
    conv = conv + cb_ref[...]
    hg = (jax.nn.gelu(conv) * b).astype(BF16)
    o_ref[...] += jnp.dot(hg, wd_ref[...], preferred_element_type=F32).reshape(o_ref.shape)

    @pl.when(j == pl.num_programs(1) - 1)
    def _():
        x = x_ref[...]
        r = alpha * x + g2_ref[...] * o_ref[...]
        o_ref[...] = _layer_norm(r, lng_ref[...], lnb_ref[...])


def _ffn(x3, mod3, mod_group0, wup, wdown, conv_w, conv_b, lng, lnb, conv_state,
         *, tm, tf, alpha):
    g, r, d = x3.shape
    m = g * r
    dff = wdown.shape[0]
    nj = dff // tf
    cwid = conv_w.shape[0]
    hist = cwid - 1
    x_spec, mod_spec, tps, gb = _row_tiling(x3, mod_group0, tm, 2)
    with_state = conv_state is not None
    const = lambda i, j: (0, 0)
    x_in_spec = pl.BlockSpec(x_spec.block_shape, x_spec.index_map, pipeline_mode=pl.Buffered(1))
    in_specs = [x_in_spec, mod_spec(4), mod_spec(3), mod_spec(5),
                pl.BlockSpec((d, tf), lambda i, j: (0, j)),
                pl.BlockSpec((d, tf), lambda i, j: (0, nj + j)),
                pl.BlockSpec((tf, d), lambda i, j: (j, 0)),
                pl.BlockSpec((cwid, tf), lambda i, j: (0, j)),
                pl.BlockSpec((1, tf), lambda i, j: (0, j)),
                pl.BlockSpec((1, d), const),
                pl.BlockSpec((1, d), const)]
    args = [x3, mod3, mod3, mod3, wup, wup, wdown, conv_w, conv_b.reshape(1, dff), lng, lnb]
    scratch = [pltpu.VMEM((tm, d), BF16)]
    if with_state:
        assert r == SUBLANES
        in_specs.append(pl.BlockSpec((gb, hist, tf), lambda i, j: (i, 0, j)))
        args.append(conv_state)
        tail_spec = pl.BlockSpec((gb, hist, tf), lambda i, j: (i, 0, j))
        scratch.append(pltpu.VMEM((gb, 2 * SUBLANES, tf), F32))
    else:
        tail_spec = pl.BlockSpec((1, hist, tf), lambda i, j: (i, 0, j))
        scratch += [pltpu.VMEM((tm + SUBLANES, tf), F32), pltpu.VMEM((nj, SUBLANES, tf), F32)]
    n_tails = g if with_state else m // tm
    kern = functools.partial(_ffn_kernel, tm=tm, tiles_per_seq=tps, alpha=alpha,
                             with_state=with_state, conv_w=cwid)
    y, tails = pl.pallas_call(
        kern,
        grid=(m // tm, nj),
        in_specs=in_specs,
        out_specs=[x_spec, tail_spec],
        out_shape=[jax.ShapeDtypeStruct(x3.shape, F32),
                   jax.ShapeDtypeStruct((n_tails, hist, dff), F32)],
        scratch_shapes=scratch,
        compiler_params=_params("arbitrary", "arbitrary"),
        name="convffn",
    )(*args)
    if not with_state:
        tails = tails.reshape(g, tps, hist, dff)[:, tps - 1]
    return y, tails


def _rope_tables(pos, half):
    inv = ROPE_BASE ** (-jnp.arange(half, dtype=F32) / half)
    ang = pos.astype(F32)[:, None] * inv[None, :]
    cos, sin = jnp.cos(ang), jnp.sin(ang)
    return jnp.concatenate([cos, cos], axis=-1), jnp.concatenate([-sin, sin], axis=-1)


def kernel(x_prompt, x_sample, cache_k, cache_v, cache_logf, state_ret, state_conv, page_table,
           c_prompt, c_sample, w_ada, b_ada, w_in, b_f, w_pa, w_pb, w_o, ln1_g, ln1_b,
           w_up, conv_w, conv_b, w_down, ln2_g, ln2_b):
    depth = w_ada.shape[0]
    b, t, d = x_prompt.shape
    bs, ts, _ = x_sample.shape
    n_pool, page_size, n_heads, dh = cache_k.shape[1:]
    n_pages = page_table.shape[1]
    past = n_pages * page_size
    wa = n_heads * dh
    dff = w_down.shape[1]
    alpha = (2.0 * depth) ** 0.25
    assert wa == 1024 and dh == LANES and ts == SUBLANES and n_pages % SUBLANES == 0
    assert w_in.shape[2] == 7 * wa + n_heads + 2 * d and d == 2 * wa

    tm_p, tm_s = 1024, bs * ts
    tq = 512
    ret_chunk = 128 if t % 128 == 0 else t

    cos_p, sin_p = _rope_tables(jnp.arange(t), dh // 2)
    cos_s, sin_s = _rope_tables(past + jnp.arange(ts), dh // 2)
    cos_s, sin_s = jnp.tile(cos_s, (tm_s // ts, 1)), jnp.tile(sin_s, (tm_s // ts, 1))

    n_mod = -(-(bs + b) // 16) * 16
    c_all = jnp.concatenate([c_sample, c_prompt, jnp.zeros((n_mod - bs - b, d), F32)], axis=0)

    hp, hs = x_prompt, x_sample
    outs = {k: [] for k in ("kp", "vp", "lp", "rp", "cp", "ks", "vs", "ls", "rs", "cs")}
    for l in range(depth):
        mod3 = _ada(c_all, w_ada[l], b_ada[l]).reshape(n_mod, 1, 6 * d)

        fa0 = 3 * wa
        w_main = _repack_w_in(w_in, l, fa0, n_heads)
        wf_t = jnp.zeros((16, d), F32).at[:n_heads].set(w_in[l, :, fa0:fa0 + n_heads].T)
        bf_col = jnp.zeros((16, 1), F32).at[:n_heads, 0].set(b_f[l].astype(F32))
        wpa, wpb, wo = w_pa[l].astype(BF16), w_pb[l].astype(BF16), w_o[l].astype(BF16)
        wup = w_up[l].astype(BF16)
        wdown = w_down[l].astype(BF16)
        lng1, lnb1 = ln1_g[l].reshape(1, d), ln1_b[l].reshape(1, d)
        lng2, lnb2 = ln2_g[l].reshape(1, d), ln2_b[l].reshape(1, d)

        pbf, pf32, lf_t, c_t = _inproj(hp, mod3, bs, w_main, wf_t, bf_col, cos_p, sin_p,
                                       tm=tm_p, seq_tiled=True)
        pbf3 = pbf.reshape(b, t, -1)
        pf32_3 = pf32.reshape(b, t, -1)
        nt = t // tm_p
        c4 = (c_t.reshape(b, nt, 16, tm_p // tq, tq).transpose(0, 1, 3, 2, 4)
              .reshape(b, t // tq, 16, tq))
        oa = _fox_prefill(pbf3, pf32_3, c4, n_heads=n_heads, tq=tq, hpb=2)
        ob, ret_p = _retention(pbf3, pf32_3, None, n_heads=n_heads, chunk=ret_chunk, bpb=2)
        x1 = _merge(oa.reshape(b * t, wa), ob.reshape(b * t, wa), pf32, hp, mod3, bs,
                    wpa, wpb, wo, lng1, lnb1, tm=256, alpha=alpha)
        hp, conv_p = _ffn(x1, mod3, bs, wup, wdown, conv_w[l], conv_b[l], lng2, lnb2,
                          None, tm=1024, tf=512, alpha=alpha)
        outs["kp"].append(pf32_3[:, :, _F32_KA * wa:(_F32_KA + 1) * wa].reshape(b, t, n_heads, dh))
        outs["vp"].append(pf32_3[:, :, _F32_VA * wa:(_F32_VA + 1) * wa].reshape(b, t, n_heads, dh))
        lf_p = lf_t[:, :n_heads, :].reshape(b, nt, n_heads, tm_p)
        outs["lp"].append(lf_p.transpose(0, 1, 3, 2).reshape(b, t, n_heads))
        outs["rp"].append(ret_p)
        outs["cp"].append(conv_p)

        sbf, sf32, lfs_t, _ = _inproj(hs, mod3, 0, w_main, wf_t, bf_col, cos_s, sin_s,
                                      tm=tm_s, seq_tiled=False)
        rows_q = ts * n_heads
        q3 = sbf[:, _BF_QA * wa:(_BF_QA + 1) * wa].reshape(bs, rows_q, dh)
        k_new = sf32[:, _F32_KA * wa:(_F32_KA + 1) * wa].reshape(bs, ts, n_heads, dh)
        v_new = sf32[:, _F32_VA * wa:(_F32_VA + 1) * wa].reshape(bs, ts, n_heads, dh)
        lf_s = lfs_t[0, :n_heads, :].T.reshape(bs, ts, n_heads)
        lfn3 = jnp.pad(lf_s.reshape(bs, 1, rows_q), ((0, 0), (0, 0), (0, LANES - rows_q)))
        ck3 = cache_k[l].reshape(n_pool, page_size * n_heads, dh)
        cv3 = cache_v[l].reshape(n_pool, page_size * n_heads, dh)
        clf3 = cache_logf[l].astype(F32).reshape(n_pool, 1, page_size * n_heads)
        oa_s = _fox_decode(page_table, q3, k_new.reshape(bs, rows_q, dh),
                           v_new.reshape(bs, rows_q, dh), lfn3, ck3, cv3, clf3,
                           n_heads=n_heads, n_new=ts)
        sbf3 = sbf.reshape(bs, ts, -1)
        sf32_3 = sf32.reshape(bs, ts, -1)
        ob_s, ret_s = _retention(sbf3, sf32_3, state_ret[l], n_heads=n_heads, chunk=ts,
                                 bpb=8)
        x1s = _merge(oa_s.reshape(bs * ts, wa), ob_s.reshape(bs * ts, wa), sf32, hs, mod3, 0,
                     wpa, wpb, wo, lng1, lnb1, tm=256, alpha=alpha)
        hs, conv_s = _ffn(x1s, mod3, 0, wup, wdown, conv_w[l], conv_b[l], lng2, lnb2,
                          state_conv[l], tm=1024, tf=512, alpha=alpha)
        outs["ks"].append(k_new)
        outs["vs"].append(v_new)
        outs["ls"].append(lf_s)
        outs["rs"].append(ret_s)
        outs["cs"].append(conv_s)

    st = lambda name: jnp.stack(outs[name])
    return (hp, hs, st("kp"), st("vp"), st("lp"), st("rp"), st("cp"),
            st("ks"), st("vs"), st("ls"), st("rs"), st("cs"))
```

```python
import functools
import math

import jax
import jax.numpy as jnp
from jax import lax
from jax.experimental import pallas as pl
from jax.experimental.pallas import tpu as pltpu

F32 = jnp.float32
BF16 = jnp.bfloat16

LANES = 128
SUBLANES = 8
VMEM_LIMIT = 56 * 1024 * 1024

ROPE_BASE = 10000.0
LN_EPS = 1e-5
GN_EPS = 1e-5
NEG_BIG = -1e30

_NT = (((1,), (1,)), ((), ()))
_TN = (((0,), (0,)), ((), ()))


def _params(*sem):
    return pltpu.CompilerParams(dimension_semantics=sem, vmem_limit_bytes=VMEM_LIMIT)


def _layer_norm(r, g, b):
    mu = jnp.mean(r, axis=-1, keepdims=True)
    d = r - mu
    var = jnp.mean(d * d, axis=-1, keepdims=True)
    return d * lax.rsqrt(var + LN_EPS) * g + b


def _ada_kernel(c_ref, w_ref, b_ref, o_ref):
    c = c_ref[...]
    s = (c * jax.nn.sigmoid(c)).astype(BF16)
    o_ref[...] = jnp.dot(s, w_ref[...].astype(BF16), preferred_element_type=F32) + b_ref[...]


def _ada(c_all, w_ada, b_ada):
    rows, d = c_all.shape
    n = w_ada.shape[1]
    tn = 512
    return pl.pallas_call(
        _ada_kernel,
        grid=(n // tn,),
        in_specs=[pl.BlockSpec((rows, d), lambda j: (0, 0)),
                  pl.BlockSpec((d, tn), lambda j: (0, j)),
                  pl.BlockSpec((1, tn), lambda j: (0, j))],
        out_specs=pl.BlockSpec((rows, tn), lambda j: (0, j)),
        out_shape=jax.ShapeDtypeStruct((rows, n), F32),
        compiler_params=_params("arbitrary"),
        name="ada_mod",
    )(c_all, w_ada, b_ada.reshape(1, n))


(_G_QA, _G_KA, _G_VA, _G_QR, _G_KR, _G_VR, _G_GA0, _G_GA1, _G_GB0, _G_GB1, _G_GR) = range(11)
_N_GROUPS = 11
_W_BLOCK_OF_GR = 6
_BF_QA, _BF_QR, _BF_KR, _BF_VR = 0, 1, 2, 3
_GATE_GA, _GATE_GB, _GATE_GR = 0, 2, 4


def _w_block(j):
    return jnp.where(j < _G_GA0, j, jnp.where(j < _G_GR, j + 1, _W_BLOCK_OF_GR))


def _bf_block(j):
    return jnp.clip(j - (_G_QR - _BF_QR), _BF_QA, _BF_VR)


def _gate_block(j):
    return jnp.clip(j - _G_GA0, _GATE_GA, _GATE_GR)


def _repack_kernel(wt_ref, o_ref):
    o_ref[...] = wt_ref[0].T.astype(BF16)


def _repack_w_in(w_in_t, layer, fa0, n_f):
    _, n, d = w_in_t.shape
    tc = 1024
    assert fa0 % tc == 0 and (n - n_f) % tc == 0

    def rows(j):
        return pl.multiple_of(jnp.where(j < fa0 // tc, j * tc, j * tc + n_f), SUBLANES)

    return pl.pallas_call(
        _repack_kernel,
        grid=((n - n_f) // tc,),
        in_specs=[pl.BlockSpec((pl.Element(1), pl.Element(tc), pl.Element(d)),
                               lambda j: (layer, rows(j), 0))],
        out_specs=pl.BlockSpec((d, tc), lambda j: (0, j)),
        out_shape=jax.ShapeDtypeStruct((d, n - n_f), BF16),
        compiler_params=_params("arbitrary"),
        name="repack_w_in",
    )(w_in_t)


def _lane_cumsum(x, carry):
    rows, width = x.shape
    lane = lax.broadcasted_iota(jnp.int32, (rows, LANES), 1)
    out = []
    for c in range(width // LANES):
        v = x[:, c * LANES:(c + 1) * LANES]
        s = 1
        while s < LANES:
            v = v + jnp.where(lane >= s, pltpu.roll(v, s, axis=1), 0.0)
            s *= 2
        out.append(v + carry)
        carry = carry + jnp.broadcast_to(v[:, LANES - 1:LANES], (rows, LANES))
    return jnp.concatenate(out, axis=1), carry


def _inproj_kernel(x_ref, sc_ref, sh_ref, w_ref, wf_ref, bf_ref, cos_ref, sin_ref,
                   obf_ref, ok_ref, ov_ref, og_ref, lf_ref, c_ref, u_sc, carry_sc,
                   *, tm, tiles_per_seq, qa_scale, kr_scale, n_heads):
    i = pl.program_id(0)
    j = pl.program_id(1)

    @pl.when(j == 0)
    def _():
        x = x_ref[...]
        u = (x * (1.0 + sc_ref[...]) + sh_ref[...]).reshape(tm, x.shape[-1]).astype(BF16)
        u_sc[...] = u
        z = lax.dot_general(wf_ref[...].astype(BF16), u, _NT,
                            preferred_element_type=F32) + bf_ref[...]
        lf = jnp.minimum(z, 0.0) - jnp.log1p(jnp.exp(-jnp.abs(z)))
        lf_ref[0] = lf
        first = (i % tiles_per_seq) == 0
        prev = jnp.where(first, 0.0, carry_sc[...])
        c, last = _lane_cumsum(lf, prev)
        c_ref[0] = c
        carry_sc[...] = last

    def project():
        return jnp.dot(u_sc[...], w_ref[...], preferred_element_type=F32)

    @pl.when(j == _G_QA)
    def _():
        obf_ref[...] = (project() * qa_scale).astype(BF16)

    def rope(scale):
        acc = project()
        cos = cos_ref[...]
        sin = sin_ref[...]
        for h in range(n_heads):
            sl = slice(h * LANES, (h + 1) * LANES)
            a = acc[:, sl]
            r = a * cos + pltpu.roll(a, LANES // 2, axis=1) * sin
            if scale is not None:
                r = r * scale
            obf_ref[:, sl] = r.astype(BF16)

    @pl.when(j == _G_QR)
    def _():
        rope(None)

    @pl.when(j == _G_KR)
    def _():
        rope(kr_scale)

    @pl.when(j == _G_VR)
    def _():
        obf_ref[...] = project().astype(BF16)

    @pl.when(j == _G_KA)
    def _():
        ok_ref[...] = project()

    @pl.when(j == _G_VA)
    def _():
        ov_ref[...] = project()

    @pl.when(j >= _G_GA0)
    def _():
        og_ref[...] = project()


def _inproj(x3, mod3, mod_group0, w_main, wf_t, bf_col, cos_t, sin_t, *, tm, seq_tiled):
    g, r, d = x3.shape
    m = g * r
    n_tiles = m // tm
    if seq_tiled:
        tps = r // tm
        gb = 1
        x_spec = pl.BlockSpec((1, tm, d), lambda i, j: (i // tps, i % tps, 0))
        mod_idx = lambda k: (lambda i, j: (mod_group0 + i // tps, 0, k))
        tab_spec = pl.BlockSpec((tm, LANES), lambda i, j: (i % tps, 0))
    else:
        tps = 1
        gb = tm // r
        x_spec = pl.BlockSpec((gb, r, d), lambda i, j: (i, 0, 0))
        mod_idx = lambda k: (lambda i, j: (mod_group0 // gb + i, 0, k))
        tab_spec = pl.BlockSpec((tm, LANES), lambda i, j: (0, 0))
    n_bf = _BF_VR + 1
    kern = functools.partial(_inproj_kernel, tm=tm, tiles_per_seq=tps,
                             qa_scale=LANES ** -0.5, kr_scale=LANES ** -0.5, n_heads=1024 // LANES)
    return pl.pallas_call(
        kern,
        grid=(n_tiles, _N_GROUPS),
        in_specs=[pl.BlockSpec(x_spec.block_shape, x_spec.index_map,
                               pipeline_mode=pl.Buffered(1)),
                  pl.BlockSpec((gb, 1, d), mod_idx(1)),
                  pl.BlockSpec((gb, 1, d), mod_idx(0)),
                  pl.BlockSpec((d, 1024), lambda i, j: (0, _w_block(j))),
                  pl.BlockSpec((16, d), lambda i, j: (0, 0)),
                  pl.BlockSpec((16, 1), lambda i, j: (0, 0)),
                  tab_spec, tab_spec],
        out_specs=[pl.BlockSpec((tm, 1024), lambda i, j: (i, _bf_block(j))),
                   pl.BlockSpec((tm, 1024), lambda i, j: (i, 0)),
                   pl.BlockSpec((tm, 1024), lambda i, j: (i, 0)),
                   pl.BlockSpec((tm, 1024), lambda i, j: (i, _gate_block(j))),
                   pl.BlockSpec((1, 16, tm), lambda i, j: (i, 0, 0)),
                   pl.BlockSpec((1, 16, tm), lambda i, j: (i, 0, 0))],
        out_shape=[jax.ShapeDtypeStruct((m, 1024 * n_bf), BF16),
                   jax.ShapeDtypeStruct((m, 1024), F32),
                   jax.ShapeDtypeStruct((m, 1024), F32),
                   jax.ShapeDtypeStruct((m, 1024 * (_GATE_GR + 1)), F32),
                   jax.ShapeDtypeStruct((n_tiles, 16, tm), F32),
                   jax.ShapeDtypeStruct((n_tiles, 16, tm), F32)],
        scratch_shapes=[pltpu.VMEM((tm, d), BF16), pltpu.VMEM((16, LANES), F32)],
        compiler_params=_params("arbitrary", "arbitrary"),
        name="inproj",
    )(x3, mod3, mod3, w_main, wf_t, bf_col, cos_t, sin_t)


def _fox_prefill_kernel(q_ref, k_ref, v_ref, c_ref, o_ref, *, tq, hpb):
    hg = pl.program_id(1)
    qi = pl.program_id(2)
    dh = LANES

    def chunk(kc, carry, diagonal):
        start = pl.multiple_of(kc * tq, tq)
        out = []
        for hh in range(hpb):
            m, l, acc = carry[hh]
            sl = slice(hh * dh, (hh + 1) * dh)
            kk = k_ref[0, pl.ds(start, tq), sl].astype(BF16)
            vv = v_ref[0, pl.ds(start, tq), sl].astype(BF16)
            s = lax.dot_general(q_ref[0, :, sl], kk, _NT, preferred_element_type=F32)
            s = s - c_ref[0, kc, pl.ds(hg * hpb + hh, 1), :]
            if diagonal:
                row = lax.broadcasted_iota(jnp.int32, (tq, tq), 0)
                col = lax.broadcasted_iota(jnp.int32, (tq, tq), 1)
                s = jnp.where(row >= col, s, NEG_BIG)
            m_new = jnp.maximum(m, jnp.max(s, axis=-1, keepdims=True))
            alpha = jnp.exp(m - m_new)
            p = jnp.exp(s - m_new)
            l = alpha * l + jnp.sum(p, axis=-1, keepdims=True)
            acc = alpha * acc + jnp.dot(p.astype(BF16), vv, preferred_element_type=F32)
            out.append((m_new, l, acc))
        return tuple(out)

    init = tuple((jnp.full((tq, 1), NEG_BIG, F32), jnp.zeros((tq, 1), F32),
                  jnp.zeros((tq, dh), F32)) for _ in range(hpb))
    carry = lax.fori_loop(0, qi, lambda kc, c: chunk(kc, c, False), init)
    carry = chunk(qi, carry, True)
    for hh in range(hpb):
        _, l, acc = carry[hh]
        o_ref[0, :, hh * dh:(hh + 1) * dh] = (acc / l).astype(BF16)


def _fox_prefill(pbf3, k3, v3, c4, *, n_heads, tq, hpb):
    b, t, _ = pbf3.shape
    w = hpb * LANES
    nb = n_heads // hpb
    return pl.pallas_call(
        functools.partial(_fox_prefill_kernel, tq=tq, hpb=hpb),
        grid=(b, nb, t // tq),
        in_specs=[pl.BlockSpec((1, tq, w), lambda bi, h, qi: (bi, qi, _BF_QA * nb + h)),
                  pl.BlockSpec((1, t, w), lambda bi, h, qi: (bi, 0, h)),
                  pl.BlockSpec((1, t, w), lambda bi, h, qi: (bi, 0, h)),
                  pl.BlockSpec((1, t // tq, 16, tq), lambda bi, h, qi: (bi, 0, 0, 0))],
        out_specs=pl.BlockSpec((1, tq, w), lambda bi, h, qi: (bi, qi, h)),
        out_shape=jax.ShapeDtypeStruct((b, t, n_heads * LANES), BF16),
        compiler_params=_params("arbitrary", "arbitrary", "arbitrary"),
        name="fox_prefill",
    )(pbf3, k3, v3, c4)


def _periodic_tail(v, lane, n_heads):
    y = jnp.where(lane >= LANES - n_heads, v, 0.0)
    s = n_heads
    while s < LANES:
        y = y + pltpu.roll(y, LANES - s, axis=1)
        s *= 2
    return y


def _page_copies(pt_ref, ck_hbm, cv_hbm, clf_hbm, kbuf, vbuf, lfbuf, sems, seq, slot, n_pages):
    copies = []
    for p in range(n_pages):
        page = pt_ref[seq * n_pages + p]
        copies.append(pltpu.make_async_copy(ck_hbm.at[page], kbuf.at[slot, p], sems.at[slot, 0]))
        copies.append(pltpu.make_async_copy(cv_hbm.at[page], vbuf.at[slot, p], sems.at[slot, 1]))
        copies.append(pltpu.make_async_copy(clf_hbm.at[page], lfbuf.at[slot, p], sems.at[slot, 2]))
    return copies


def _fox_decode_kernel(pt_ref, q_ref, kn_ref, vn_ref, lfn_ref, ck_hbm, cv_hbm, clf_hbm, o_ref,
                       kbuf, vbuf, lfbuf, sems, lf_sc, s_sc, *, n_pages, n_heads, n_new):
    b = pl.program_id(0)
    slot = b % 2
    copies = functools.partial(_page_copies, pt_ref, ck_hbm, cv_hbm, clf_hbm, kbuf, vbuf, lfbuf,
                               sems, n_pages=n_pages)

    @pl.when(b == 0)
    def _():
        for cp in copies(seq=0, slot=0):
            cp.start()

    @pl.when(b + 1 < pl.num_programs(0))
    def _():
        for cp in copies(seq=b + 1, slot=1 - slot):
            cp.start()

    for cp in copies(seq=b, slot=slot):
        cp.wait()

    k_refs = [kbuf.at[slot, p] for p in range(n_pages)]
    v_refs = [vbuf.at[slot, p] for p in range(n_pages)]
    lf_refs = [lfbuf.at[slot, p] for p in range(n_pages)]

    page_w = lfbuf.shape[-1]
    ppr = n_pages // SUBLANES
    n_chunks = ppr * page_w // LANES
    rows_q = n_new * n_heads

    for p in range(n_pages):
        r, part = divmod(p, ppr)
        lf_sc[r:r + 1, part * page_w:(part + 1) * page_w] = lf_refs[p][...]
    lane = lax.broadcasted_iota(jnp.int32, (SUBLANES, LANES), 1)
    sub = lax.broadcasted_iota(jnp.int32, (SUBLANES, LANES), 0)
    chunks = []
    carry = jnp.zeros((SUBLANES, LANES), F32)
    for c in range(n_chunks):
        v = lf_sc[:, c * LANES:(c + 1) * LANES]
        s = n_heads
        while s < LANES:
            v = v + jnp.where(lane >= s, pltpu.roll(v, s, axis=1), 0.0)
            s *= 2
        chunks.append(v + carry)
        carry = carry + _periodic_tail(v, lane, n_heads)
    inc = carry
    s = 1
    while s < SUBLANES:
        inc = inc + jnp.where(sub >= s, pltpu.roll(inc, s, axis=0), 0.0)
        s *= 2
    exc = jnp.where(sub >= 1, pltpu.roll(inc, 1, axis=0), 0.0)
    chunks = [v + exc for v in chunks]
    past_total = inc[SUBLANES - 1:SUBLANES, :]

    q = q_ref[0]
    dh = q.shape[-1]
    row_h = lax.broadcasted_iota(jnp.int32, (rows_q, page_w), 0) % n_heads
    col_h = lax.broadcasted_iota(jnp.int32, (rows_q, page_w), 1) % n_heads
    same_head = row_h == col_h

    cpp = page_w // LANES
    mx = jnp.full((rows_q, LANES), NEG_BIG, F32)
    for p in range(n_pages):
        r, part = divmod(p, ppr)
        ck = jnp.concatenate([chunks[part * cpp + c][r:r + 1, :] for c in range(cpp)], axis=1)
        kk = k_refs[p][...].astype(BF16)
        s = lax.dot_general(q, kk, _NT, preferred_element_type=F32)
        s = jnp.where(same_head, s - ck, NEG_BIG)
        s_sc[p] = s
        for c in range(cpp):
            mx = jnp.maximum(mx, s[:, c * LANES:(c + 1) * LANES])

    lane1 = lax.broadcasted_iota(jnp.int32, (1, LANES), 1)
    cn = lfn_ref[0]
    s = n_heads
    while s < rows_q:
        cn = cn + jnp.where(lane1 >= s, pltpu.roll(cn, s, axis=1), 0.0)
        s *= 2
    cn = cn + past_total
    s_new = lax.dot_general(q, kn_ref[0].astype(BF16), _NT, preferred_element_type=F32)
    row = lax.broadcasted_iota(jnp.int32, (rows_q, rows_q), 0)
    col = lax.broadcasted_iota(jnp.int32, (rows_q, rows_q), 1)
    ok = ((row % n_heads) == (col % n_heads)) & (col <= row)
    s_new = jnp.where(ok, s_new - cn[:, :rows_q], NEG_BIG)
    m = jnp.maximum(jnp.max(mx, axis=-1, keepdims=True), jnp.max(s_new, axis=-1, keepdims=True))

    lsum = jnp.zeros((rows_q, LANES), F32)
    acc = jnp.zeros((rows_q, dh), F32)
    for p in range(n_pages):
        pr = jnp.exp(s_sc[p] - m)
        for c in range(cpp):
            lsum = lsum + pr[:, c * LANES:(c + 1) * LANES]
        acc = acc + jnp.dot(pr.astype(BF16), v_refs[p][...].astype(BF16),
                            preferred_element_type=F32)
    pr = jnp.exp(s_new - m)
    l = jnp.sum(lsum, axis=-1, keepdims=True) + jnp.sum(pr, axis=-1, keepdims=True)
    acc = acc + jnp.dot(pr.astype(BF16), vn_ref[0].astype(BF16), preferred_element_type=F32)
    o_ref[0] = (acc / l).astype(BF16)


def _fox_decode(page_table, q3, kn3, vn3, lfn3, ck3, cv3, clf3, *, n_heads, n_new):
    bs, n_pages = page_table.shape
    rows_q, dh = q3.shape[1], q3.shape[2]
    page_rows = ck3.shape[1]
    page_w = clf3.shape[2]

    def same(b, pt):
        return (b, 0, 0)

    hbm = pl.BlockSpec(memory_space=pl.ANY)
    grid_spec = pltpu.PrefetchScalarGridSpec(
        num_scalar_prefetch=1,
        grid=(bs,),
        in_specs=[pl.BlockSpec((1, rows_q, dh), same),
                  pl.BlockSpec((1, rows_q, dh), same),
                  pl.BlockSpec((1, rows_q, dh), same),
                  pl.BlockSpec((1, 1, LANES), same),
                  hbm, hbm, hbm],
        out_specs=pl.BlockSpec((1, rows_q, dh), same),
        scratch_shapes=[pltpu.VMEM((2, n_pages, page_rows, dh), F32),
                        pltpu.VMEM((2, n_pages, page_rows, dh), F32),
                        pltpu.VMEM((2, n_pages, 1, page_w), F32),
                        pltpu.SemaphoreType.DMA((2, 3)),
                        pltpu.VMEM((SUBLANES, n_pages // SUBLANES * page_w), F32),
                        pltpu.VMEM((n_pages, rows_q, page_w), F32)],
    )
    return pl.pallas_call(
        functools.partial(_fox_decode_kernel, n_pages=n_pages, n_heads=n_heads, n_new=n_new),
        grid_spec=grid_spec,
        out_shape=jax.ShapeDtypeStruct((bs, rows_q, dh), BF16),
        compiler_params=_params("arbitrary"),
        name="fox_decode",
    )(page_table.reshape(-1), q3, kn3, vn3, lfn3, ck3, cv3, clf3)


def _ret_kernel(*refs, n_heads, has_state, bpb):
    if has_state:
        (q_ref, k_ref, v_ref, g_ref, dmat_ref, qdec_ref, kdec_ref, sdec_ref, s0_ref,
         o_ref, sout_ref, s_sc) = refs
    else:
        (q_ref, k_ref, v_ref, g_ref, dmat_ref, qdec_ref, kdec_ref, sdec_ref,
         o_ref, sout_ref, s_sc) = refs
    c = pl.program_id(1)

    @pl.when(c == 0)
    def _():
        if has_state:
            s_sc[...] = s0_ref[...]
        else:
            s_sc[...] = jnp.zeros_like(s_sc)

    for bb in range(bpb):
        for h in range(n_heads):
            sl = slice(h * LANES, (h + 1) * LANES)
            q = q_ref[bb, :, sl]
            k = k_ref[bb, :, sl]
            v = v_ref[bb, :, sl]
            st = s_sc[bb, h]
            att = lax.dot_general(q, k, _NT, preferred_element_type=F32) * dmat_ref[h]
            inner = jnp.dot(att.astype(BF16), v, preferred_element_type=F32)
            cross = jnp.dot(q, st.astype(BF16), preferred_element_type=F32) * qdec_ref[h]
            o = inner + cross
            kw = (k.astype(F32) * kdec_ref[h]).astype(BF16)
            s_sc[bb, h] = sdec_ref[h] * st + lax.dot_general(kw, v, _TN,
                                                            preferred_element_type=F32)
            mu = jnp.mean(o, axis=-1, keepdims=True)
            d = o - mu
            var = jnp.mean(d * d, axis=-1, keepdims=True)
            g = g_ref[bb, :, sl]
            o_ref[bb, :, sl] = (d * lax.rsqrt(var + GN_EPS)
                                * (g * jax.nn.sigmoid(g))).astype(BF16)

    @pl.when(c == pl.num_programs(1) - 1)
    def _():
        sout_ref[...] = s_sc[...]


def _retention_tables(n_heads, chunk, dk):
    lg = jnp.log(1.0 - jnp.exp2(-5.0 - jnp.arange(n_heads, dtype=F32)))
    idx = jnp.arange(chunk, dtype=F32)
    diff = idx[:, None] - idx[None, :]
    dmat = jnp.where(diff[None] >= 0, jnp.exp(diff[None] * lg[:, None, None]), 0.0)
    qdec = jnp.exp((idx + 1.0)[None, :] * lg[:, None])
    kdec = jnp.exp((chunk - 1.0 - idx)[None, :] * lg[:, None])
    sdec = jnp.exp(chunk * lg)
    bc = lambda a: jnp.broadcast_to(a[:, :, None], a.shape + (dk,))
    return dmat, bc(qdec), bc(kdec), jnp.broadcast_to(sdec[:, None, None], (n_heads, 1, dk))


def _retention(pbf3, gates3, state, *, n_heads, chunk, bpb):
    b, t, _ = pbf3.shape
    w = n_heads * LANES
    n_chunks = t // chunk
    tabs = _retention_tables(n_heads, chunk, LANES)
    has_state = state is not None
    const3 = lambda bi, c: (0, 0, 0)
    in_specs = [pl.BlockSpec((bpb, chunk, w), lambda bi, c: (bi, c, _BF_QR)),
                pl.BlockSpec((bpb, chunk, w), lambda bi, c: (bi, c, _BF_KR)),
                pl.BlockSpec((bpb, chunk, w), lambda bi, c: (bi, c, _BF_VR)),
                pl.BlockSpec((bpb, chunk, w), lambda bi, c: (bi, c, _GATE_GR)),
                pl.BlockSpec((n_heads, chunk, chunk), const3),
                pl.BlockSpec((n_heads, chunk, LANES), const3),
                pl.BlockSpec((n_heads, chunk, LANES), const3),
                pl.BlockSpec((n_heads, 1, LANES), const3)]
    args = [pbf3, pbf3, pbf3, gates3, *tabs]
    state_spec = pl.BlockSpec((bpb, n_heads, LANES, LANES), lambda bi, c: (bi, 0, 0, 0))
    if has_state:
        in_specs.append(state_spec)
        args.append(state)
    return pl.pallas_call(
        functools.partial(_ret_kernel, n_heads=n_heads, has_state=has_state, bpb=bpb),
        grid=(b // bpb, n_chunks),
        in_specs=in_specs,
        out_specs=[pl.BlockSpec((bpb, chunk, w), lambda bi, c: (bi, c, 0)), state_spec],
        out_shape=[jax.ShapeDtypeStruct((b, t, w), BF16),
                   jax.ShapeDtypeStruct((b, n_heads, LANES, LANES), F32)],
        scratch_shapes=[pltpu.VMEM((bpb, n_heads, LANES, LANES), F32)],
        compiler_params=_params("arbitrary", "arbitrary"),
        name="retention",
    )(*args)


def _merge_kernel(oa_ref, ob_ref, ga_ref, gb_ref, x_ref, g1_ref, wpa_ref, wpb_ref, wo_ref,
                  lng_ref, lnb_ref, o_ref, *, alpha):
    a = jnp.dot(oa_ref[...], wpa_ref[...], preferred_element_type=F32)
    b = jnp.dot(ob_ref[...], wpb_ref[...], preferred_element_type=F32)
    merged = jax.nn.sigmoid(ga_ref[...]) * a + jax.nn.sigmoid(gb_ref[...]) * b
    y = jnp.dot(merged.astype(BF16), wo_ref[...], preferred_element_type=F32)
    x = x_ref[...]
    r = alpha * x + g1_ref[...] * y.reshape(x.shape)
    o_ref[...] = _layer_norm(r, lng_ref[...], lnb_ref[...])


def _row_tiling(x3, mod_group0, tm, n_grid_axes):
    g, r, d = x3.shape
    if r >= tm:
        tps = r // tm
        gb = 1
        if n_grid_axes == 1:
            x_map = lambda i: (i // tps, i % tps, 0)
            mod_idx = lambda k: (lambda i: (mod_group0 + i // tps, 0, k))
        else:
            x_map = lambda i, j: (i // tps, i % tps, 0)
            mod_idx = lambda k: (lambda i, j: (mod_group0 + i // tps, 0, k))
        x_spec = pl.BlockSpec((1, tm, d), x_map)
    else:
        tps = 1
        gb = tm // r
        if n_grid_axes == 1:
            x_map = lambda i: (i, 0, 0)
            mod_idx = lambda k: (lambda i: (mod_group0 // gb + i, 0, k))
        else:
            x_map = lambda i, j: (i, 0, 0)
            mod_idx = lambda k: (lambda i, j: (mod_group0 // gb + i, 0, k))
        x_spec = pl.BlockSpec((gb, r, d), x_map)
    mod_spec = lambda k: pl.BlockSpec((gb, 1, d), mod_idx(k))
    return x_spec, mod_spec, tps, gb


def _merge(oa2, ob2, gates, x3, mod3, mod_group0, wpa, wpb, wo, lng, lnb, *, tm, alpha):
    g, r, d = x3.shape
    m = g * r
    wa = oa2.shape[1]
    x_spec, mod_spec, _, _ = _row_tiling(x3, mod_group0, tm, 1)
    const = lambda i: (0, 0)
    return pl.pallas_call(
        functools.partial(_merge_kernel, alpha=alpha),
        grid=(m // tm,),
        in_specs=[pl.BlockSpec((tm, wa), lambda i: (i, 0)),
                  pl.BlockSpec((tm, wa), lambda i: (i, 0)),
                  pl.BlockSpec((tm, d), lambda i: (i, _GATE_GA // 2)),
                  pl.BlockSpec((tm, d), lambda i: (i, _GATE_GB // 2)),
                  x_spec,
                  mod_spec(2),
                  pl.BlockSpec(wpa.shape, const, pipeline_mode=pl.Buffered(1)),
                  pl.BlockSpec(wpb.shape, const, pipeline_mode=pl.Buffered(1)),
                  pl.BlockSpec(wo.shape, const, pipeline_mode=pl.Buffered(1)),
                  pl.BlockSpec((1, d), const),
                  pl.BlockSpec((1, d), const)],
        out_specs=x_spec,
        out_shape=jax.ShapeDtypeStruct(x3.shape, F32),
        compiler_params=_params("arbitrary"),
        name="merge_outproj",
    )(oa2, ob2, gates, gates, x3, mod3, wpa, wpb, wo, lng, lnb)


def _ffn_kernel(*refs, tm, tiles_per_seq, alpha, with_state, conv_w):
    if with_state:
        (x_ref, sc_ref, sh_ref, g2_ref, wa_ref, wb_ref, wd_ref, cw_ref, cb_ref, lng_ref, lnb_ref,
         st_ref, o_ref, tail_ref, u_sc, abuf) = refs
    else:
        (x_ref, sc_ref, sh_ref, g2_ref, wa_ref, wb_ref, wd_ref, cw_ref, cb_ref, lng_ref, lnb_ref,
         o_ref, tail_ref, u_sc, abuf, carry_sc) = refs
    i = pl.program_id(0)
    j = pl.program_id(1)
    hist = conv_w - 1

    @pl.when(j == 0)
    def _():
        x = x_ref[...]
        u_sc[...] = (x * (1.0 + sc_ref[...]) + sh_ref[...]).reshape(tm, x.shape[-1]).astype(BF16)
        o_ref[...] = jnp.zeros_like(o_ref)

    u = u_sc[...]
    a = jnp.dot(u, wa_ref[...], preferred_element_type=F32)
    b = jnp.dot(u, wb_ref[...], preferred_element_type=F32)
    tf = a.shape[-1]
    cw = cw_ref[...]
    if with_state:
        gb = tm // SUBLANES
        a3 = a.reshape(gb, SUBLANES, tf)
        abuf[:, SUBLANES:2 * SUBLANES, :] = a3
        abuf[:, SUBLANES - hist:SUBLANES, :] = st_ref[...]
        shifted = [abuf[:, SUBLANES - hist + w:2 * SUBLANES - hist + w, :].reshape(tm, tf)
                   for w in range(hist)]
        tail_ref[...] = a3[:, SUBLANES - hist:, :]
    else:
        first = (i % tiles_per_seq) == 0
        abuf[0:SUBLANES, :] = jnp.where(first, 0.0, carry_sc[j])
        abuf[SUBLANES:SUBLANES + tm, :] = a
        shifted = [abuf[SUBLANES - hist + w:SUBLANES - hist + w + tm, :] for w in range(hist)]
        carry_sc[j] = a[tm - SUBLANES:, :]
        tail_ref[0] = a[tm - hist:, :]
    conv = a * cw[hist:hist + 1, :]
    for w in range(hist):
        conv = conv + shifted[w] * cw[w:w + 1, :]
    conv = conv + cb_ref[...]
    hg = (jax.nn.gelu(conv) * b).astype(BF16)
    o_ref[...] += jnp.dot(hg, wd_ref[...], preferred_element_type=F32).reshape(o_ref.shape)

    @pl.when(j == pl.num_programs(1) - 1)
    def _():
        x = x_ref[...]
        r = alpha * x + g2_ref[...] * o_ref[...]
        o_ref[...] = _layer_norm(r, lng_ref[...], lnb_ref[...])


def _ffn(x3, mod3, mod_group0, wup, wdown, conv_w, conv_b, lng, lnb, conv_state,
         *, tm, tf, alpha):
    g, r, d = x3.shape
    m = g * r
    dff = wdown.shape[0]
    nj = dff // tf
    cwid = conv_w.shape[0]
    hist = cwid - 1
    x_spec, mod_spec, tps, gb = _row_tiling(x3, mod_group0, tm, 2)
    with_state = conv_state is not None
    const = lambda i, j: (0, 0)
    x_in_spec = pl.BlockSpec(x_spec.block_shape, x_spec.index_map, pipeline_mode=pl.Buffered(1))
    in_specs = [x_in_spec, mod_spec(4), mod_spec(3), mod_spec(5),
                pl.BlockSpec((d, tf), lambda i, j: (0, j)),
                pl.BlockSpec((d, tf), lambda i, j: (0, nj + j)),
                pl.BlockSpec((tf, d), lambda i, j: (j, 0)),
                pl.BlockSpec((cwid, tf), lambda i, j: (0, j)),
                pl.BlockSpec((1, tf), lambda i, j: (0, j)),
                pl.BlockSpec((1, d), const),
                pl.BlockSpec((1, d), const)]
    args = [x3, mod3, mod3, mod3, wup, wup, wdown, conv_w, conv_b.reshape(1, dff), lng, lnb]
    scratch = [pltpu.VMEM((tm, d), BF16)]
    if with_state:
        assert r == SUBLANES
        in_specs.append(pl.BlockSpec((gb, hist, tf), lambda i, j: (i, 0, j)))
        args.append(conv_state)
        tail_spec = pl.BlockSpec((gb, hist, tf), lambda i, j: (i, 0, j))
        scratch.append(pltpu.VMEM((gb, 2 * SUBLANES, tf), F32))
    else:
        tail_spec = pl.BlockSpec((1, hist, tf), lambda i, j: (i, 0, j))
        scratch += [pltpu.VMEM((tm + SUBLANES, tf), F32), pltpu.VMEM((nj, SUBLANES, tf), F32)]
    n_tails = g if with_state else m // tm
    kern = functools.partial(_ffn_kernel, tm=tm, tiles_per_seq=tps, alpha=alpha,
                             with_state=with_state, conv_w=cwid)
    y, tails = pl.pallas_call(
        kern,
        grid=(m // tm, nj),
        in_specs=in_specs,
        out_specs=[x_spec, tail_spec],
        out_shape=[jax.ShapeDtypeStruct(x3.shape, F32),
                   jax.ShapeDtypeStruct((n_tails, hist, dff), F32)],
        scratch_shapes=scratch,
        compiler_params=_params("arbitrary", "arbitrary"),
        name="convffn",
    )(*args)
    if not with_state:
        tails = tails.reshape(g, tps, hist, dff)[:, tps - 1]
    return y, tails


def _rope_tables(pos, half):
    inv = ROPE_BASE ** (-jnp.arange(half, dtype=F32) / half)
    ang = pos.astype(F32)[:, None] * inv[None, :]
    cos, sin = jnp.cos(ang), jnp.sin(ang)
    return jnp.concatenate([cos, cos], axis=-1), jnp.concatenate([-sin, sin], axis=-1)


def kernel(x_prompt, x_sample, cache_k, cache_v, cache_logf, state_ret, state_conv, page_table,
           c_prompt, c_sample, w_ada, b_ada, w_in, b_f, w_pa, w_pb, w_o, ln1_g, ln1_b,
           w_up, conv_w, conv_b, w_down, ln2_g, ln2_b):
    depth = w_ada.shape[0]
    b, t, d = x_prompt.shape
    bs, ts, _ = x_sample.shape
    n_pool, page_size, n_heads, dh = cache_k.shape[1:]
    n_pages = page_table.shape[1]
    past = n_pages * page_size
    wa = n_heads * dh
    dff = w_down.shape[1]
    alpha = (2.0 * depth) ** 0.25
    assert wa == 1024 and dh == LANES and ts == SUBLANES and n_pages % SUBLANES == 0
    assert w_in.shape[2] == 7 * wa + n_heads + 2 * d and d == 2 * wa

    tm_p, tm_s = 1024, bs * ts
    tq = 512
    ret_chunk = 128 if t % 128 == 0 else t

    cos_p, sin_p = _rope_tables(jnp.arange(t), dh // 2)
    cos_s, sin_s = _rope_tables(past + jnp.arange(ts), dh // 2)
    cos_s, sin_s = jnp.tile(cos_s, (tm_s // ts, 1)), jnp.tile(sin_s, (tm_s // ts, 1))

    n_mod = -(-(bs + b) // 16) * 16
    c_all = jnp.concatenate([c_sample, c_prompt, jnp.zeros((n_mod - bs - b, d), F32)], axis=0)

    w_in_t = jnp.swapaxes(w_in, 1, 2)

    hp, hs = x_prompt, x_sample
    outs = {k: [] for k in ("kp", "vp", "lp", "rp", "cp", "ks", "vs", "ls", "rs", "cs")}
    for l in range(depth):
        mod3 = _ada(c_all, w_ada[l], b_ada[l]).reshape(n_mod, 1, 6 * d)

        fa0 = 3 * wa
        w_main = _repack_w_in(w_in_t, l, fa0, n_heads)
        wf_t = jnp.zeros((16, d), F32).at[:n_heads].set(w_in_t[l, fa0:fa0 + n_heads, :])
        bf_col = jnp.zeros((16, 1), F32).at[:n_heads, 0].set(b_f[l].astype(F32))
        wpa, wpb, wo = w_pa[l].astype(BF16), w_pb[l].astype(BF16), w_o[l].astype(BF16)
        wup = w_up[l].astype(BF16)
        wdown = w_down[l].astype(BF16)
        lng1, lnb1 = ln1_g[l].reshape(1, d), ln1_b[l].reshape(1, d)
        lng2, lnb2 = ln2_g[l].reshape(1, d), ln2_b[l].reshape(1, d)

        pbf, pk, pv, pg, lf_t, c_t = _inproj(hp, mod3, bs, w_main, wf_t, bf_col, cos_p, sin_p,
                                             tm=tm_p, seq_tiled=True)
        pbf3 = pbf.reshape(b, t, -1)
        nt = t // tm_p
        c4 = (c_t.reshape(b, nt, 16, tm_p // tq, tq).transpose(0, 1, 3, 2, 4)
              .reshape(b, t // tq, 16, tq))
        oa = _fox_prefill(pbf3, pk.reshape(b, t, wa), pv.reshape(b, t, wa), c4,
                          n_heads=n_heads, tq=tq, hpb=2)
        ob, ret_p = _retention(pbf3, pg.reshape(b, t, -1), None, n_heads=n_heads,
                               chunk=ret_chunk, bpb=2)
        x1 = _merge(oa.reshape(b * t, wa), ob.reshape(b * t, wa), pg, hp, mod3, bs,
                    wpa, wpb, wo, lng1, lnb1, tm=256, alpha=alpha)
        hp, conv_p = _ffn(x1, mod3, bs, wup, wdown, conv_w[l], conv_b[l], lng2, lnb2,
                          None, tm=1024, tf=512, alpha=alpha)
        outs["kp"].append(pk.reshape(b, t, n_heads, dh))
        outs["vp"].append(pv.reshape(b, t, n_heads, dh))
        lf_p = lf_t[:, :n_heads, :].reshape(b, nt, n_heads, tm_p)
        outs["lp"].append(lf_p.transpose(0, 1, 3, 2).reshape(b, t, n_heads))
        outs["rp"].append(ret_p)
        outs["cp"].append(conv_p)

        sbf, sk, sv, sg, lfs_t, _ = _inproj(hs, mod3, 0, w_main, wf_t, bf_col, cos_s, sin_s,
                                            tm=tm_s, seq_tiled=False)
        rows_q = ts * n_heads
        q3 = sbf[:, _BF_QA * wa:(_BF_QA + 1) * wa].reshape(bs, rows_q, dh)
        k_new = sk.reshape(bs, ts, n_heads, dh)
        v_new = sv.reshape(bs, ts, n_heads, dh)
        lf_s = lfs_t[0, :n_heads, :].T.reshape(bs, ts, n_heads)
        lfn3 = jnp.pad(lf_s.reshape(bs, 1, rows_q), ((0, 0), (0, 0), (0, LANES - rows_q)))
        ck3 = cache_k[l].reshape(n_pool, page_size * n_heads, dh)
        cv3 = cache_v[l].reshape(n_pool, page_size * n_heads, dh)
        clf3 = cache_logf[l].astype(F32).reshape(n_pool, 1, page_size * n_heads)
        oa_s = _fox_decode(page_table, q3, k_new.reshape(bs, rows_q, dh),
                           v_new.reshape(bs, rows_q, dh), lfn3, ck3, cv3, clf3,
                           n_heads=n_heads, n_new=ts)
        sbf3 = sbf.reshape(bs, ts, -1)
        ob_s, ret_s = _retention(sbf3, sg.reshape(bs, ts, -1), state_ret[l], n_heads=n_heads,
                                 chunk=ts, bpb=8)
        x1s = _merge(oa_s.reshape(bs * ts, wa), ob_s.reshape(bs * ts, wa), sg, hs, mod3, 0,
                     wpa, wpb, wo, lng1, lnb1, tm=256, alpha=alpha)
        hs, conv_s = _ffn(x1s, mod3, 0, wup, wdown, conv_w[l], conv_b[l], lng2, lnb2,
                          state_conv[l], tm=1024, tf=512, alpha=alpha)
        outs["ks"].append(k_new)
        outs["vs"].append(v_new)
        outs["ls"].append(lf_s)
        outs["rs"].append(ret_s)
        outs["cs"].append(conv_s)

    st = lambda name: jnp.stack(outs[name])
    return (hp, hs, st("kp"), st("vp"), st("lp"), st("rp"), st("cp"),
            st("ks"), st("vs"), st("ls"), st("rs"), st("cs"))
```

```python
import functools
import math

import jax
import jax.numpy as jnp
from jax import lax
from jax.experimental import pallas as pl
from jax.experimental.pallas import tpu as pltpu

F32 = jnp.float32
BF16 = jnp.bfloat16

LANES = 128
SUBLANES = 8
VMEM_LIMIT = 56 * 1024 * 1024

ROPE_BASE = 10000.0
LN_EPS = 1e-5
GN_EPS = 1e-5
NEG_BIG = -1e30
LOG2E = math.log2(math.e)

_NT = (((1,), (1,)), ((), ()))
_TN = (((0,), (0,)), ((), ()))


def _params(*sem):
    return pltpu.CompilerParams(dimension_semantics=sem, vmem_limit_bytes=VMEM_LIMIT)


def _layer_norm(r, g, b):
    mu = jnp.mean(r, axis=-1, keepdims=True)
    d = r - mu
    var = jnp.mean(d * d, axis=-1, keepdims=True)
    return d * lax.rsqrt(var + LN_EPS) * g + b


def _ada_kernel(c_ref, w_ref, b_ref, o_ref):
    c = c_ref[...]
    s = (c * jax.nn.sigmoid(c)).astype(BF16)
    o_ref[...] = jnp.dot(s, w_ref[...].astype(BF16), preferred_element_type=F32) + b_ref[...]


def _ada(c_all, w_ada, b_ada):
    rows, d = c_all.shape
    n = w_ada.shape[1]
    tn = 512
    return pl.pallas_call(
        _ada_kernel,
        grid=(n // tn,),
        in_specs=[pl.BlockSpec((rows, d), lambda j: (0, 0)),
                  pl.BlockSpec((d, tn), lambda j: (0, j)),
                  pl.BlockSpec((1, tn), lambda j: (0, j))],
        out_specs=pl.BlockSpec((rows, tn), lambda j: (0, j)),
        out_shape=jax.ShapeDtypeStruct((rows, n), F32),
        compiler_params=_params("arbitrary"),
        name="ada_mod",
    )(c_all, w_ada, b_ada.reshape(1, n))


(_G_QA, _G_KA, _G_VA, _G_QR, _G_KR, _G_VR, _G_GA0, _G_GA1, _G_GB0, _G_GB1, _G_GR) = range(11)
_N_GROUPS = 11
_W_BLOCK_OF_GR = 6
_BF_QA, _BF_QR, _BF_KR, _BF_VR = 0, 1, 2, 3
_GATE_GA, _GATE_GB, _GATE_GR = 0, 2, 4


def _w_block(j):
    return jnp.where(j < _G_GA0, j, jnp.where(j < _G_GR, j + 1, _W_BLOCK_OF_GR))


def _bf_block(j):
    return jnp.clip(j - (_G_QR - _BF_QR), _BF_QA, _BF_VR)


def _gate_block(j):
    return jnp.clip(j - _G_GA0, _GATE_GA, _GATE_GR)


def _repack_kernel(wt_ref, o_ref):
    o_ref[...] = wt_ref[0].T.astype(BF16)


def _repack_w_in(w_in_t, layer, fa0, n_f):
    _, n, d = w_in_t.shape
    tc = 1024
    assert fa0 % tc == 0 and (n - n_f) % tc == 0

    def rows(j):
        return pl.multiple_of(jnp.where(j < fa0 // tc, j * tc, j * tc + n_f), SUBLANES)

    return pl.pallas_call(
        _repack_kernel,
        grid=((n - n_f) // tc,),
        in_specs=[pl.BlockSpec((pl.Element(1), pl.Element(tc), pl.Element(d)),
                               lambda j: (layer, rows(j), 0))],
        out_specs=pl.BlockSpec((d, tc), lambda j: (0, j)),
        out_shape=jax.ShapeDtypeStruct((d, n - n_f), BF16),
        compiler_params=_params("arbitrary"),
        name="repack_w_in",
    )(w_in_t)


def _lane_cumsum(x, carry):
    rows, width = x.shape
    lane = lax.broadcasted_iota(jnp.int32, (rows, LANES), 1)
    out = []
    for c in range(width // LANES):
        v = x[:, c * LANES:(c + 1) * LANES]
        s = 1
        while s < LANES:
            v = v + jnp.where(lane >= s, pltpu.roll(v, s, axis=1), 0.0)
            s *= 2
        out.append(v + carry)
        carry = carry + jnp.broadcast_to(v[:, LANES - 1:LANES], (rows, LANES))
    return jnp.concatenate(out, axis=1), carry


def _x_tile_copy(x_hbm, xbuf, sem, tile, *, tm, tiles_per_seq):
    if tiles_per_seq > 1:
        src = x_hbm.at[pl.ds(tile // tiles_per_seq, 1), pl.ds((tile % tiles_per_seq) * tm, tm)]
    else:
        gb = xbuf.shape[0]
        src = x_hbm.at[pl.ds(tile * gb, gb)]
    return pltpu.make_async_copy(src, xbuf, sem)


def _inproj_kernel(x_hbm, sc_ref, sh_ref, w_ref, wf_ref, bf_ref, cos_ref, sin_ref,
                   obf_ref, ok_ref, ov_ref, og_ref, lf_ref, c_ref, u_sc, carry_sc, xbuf, xsem,
                   *, tm, tiles_per_seq, qa_scale, kr_scale, n_heads):
    i = pl.program_id(0)
    j = pl.program_id(1)
    x_copy = functools.partial(_x_tile_copy, x_hbm, xbuf, xsem, tm=tm,
                               tiles_per_seq=tiles_per_seq)

    @pl.when(j == 0)
    def _():
        @pl.when(i == 0)
        def _():
            x_copy(0).start()

        x_copy(i).wait()
        x = xbuf[...]
        u = (x * (1.0 + sc_ref[...]) + sh_ref[...]).reshape(tm, x.shape[-1]).astype(BF16)
        u_sc[...] = u

        @pl.when(i + 1 < pl.num_programs(0))
        def _():
            x_copy(i + 1).start()

        z = lax.dot_general(wf_ref[...].astype(BF16), u, _NT,
                            preferred_element_type=F32) + bf_ref[...]
        lf = jnp.minimum(z, 0.0) - jnp.log1p(jnp.exp(-jnp.abs(z)))
        lf_ref[0] = lf
        first = (i % tiles_per_seq) == 0
        prev = jnp.where(first, 0.0, carry_sc[...])
        c, last = _lane_cumsum(lf, prev)
        c_ref[0] = c * LOG2E
        carry_sc[...] = last

    def project():
        return jnp.dot(u_sc[...], w_ref[...], preferred_element_type=F32)

    @pl.when(j == _G_QA)
    def _():
        obf_ref[...] = (project() * qa_scale).astype(BF16)

    def rope(scale):
        acc = project()
        cos = cos_ref[...]
        sin = sin_ref[...]
        for h in range(n_heads):
            sl = slice(h * LANES, (h + 1) * LANES)
            a = acc[:, sl]
            r = a * cos + pltpu.roll(a, LANES // 2, axis=1) * sin
            if scale is not None:
                r = r * scale
            obf_ref[:, sl] = r.astype(BF16)

    @pl.when(j == _G_QR)
    def _():
        rope(None)

    @pl.when(j == _G_KR)
    def _():
        rope(kr_scale)

    @pl.when(j == _G_VR)
    def _():
        obf_ref[...] = project().astype(BF16)

    @pl.when(j == _G_KA)
    def _():
        ok_ref[...] = project()

    @pl.when(j == _G_VA)
    def _():
        ov_ref[...] = project()

    @pl.when(j >= _G_GA0)
    def _():
        og_ref[...] = project()


def _inproj(x3, mod3, mod_group0, w_main, wf_t, bf_col, cos_t, sin_t, *, tm, seq_tiled):
    g, r, d = x3.shape
    m = g * r
    n_tiles = m // tm
    if seq_tiled:
        tps = r // tm
        gb = 1
        x_block = (1, tm, d)
        mod_idx = lambda k: (lambda i, j: (mod_group0 + i // tps, 0, k))
        tab_spec = pl.BlockSpec((tm, LANES), lambda i, j: (i % tps, 0))
    else:
        tps = 1
        gb = tm // r
        x_block = (gb, r, d)
        mod_idx = lambda k: (lambda i, j: (mod_group0 // gb + i, 0, k))
        tab_spec = pl.BlockSpec((tm, LANES), lambda i, j: (0, 0))
    n_bf = _BF_VR + 1
    kern = functools.partial(_inproj_kernel, tm=tm, tiles_per_seq=tps,
                             qa_scale=LANES ** -0.5 * LOG2E, kr_scale=LANES ** -0.5,
                             n_heads=1024 // LANES)
    return pl.pallas_call(
        kern,
        grid=(n_tiles, _N_GROUPS),
        in_specs=[pl.BlockSpec(memory_space=pl.ANY),
                  pl.BlockSpec((gb, 1, d), mod_idx(1)),
                  pl.BlockSpec((gb, 1, d), mod_idx(0)),
                  pl.BlockSpec((d, 1024), lambda i, j: (0, _w_block(j))),
                  pl.BlockSpec((16, d), lambda i, j: (0, 0)),
                  pl.BlockSpec((16, 1), lambda i, j: (0, 0)),
                  tab_spec, tab_spec],
        out_specs=[pl.BlockSpec((tm, 1024), lambda i, j: (i, _bf_block(j))),
                   pl.BlockSpec((tm, 1024), lambda i, j: (i, 0)),
                   pl.BlockSpec((tm, 1024), lambda i, j: (i, 0)),
                   pl.BlockSpec((tm, 1024), lambda i, j: (i, _gate_block(j))),
                   pl.BlockSpec((1, 16, tm), lambda i, j: (i, 0, 0)),
                   pl.BlockSpec((1, 16, tm), lambda i, j: (i, 0, 0))],
        out_shape=[jax.ShapeDtypeStruct((m, 1024 * n_bf), BF16),
                   jax.ShapeDtypeStruct((m, 1024), F32),
                   jax.ShapeDtypeStruct((m, 1024), F32),
                   jax.ShapeDtypeStruct((m, 1024 * (_GATE_GR + 1)), F32),
                   jax.ShapeDtypeStruct((n_tiles, 16, tm), F32),
                   jax.ShapeDtypeStruct((n_tiles, 16, tm), F32)],
        scratch_shapes=[pltpu.VMEM((tm, d), BF16), pltpu.VMEM((16, LANES), F32),
                        pltpu.VMEM(x_block, F32), pltpu.SemaphoreType.DMA(())],
        compiler_params=_params("arbitrary", "arbitrary"),
        name="inproj",
    )(x3, mod3, mod3, w_main, wf_t, bf_col, cos_t, sin_t)


def _fox_prefill_body(q_ref, k_ref, v_ref, c_ref, o_ref, hg, qi, *, tq, hpb):
    dh = LANES

    def chunk(kc, carry, diagonal):
        start = pl.multiple_of(kc * tq, tq)
        out = []
        for hh in range(hpb):
            m, l, acc = carry[hh]
            sl = slice(hh * dh, (hh + 1) * dh)
            kk = k_ref[0, pl.ds(start, tq), sl].astype(BF16)
            vv = v_ref[0, pl.ds(start, tq), sl].astype(BF16)
            s = lax.dot_general(q_ref[0, :, sl], kk, _NT, preferred_element_type=F32)
            s = s - c_ref[0, kc, pl.ds(hg * hpb + hh, 1), :]
            if diagonal:
                row = lax.broadcasted_iota(jnp.int32, (tq, tq), 0)
                col = lax.broadcasted_iota(jnp.int32, (tq, tq), 1)
                s = jnp.where(row >= col, s, NEG_BIG)
            m_new = jnp.maximum(m, jnp.max(s, axis=-1, keepdims=True))
            alpha = jnp.exp2(m - m_new)
            p = jnp.exp2(s - m_new)
            l = alpha * l + jnp.sum(p, axis=-1, keepdims=True)
            acc = alpha * acc + jnp.dot(p.astype(BF16), vv, preferred_element_type=F32)
            out.append((m_new, l, acc))
        return tuple(out)

    init = tuple((jnp.full((tq, 1), NEG_BIG, F32), jnp.zeros((tq, 1), F32),
                  jnp.zeros((tq, dh), F32)) for _ in range(hpb))
    carry = lax.fori_loop(0, qi, lambda kc, c: chunk(kc, c, False), init)
    carry = chunk(qi, carry, True)
    for hh in range(hpb):
        _, l, acc = carry[hh]
        o_ref[0, :, hh * dh:(hh + 1) * dh] = (acc / l).astype(BF16)


def _periodic_tail(v, lane, n_heads):
    y = jnp.where(lane >= LANES - n_heads, v, 0.0)
    s = n_heads
    while s < LANES:
        y = y + pltpu.roll(y, LANES - s, axis=1)
        s *= 2
    return y


def _page_copies(pt_ref, ck_hbm, cv_hbm, clf_hbm, kbuf, vbuf, lfbuf, sems, seq, slot, n_pages):
    copies = []
    for p in range(n_pages):
        page = pt_ref[seq * n_pages + p]
        copies.append(pltpu.make_async_copy(ck_hbm.at[page], kbuf.at[slot, p], sems.at[slot, 0]))
        copies.append(pltpu.make_async_copy(cv_hbm.at[page], vbuf.at[slot, p], sems.at[slot, 1]))
        copies.append(pltpu.make_async_copy(clf_hbm.at[page], lfbuf.at[slot, p], sems.at[slot, 2]))
    return copies


def _fox_decode_issue(b, copies):
    @pl.when(b == 0)
    def _():
        for cp in copies(seq=0, slot=0):
            cp.start()

    @pl.when(b + 1 < pl.num_programs(0))
    def _():
        for cp in copies(seq=b + 1, slot=1 - b % 2):
            cp.start()


def _fox_decode_attend(b, copies, q_ref, kn_ref, vn_ref, lfn_ref, o_ref, kbuf, vbuf, lfbuf,
                       lf_sc, s_sc, *, n_pages, n_heads, n_new):
    slot = b % 2
    for cp in copies(seq=b, slot=slot):
        cp.wait()

    k_refs = [kbuf.at[slot, p] for p in range(n_pages)]
    v_refs = [vbuf.at[slot, p] for p in range(n_pages)]
    lf_refs = [lfbuf.at[slot, p] for p in range(n_pages)]

    page_w = lfbuf.shape[-1]
    ppr = n_pages // SUBLANES
    n_chunks = ppr * page_w // LANES
    rows_q = n_new * n_heads

    for p in range(n_pages):
        r, part = divmod(p, ppr)
        lf_sc[r:r + 1, part * page_w:(part + 1) * page_w] = lf_refs[p][...] * LOG2E
    lane = lax.broadcasted_iota(jnp.int32, (SUBLANES, LANES), 1)
    sub = lax.broadcasted_iota(jnp.int32, (SUBLANES, LANES), 0)
    chunks = []
    carry = jnp.zeros((SUBLANES, LANES), F32)
    for c in range(n_chunks):
        v = lf_sc[:, c * LANES:(c + 1) * LANES]
        s = n_heads
        while s < LANES:
            v = v + jnp.where(lane >= s, pltpu.roll(v, s, axis=1), 0.0)
            s *= 2
        chunks.append(v + carry)
        carry = carry + _periodic_tail(v, lane, n_heads)
    inc = carry
    s = 1
    while s < SUBLANES:
        inc = inc + jnp.where(sub >= s, pltpu.roll(inc, s, axis=0), 0.0)
        s *= 2
    exc = jnp.where(sub >= 1, pltpu.roll(inc, 1, axis=0), 0.0)
    chunks = [v + exc for v in chunks]
    past_total = inc[SUBLANES - 1:SUBLANES, :]

    q = q_ref[0]
    dh = q.shape[-1]
    row_h = lax.broadcasted_iota(jnp.int32, (rows_q, page_w), 0) % n_heads
    col_h = lax.broadcasted_iota(jnp.int32, (rows_q, page_w), 1) % n_heads
    same_head = row_h == col_h

    cpp = page_w // LANES
    mx = jnp.full((rows_q, LANES), NEG_BIG, F32)
    for p in range(n_pages):
        r, part = divmod(p, ppr)
        ck = jnp.concatenate([chunks[part * cpp + c][r:r + 1, :] for c in range(cpp)], axis=1)
        kk = k_refs[p][...].astype(BF16)
        s = lax.dot_general(q, kk, _NT, preferred_element_type=F32)
        s = jnp.where(same_head, s - ck, NEG_BIG)
        s_sc[p] = s
        for c in range(cpp):
            mx = jnp.maximum(mx, s[:, c * LANES:(c + 1) * LANES])

    lane1 = lax.broadcasted_iota(jnp.int32, (1, LANES), 1)
    cn = lfn_ref[0] * LOG2E
    s = n_heads
    while s < rows_q:
        cn = cn + jnp.where(lane1 >= s, pltpu.roll(cn, s, axis=1), 0.0)
        s *= 2
    cn = cn + past_total
    s_new = lax.dot_general(q, kn_ref[0].astype(BF16), _NT, preferred_element_type=F32)
    row = lax.broadcasted_iota(jnp.int32, (rows_q, rows_q), 0)
    col = lax.broadcasted_iota(jnp.int32, (rows_q, rows_q), 1)
    ok = ((row % n_heads) == (col % n_heads)) & (col <= row)
    s_new = jnp.where(ok, s_new - cn[:, :rows_q], NEG_BIG)
    m = jnp.maximum(jnp.max(mx, axis=-1, keepdims=True), jnp.max(s_new, axis=-1, keepdims=True))

    lsum = jnp.zeros((rows_q, LANES), F32)
    acc = jnp.zeros((rows_q, dh), F32)
    for p in range(n_pages):
        pr = jnp.exp2(s_sc[p] - m)
        for c in range(cpp):
            lsum = lsum + pr[:, c * LANES:(c + 1) * LANES]
        acc = acc + jnp.dot(pr.astype(BF16), v_refs[p][...].astype(BF16),
                            preferred_element_type=F32)
    pr = jnp.exp2(s_new - m)
    l = jnp.sum(lsum, axis=-1, keepdims=True) + jnp.sum(pr, axis=-1, keepdims=True)
    acc = acc + jnp.dot(pr.astype(BF16), vn_ref[0].astype(BF16), preferred_element_type=F32)
    o_ref[0] = (acc / l).astype(BF16)


def _fox_kernel(pt_ref, qp_ref, kp_ref, vp_ref, cp_ref, q_ref, kn_ref, vn_ref, lfn_ref,
                ck_hbm, cv_hbm, clf_hbm, op_ref, o_ref, kbuf, vbuf, lfbuf, sems, lf_sc, s_sc,
                *, tq, hpb, n_hg, n_q, n_pages, n_heads, n_new):
    step = pl.program_id(0)
    copies = functools.partial(_page_copies, pt_ref, ck_hbm, cv_hbm, clf_hbm, kbuf, vbuf, lfbuf,
                               sems, n_pages=n_pages)
    _fox_decode_issue(step, copies)
    _fox_prefill_body(qp_ref, kp_ref, vp_ref, cp_ref, op_ref, (step // n_q) % n_hg, step % n_q,
                      tq=tq, hpb=hpb)
    _fox_decode_attend(step, copies, q_ref, kn_ref, vn_ref, lfn_ref, o_ref, kbuf, vbuf, lfbuf,
                       lf_sc, s_sc, n_pages=n_pages, n_heads=n_heads, n_new=n_new)


def _fox_attention(pbf3, k3, v3, c4, page_table, q3, kn3, vn3, lfn3, ck3, cv3, clf3,
                   *, n_heads, tq, hpb, n_new):
    b, t, _ = pbf3.shape
    w = hpb * LANES
    n_hg = n_heads // hpb
    n_q = t // tq
    bs, n_pages = page_table.shape
    rows_q, dh = q3.shape[1], q3.shape[2]
    page_rows = ck3.shape[1]
    page_w = clf3.shape[2]
    assert b * n_hg * n_q == bs, "prefill steps and decode sequences must pair up"

    def pre(col0):
        return lambda s, pt: (s // (n_hg * n_q), s % n_q, col0 + (s // n_q) % n_hg)

    def pre_kv(s, pt):
        return (s // (n_hg * n_q), 0, (s // n_q) % n_hg)

    def same(s, pt):
        return (s, 0, 0)

    hbm = pl.BlockSpec(memory_space=pl.ANY)
    grid_spec = pltpu.PrefetchScalarGridSpec(
        num_scalar_prefetch=1,
        grid=(bs,),
        in_specs=[pl.BlockSpec((1, tq, w), pre(_BF_QA * n_hg)),
                  pl.BlockSpec((1, t, w), pre_kv),
                  pl.BlockSpec((1, t, w), pre_kv),
                  pl.BlockSpec((1, n_q, 16, tq), lambda s, pt: (s // (n_hg * n_q), 0, 0, 0)),
                  pl.BlockSpec((1, rows_q, dh), same),
                  pl.BlockSpec((1, rows_q, dh), same),
                  pl.BlockSpec((1, rows_q, dh), same),
                  pl.BlockSpec((1, 1, LANES), same),
                  hbm, hbm, hbm],
        out_specs=[pl.BlockSpec((1, tq, w), pre(0)),
                   pl.BlockSpec((1, rows_q, dh), same)],
        scratch_shapes=[pltpu.VMEM((2, n_pages, page_rows, dh), F32),
                        pltpu.VMEM((2, n_pages, page_rows, dh), F32),
                        pltpu.VMEM((2, n_pages, 1, page_w), F32),
                        pltpu.SemaphoreType.DMA((2, 3)),
                        pltpu.VMEM((SUBLANES, n_pages // SUBLANES * page_w), F32),
                        pltpu.VMEM((n_pages, rows_q, page_w), F32)],
    )
    return pl.pallas_call(
        functools.partial(_fox_kernel, tq=tq, hpb=hpb, n_hg=n_hg, n_q=n_q, n_pages=n_pages,
                          n_heads=n_heads, n_new=n_new),
        grid_spec=grid_spec,
        out_shape=[jax.ShapeDtypeStruct((b, t, n_heads * LANES), BF16),
                   jax.ShapeDtypeStruct((bs, rows_q, dh), BF16)],
        compiler_params=_params("arbitrary"),
        name="fox_attention",
    )(page_table.reshape(-1), pbf3, k3, v3, c4, q3, kn3, vn3, lfn3, ck3, cv3, clf3)


def _ret_kernel(*refs, n_heads, has_state, bpb):
    if has_state:
        (q_ref, k_ref, v_ref, g_ref, dmat_ref, qdec_ref, kdec_ref, sdec_ref, s0_ref,
         o_ref, sout_ref, s_sc) = refs
    else:
        (q_ref, k_ref, v_ref, g_ref, dmat_ref, qdec_ref, kdec_ref, sdec_ref,
         o_ref, sout_ref, s_sc) = refs
    c = pl.program_id(1)

    @pl.when(c == 0)
    def _():
        if has_state:
            s_sc[...] = s0_ref[...]
        else:
            s_sc[...] = jnp.zeros_like(s_sc)

    for bb in range(bpb):
        for h in range(n_heads):
            sl = slice(h * LANES, (h + 1) * LANES)
            q = q_ref[bb, :, sl]
            k = k_ref[bb, :, sl]
            v = v_ref[bb, :, sl]
            st = s_sc[bb, h]
            att = lax.dot_general(q, k, _NT, preferred_element_type=F32) * dmat_ref[h]
            inner = jnp.dot(att.astype(BF16), v, preferred_element_type=F32)
            cross = jnp.dot(q, st.astype(BF16), preferred_element_type=F32) * qdec_ref[h]
            o = inner + cross
            kw = (k.astype(F32) * kdec_ref[h]).astype(BF16)
            s_sc[bb, h] = sdec_ref[h] * st + lax.dot_general(kw, v, _TN,
                                                            preferred_element_type=F32)
            mu = jnp.mean(o, axis=-1, keepdims=True)
            d = o - mu
            var = jnp.mean(d * d, axis=-1, keepdims=True)
            g = g_ref[bb, :, sl]
            o_ref[bb, :, sl] = (d * lax.rsqrt(var + GN_EPS)
                                * (g * jax.nn.sigmoid(g))).astype(BF16)

    @pl.when(c == pl.num_programs(1) - 1)
    def _():
        sout_ref[...] = s_sc[...]


def _retention_tables(n_heads, chunk, dk):
    lg = jnp.log(1.0 - jnp.exp2(-5.0 - jnp.arange(n_heads, dtype=F32)))
    idx = jnp.arange(chunk, dtype=F32)
    diff = idx[:, None] - idx[None, :]
    dmat = jnp.where(diff[None] >= 0, jnp.exp(diff[None] * lg[:, None, None]), 0.0)
    qdec = jnp.exp((idx + 1.0)[None, :] * lg[:, None])
    kdec = jnp.exp((chunk - 1.0 - idx)[None, :] * lg[:, None])
    sdec = jnp.exp(chunk * lg)
    bc = lambda a: jnp.broadcast_to(a[:, :, None], a.shape + (dk,))
    return dmat, bc(qdec), bc(kdec), jnp.broadcast_to(sdec[:, None, None], (n_heads, 1, dk))


def _retention(pbf3, gates3, state, *, n_heads, chunk, bpb):
    b, t, _ = pbf3.shape
    w = n_heads * LANES
    n_chunks = t // chunk
    tabs = _retention_tables(n_heads, chunk, LANES)
    has_state = state is not None
    const3 = lambda bi, c: (0, 0, 0)
    in_specs = [pl.BlockSpec((bpb, chunk, w), lambda bi, c: (bi, c, _BF_QR)),
                pl.BlockSpec((bpb, chunk, w), lambda bi, c: (bi, c, _BF_KR)),
                pl.BlockSpec((bpb, chunk, w), lambda bi, c: (bi, c, _BF_VR)),
                pl.BlockSpec((bpb, chunk, w), lambda bi, c: (bi, c, _GATE_GR)),
                pl.BlockSpec((n_heads, chunk, chunk), const3),
                pl.BlockSpec((n_heads, chunk, LANES), const3),
                pl.BlockSpec((n_heads, chunk, LANES), const3),
                pl.BlockSpec((n_heads, 1, LANES), const3)]
    args = [pbf3, pbf3, pbf3, gates3, *tabs]
    state_spec = pl.BlockSpec((bpb, n_heads, LANES, LANES), lambda bi, c: (bi, 0, 0, 0))
    if has_state:
        in_specs.append(state_spec)
        args.append(state)
    return pl.pallas_call(
        functools.partial(_ret_kernel, n_heads=n_heads, has_state=has_state, bpb=bpb),
        grid=(b // bpb, n_chunks),
        in_specs=in_specs,
        out_specs=[pl.BlockSpec((bpb, chunk, w), lambda bi, c: (bi, c, 0)), state_spec],
        out_shape=[jax.ShapeDtypeStruct((b, t, w), BF16),
                   jax.ShapeDtypeStruct((b, n_heads, LANES, LANES), F32)],
        scratch_shapes=[pltpu.VMEM((bpb, n_heads, LANES, LANES), F32)],
        compiler_params=_params("arbitrary", "arbitrary"),
        name="retention",
    )(*args)


def _merge_kernel(oa_ref, ob_ref, ga_ref, gb_ref, x_ref, g1_ref, wpa_ref, wpb_ref, wo_ref,
                  lng_ref, lnb_ref, o_ref, *, alpha):
    a = jnp.dot(oa_ref[...], wpa_ref[...], preferred_element_type=F32)
    b = jnp.dot(ob_ref[...], wpb_ref[...], preferred_element_type=F32)
    merged = jax.nn.sigmoid(ga_ref[...]) * a + jax.nn.sigmoid(gb_ref[...]) * b
    y = jnp.dot(merged.astype(BF16), wo_ref[...], preferred_element_type=F32)
    x = x_ref[...]
    r = alpha * x + g1_ref[...] * y.reshape(x.shape)
    o_ref[...] = _layer_norm(r, lng_ref[...], lnb_ref[...])


def _row_tiling(x3, mod_group0, tm, n_grid_axes):
    g, r, d = x3.shape
    if r >= tm:
        tps = r // tm
        gb = 1
        if n_grid_axes == 1:
            x_map = lambda i: (i // tps, i % tps, 0)
            mod_idx = lambda k: (lambda i: (mod_group0 + i // tps, 0, k))
        else:
            x_map = lambda i, j: (i // tps, i % tps, 0)
            mod_idx = lambda k: (lambda i, j: (mod_group0 + i // tps, 0, k))
        x_spec = pl.BlockSpec((1, tm, d), x_map)
    else:
        tps = 1
        gb = tm // r
        if n_grid_axes == 1:
            x_map = lambda i: (i, 0, 0)
            mod_idx = lambda k: (lambda i: (mod_group0 // gb + i, 0, k))
        else:
            x_map = lambda i, j: (i, 0, 0)
            mod_idx = lambda k: (lambda i, j: (mod_group0 // gb + i, 0, k))
        x_spec = pl.BlockSpec((gb, r, d), x_map)
    mod_spec = lambda k: pl.BlockSpec((gb, 1, d), mod_idx(k))
    return x_spec, mod_spec, tps, gb


def _merge(oa2, ob2, gates, x3, mod3, mod_group0, wpa, wpb, wo, lng, lnb, *, tm, alpha):
    g, r, d = x3.shape
    m = g * r
    wa = oa2.shape[1]
    x_spec, mod_spec, _, _ = _row_tiling(x3, mod_group0, tm, 1)
    const = lambda i: (0, 0)
    return pl.pallas_call(
        functools.partial(_merge_kernel, alpha=alpha),
        grid=(m // tm,),
        in_specs=[pl.BlockSpec((tm, wa), lambda i: (i, 0)),
                  pl.BlockSpec((tm, wa), lambda i: (i, 0)),
                  pl.BlockSpec((tm, d), lambda i: (i, _GATE_GA // 2)),
                  pl.BlockSpec((tm, d), lambda i: (i, _GATE_GB // 2)),
                  x_spec,
                  mod_spec(2),
                  pl.BlockSpec(wpa.shape, const, pipeline_mode=pl.Buffered(1)),
                  pl.BlockSpec(wpb.shape, const, pipeline_mode=pl.Buffered(1)),
                  pl.BlockSpec(wo.shape, const, pipeline_mode=pl.Buffered(1)),
                  pl.BlockSpec((1, d), const),
                  pl.BlockSpec((1, d), const)],
        out_specs=x_spec,
        out_shape=jax.ShapeDtypeStruct(x3.shape, F32),
        compiler_params=_params("arbitrary"),
        name="merge_outproj",
    )(oa2, ob2, gates, gates, x3, mod3, wpa, wpb, wo, lng, lnb)


def _ffn_kernel(*refs, tm, tiles_per_seq, alpha, with_state, conv_w):
    if with_state:
        (x_ref, sc_ref, sh_ref, g2_ref, wa_ref, wb_ref, wd_ref, cw_ref, cb_ref, lng_ref, lnb_ref,
         st_ref, o_ref, tail_ref, u_sc, abuf) = refs
    else:
        (x_ref, sc_ref, sh_ref, g2_ref, wa_ref, wb_ref, wd_ref, cw_ref, cb_ref, lng_ref, lnb_ref,
         o_ref, tail_ref, u_sc, abuf, carry_sc) = refs
    i = pl.program_id(0)
    j = pl.program_id(1)
    hist = conv_w - 1

    @pl.when(j == 0)
    def _():
        x = x_ref[...]
        u_sc[...] = (x * (1.0 + sc_ref[...]) + sh_ref[...]).reshape(tm, x.shape[-1]).astype(BF16)
        o_ref[...] = jnp.zeros_like(o_ref)

    u = u_sc[...]
    a = jnp.dot(u, wa_ref[...], preferred_element_type=F32)
    b = jnp.dot(u, wb_ref[...], preferred_element_type=F32)
    tf = a.shape[-1]
    cw = cw_ref[...]
    if with_state:
        gb = tm // SUBLANES
        a3 = a.reshape(gb, SUBLANES, tf)
        abuf[:, SUBLANES:2 * SUBLANES, :] = a3
        abuf[:, SUBLANES - hist:SUBLANES, :] = st_ref[...]
        shifted = [abuf[:, SUBLANES - hist + w:2 * SUBLANES - hist + w, :].reshape(tm, tf)
                   for w in range(hist)]
        tail_ref[...] = a3[:, SUBLANES - hist:, :]
    else:
        first = (i % tiles_per_seq) == 0
        abuf[0:SUBLANES, :] = jnp.where(first, 0.0, carry_sc[j])
        abuf[SUBLANES:SUBLANES + tm, :] = a
        shifted = [abuf[SUBLANES - hist + w:SUBLANES - hist + w + tm, :] for w in range(hist)]
        carry_sc[j] = a[tm - SUBLANES:, :]
        tail_ref[0] = a[tm - hist:, :]
    conv = a * cw[hist:hist + 1, :]
    for w in range(hist):
        conv = conv + shifted[w] * cw[w:w + 1, :]
    conv = conv + cb_ref[...]
    hg = (jax.nn.gelu(conv) * b).astype(BF16)
    o_ref[...] += jnp.dot(hg, wd_ref[...], preferred_element_type=F32).reshape(o_ref.shape)

    @pl.when(j == pl.num_programs(1) - 1)
    def _():
        x = x_ref[...]
        r = alpha * x + g2_ref[...] * o_ref[...]
        o_ref[...] = _layer_norm(r, lng_ref[...], lnb_ref[...])


def _ffn(x3, mod3, mod_group0, wup, wdown, conv_w, conv_b, lng, lnb, conv_state,
         *, tm, tf, alpha):
    g, r, d = x3.shape
    m = g * r
    dff = wdown.shape[0]
    nj = dff // tf
    cwid = conv_w.shape[0]
    hist = cwid - 1
    x_spec, mod_spec, tps, gb = _row_tiling(x3, mod_group0, tm, 2)
    with_state = conv_state is not None
    const = lambda i, j: (0, 0)
    x_in_spec = pl.BlockSpec(x_spec.block_shape, x_spec.index_map, pipeline_mode=pl.Buffered(1))
    in_specs = [x_in_spec, mod_spec(4), mod_spec(3), mod_spec(5),
                pl.BlockSpec((d, tf), lambda i, j: (0, j)),
                pl.BlockSpec((d, tf), lambda i, j: (0, nj + j)),
                pl.BlockSpec((tf, d), lambda i, j: (j, 0)),
                pl.BlockSpec((cwid, tf), lambda i, j: (0, j)),
                pl.BlockSpec((1, tf), lambda i, j: (0, j)),
                pl.BlockSpec((1, d), const),
                pl.BlockSpec((1, d), const)]
    args = [x3, mod3, mod3, mod3, wup, wup, wdown, conv_w, conv_b.reshape(1, dff), lng, lnb]
    scratch = [pltpu.VMEM((tm, d), BF16)]
    if with_state:
        assert r == SUBLANES
        in_specs.append(pl.BlockSpec((gb, hist, tf), lambda i, j: (i, 0, j)))
        args.append(conv_state)
        tail_spec = pl.BlockSpec((gb, hist, tf), lambda i, j: (i, 0, j))
        scratch.append(pltpu.VMEM((gb, 2 * SUBLANES, tf), F32))
    else:
        tail_spec = pl.BlockSpec((1, hist, tf), lambda i, j: (i, 0, j))
        scratch += [pltpu.VMEM((tm + SUBLANES, tf), F32), pltpu.VMEM((nj, SUBLANES, tf), F32)]
    n_tails = g if with_state else m // tm
    kern = functools.partial(_ffn_kernel, tm=tm, tiles_per_seq=tps, alpha=alpha,
                             with_state=with_state, conv_w=cwid)
    y, tails = pl.pallas_call(
        kern,
        grid=(m // tm, nj),
        in_specs=in_specs,
        out_specs=[x_spec, tail_spec],
        out_shape=[jax.ShapeDtypeStruct(x3.shape, F32),
                   jax.ShapeDtypeStruct((n_tails, hist, dff), F32)],
        scratch_shapes=scratch,
        compiler_params=_params("arbitrary", "arbitrary"),
        name="convffn",
    )(*args)
    if not with_state:
        tails = tails.reshape(g, tps, hist, dff)[:, tps - 1]
    return y, tails


def _rope_tables(pos, half):
    inv = ROPE_BASE ** (-jnp.arange(half, dtype=F32) / half)
    ang = pos.astype(F32)[:, None] * inv[None, :]
    cos, sin = jnp.cos(ang), jnp.sin(ang)
    return jnp.concatenate([cos, cos], axis=-1), jnp.concatenate([-sin, sin], axis=-1)


def kernel(x_prompt, x_sample, cache_k, cache_v, cache_logf, state_ret, state_conv, page_table,
           c_prompt, c_sample, w_ada, b_ada, w_in, b_f, w_pa, w_pb, w_o, ln1_g, ln1_b,
           w_up, conv_w, conv_b, w_down, ln2_g, ln2_b):
    depth = w_ada.shape[0]
    b, t, d = x_prompt.shape
    bs, ts, _ = x_sample.shape
    n_pool, page_size, n_heads, dh = cache_k.shape[1:]
    n_pages = page_table.shape[1]
    past = n_pages * page_size
    wa = n_heads * dh
    dff = w_down.shape[1]
    alpha = (2.0 * depth) ** 0.25
    assert wa == 1024 and dh == LANES and ts == SUBLANES and n_pages % SUBLANES == 0
    assert w_in.shape[2] == 7 * wa + n_heads + 2 * d and d == 2 * wa

    tm_p, tm_s = 1024, bs * ts
    tq = 512
    ret_chunk = 128 if t % 128 == 0 else t

    cos_p, sin_p = _rope_tables(jnp.arange(t), dh // 2)
    cos_s, sin_s = _rope_tables(past + jnp.arange(ts), dh // 2)
    cos_s, sin_s = jnp.tile(cos_s, (tm_s // ts, 1)), jnp.tile(sin_s, (tm_s // ts, 1))

    n_mod = -(-(bs + b) // 16) * 16
    c_all = jnp.concatenate([c_sample, c_prompt, jnp.zeros((n_mod - bs - b, d), F32)], axis=0)

    w_in_t = jnp.swapaxes(w_in, 1, 2)

    hp, hs = x_prompt, x_sample
    outs = {k: [] for k in ("kp", "vp", "lp", "rp", "cp", "ks", "vs", "ls", "rs", "cs")}
    for l in range(depth):
        mod3 = _ada(c_all, w_ada[l], b_ada[l]).reshape(n_mod, 1, 6 * d)

        fa0 = 3 * wa
        w_main = _repack_w_in(w_in_t, l, fa0, n_heads)
        wf_t = jnp.zeros((16, d), F32).at[:n_heads].set(w_in_t[l, fa0:fa0 + n_heads, :])
        bf_col = jnp.zeros((16, 1), F32).at[:n_heads, 0].set(b_f[l].astype(F32))
        wpa, wpb, wo = w_pa[l].astype(BF16), w_pb[l].astype(BF16), w_o[l].astype(BF16)
        wup = w_up[l].astype(BF16)
        wdown = w_down[l].astype(BF16)
        lng1, lnb1 = ln1_g[l].reshape(1, d), ln1_b[l].reshape(1, d)
        lng2, lnb2 = ln2_g[l].reshape(1, d), ln2_b[l].reshape(1, d)

        pbf, pk, pv, pg, lf_t, c_t = _inproj(hp, mod3, bs, w_main, wf_t, bf_col, cos_p, sin_p,
                                             tm=tm_p, seq_tiled=True)
        pbf3 = pbf.reshape(b, t, -1)
        nt = t // tm_p
        c4 = (c_t.reshape(b, nt, 16, tm_p // tq, tq).transpose(0, 1, 3, 2, 4)
              .reshape(b, t // tq, 16, tq))
        sbf, sk, sv, sg, lfs_t, _ = _inproj(hs, mod3, 0, w_main, wf_t, bf_col, cos_s, sin_s,
                                            tm=tm_s, seq_tiled=False)
        rows_q = ts * n_heads
        q3 = sbf[:, _BF_QA * wa:(_BF_QA + 1) * wa].reshape(bs, rows_q, dh)
        k_new = sk.reshape(bs, ts, n_heads, dh)
        v_new = sv.reshape(bs, ts, n_heads, dh)
        lf_s = lfs_t[0, :n_heads, :].T.reshape(bs, ts, n_heads)
        lfn3 = jnp.pad(lf_s.reshape(bs, 1, rows_q), ((0, 0), (0, 0), (0, LANES - rows_q)))
        ck3 = cache_k[l].reshape(n_pool, page_size * n_heads, dh)
        cv3 = cache_v[l].reshape(n_pool, page_size * n_heads, dh)
        clf3 = cache_logf[l].astype(F32).reshape(n_pool, 1, page_size * n_heads)
        oa, oa_s = _fox_attention(pbf3, pk.reshape(b, t, wa), pv.reshape(b, t, wa), c4,
                                  page_table, q3, k_new.reshape(bs, rows_q, dh),
                                  v_new.reshape(bs, rows_q, dh), lfn3, ck3, cv3, clf3,
                                  n_heads=n_heads, tq=tq, hpb=2, n_new=ts)

        ob, ret_p = _retention(pbf3, pg.reshape(b, t, -1), None, n_heads=n_heads,
                               chunk=ret_chunk, bpb=2)
        x1 = _merge(oa.reshape(b * t, wa), ob.reshape(b * t, wa), pg, hp, mod3, bs,
                    wpa, wpb, wo, lng1, lnb1, tm=256, alpha=alpha)
        hp, conv_p = _ffn(x1, mod3, bs, wup, wdown, conv_w[l], conv_b[l], lng2, lnb2,
                          None, tm=1024, tf=512, alpha=alpha)
        outs["kp"].append(pk.reshape(b, t, n_heads, dh))
        outs["vp"].append(pv.reshape(b, t, n_heads, dh))
        lf_p = lf_t[:, :n_heads, :].reshape(b, nt, n_heads, tm_p)
        outs["lp"].append(lf_p.transpose(0, 1, 3, 2).reshape(b, t, n_heads))
        outs["rp"].append(ret_p)
        outs["cp"].append(conv_p)

        sbf3 = sbf.reshape(bs, ts, -1)
        ob_s, ret_s = _retention(sbf3, sg.reshape(bs, ts, -1), state_ret[l], n_heads=n_heads,
                                 chunk=ts, bpb=8)
        x1s = _merge(oa_s.reshape(bs * ts, wa), ob_s.reshape(bs * ts, wa), sg, hs, mod3, 0,
                     wpa, wpb, wo, lng1, lnb1, tm=256, alpha=alpha)
        hs, conv_s = _ffn(x1s, mod3, 0, wup, wdown, conv_w[l], conv_b[l], lng2, lnb2,
                          state_conv[l], tm=1024, tf=512, alpha=alpha)
        outs["ks"].append(k_new)
        outs["vs"].append(v_new)
        outs["ls"].append(lf_s)
        outs["rs"].append(ret_s)
        outs["cs"].append(conv_s)

    st = lambda name: jnp.stack(outs[name])
    return (hp, hs, st("kp"), st("vp"), st("lp"), st("rp"), st("cp"),
            st("ks"), st("vs"), st("ls"), st("rs"), st("cs"))
```

```python
import functools
import math

import jax
import jax.numpy as jnp
from jax import lax
from jax.experimental import pallas as pl
from jax.experimental.pallas import tpu as pltpu

F32 = jnp.float32
BF16 = jnp.bfloat16

LANES = 128
SUBLANES = 8
VMEM_LIMIT = 56 * 1024 * 1024

ROPE_BASE = 10000.0
LN_EPS = 1e-5
GN_EPS = 1e-5
NEG_BIG = -1e30
LOG2E = math.log2(math.e)

_NT = (((1,), (1,)), ((), ()))
_TN = (((0,), (0,)), ((), ()))


def _params(*sem):
    return pltpu.CompilerParams(dimension_semantics=sem, vmem_limit_bytes=VMEM_LIMIT)


def _layer_norm(r, g, b):
    mu = jnp.mean(r, axis=-1, keepdims=True)
    d = r - mu
    var = jnp.mean(d * d, axis=-1, keepdims=True)
    return d * lax.rsqrt(var + LN_EPS) * g + b


def _ada_kernel(c_ref, w_ref, b_ref, o_ref):
    c = c_ref[...]
    s = (c * jax.nn.sigmoid(c)).astype(BF16)
    o_ref[...] = jnp.dot(s, w_ref[...].astype(BF16), preferred_element_type=F32) + b_ref[...]


def _ada(c_all, w_ada, b_ada):
    rows, d = c_all.shape
    n = w_ada.shape[1]
    tn = 512
    return pl.pallas_call(
        _ada_kernel,
        grid=(n // tn,),
        in_specs=[pl.BlockSpec((rows, d), lambda j: (0, 0)),
                  pl.BlockSpec((d, tn), lambda j: (0, j)),
                  pl.BlockSpec((1, tn), lambda j: (0, j))],
        out_specs=pl.BlockSpec((rows, tn), lambda j: (0, j)),
        out_shape=jax.ShapeDtypeStruct((rows, n), F32),
        compiler_params=_params("arbitrary"),
        name="ada_mod",
    )(c_all, w_ada, b_ada.reshape(1, n))


(_G_QA, _G_KA, _G_VA, _G_QR, _G_KR, _G_VR, _G_GA0, _G_GA1, _G_GB0, _G_GB1, _G_GR) = range(11)
_N_GROUPS = 11
_W_BLOCK_OF_GR = 6
_BF_QA, _BF_QR, _BF_KR, _BF_VR = 0, 1, 2, 3
_GATE_GA, _GATE_GB, _GATE_GR = 0, 2, 4


def _w_block(j):
    return jnp.where(j < _G_GA0, j, jnp.where(j < _G_GR, j + 1, _W_BLOCK_OF_GR))


def _bf_block(j):
    return jnp.clip(j - (_G_QR - _BF_QR), _BF_QA, _BF_VR)


def _gate_block(j):
    return jnp.clip(j - _G_GA0, _GATE_GA, _GATE_GR)


def _repack_kernel(wt_ref, o_ref):
    o_ref[...] = wt_ref[0].T.astype(BF16)


def _repack_w_in(w_in_t, layer, fa0, n_f):
    _, n, d = w_in_t.shape
    tc = 1024
    assert fa0 % tc == 0 and (n - n_f) % tc == 0

    def rows(j):
        return pl.multiple_of(jnp.where(j < fa0 // tc, j * tc, j * tc + n_f), SUBLANES)

    return pl.pallas_call(
        _repack_kernel,
        grid=((n - n_f) // tc,),
        in_specs=[pl.BlockSpec((pl.Element(1), pl.Element(tc), pl.Element(d)),
                               lambda j: (layer, rows(j), 0))],
        out_specs=pl.BlockSpec((d, tc), lambda j: (0, j)),
        out_shape=jax.ShapeDtypeStruct((d, n - n_f), BF16),
        compiler_params=_params("arbitrary"),
        name="repack_w_in",
    )(w_in_t)


def _lane_cumsum(x, carry):
    rows, width = x.shape
    lane = lax.broadcasted_iota(jnp.int32, (rows, LANES), 1)
    out = []
    for c in range(width // LANES):
        v = x[:, c * LANES:(c + 1) * LANES]
        s = 1
        while s < LANES:
            v = v + jnp.where(lane >= s, pltpu.roll(v, s, axis=1), 0.0)
            s *= 2
        out.append(v + carry)
        carry = carry + jnp.broadcast_to(v[:, LANES - 1:LANES], (rows, LANES))
    return jnp.concatenate(out, axis=1), carry


def _x_tile_copy(x_hbm, xbuf, sem, tile, *, tm, tiles_per_seq):
    if tiles_per_seq > 1:
        src = x_hbm.at[pl.ds(tile // tiles_per_seq, 1), pl.ds((tile % tiles_per_seq) * tm, tm)]
    else:
        gb = xbuf.shape[0]
        src = x_hbm.at[pl.ds(tile * gb, gb)]
    return pltpu.make_async_copy(src, xbuf, sem)


def _inproj_kernel(x_hbm, sc_ref, sh_ref, w_ref, wf_ref, bf_ref, cos_ref, sin_ref,
                   obf_ref, ok_ref, ov_ref, og_ref, lf_ref, c_ref, u_sc, carry_sc, xbuf, xsem,
                   *, tm, tiles_per_seq, qa_scale, kr_scale, n_heads):
    i = pl.program_id(0)
    j = pl.program_id(1)
    x_copy = functools.partial(_x_tile_copy, x_hbm, xbuf, xsem, tm=tm,
                               tiles_per_seq=tiles_per_seq)

    @pl.when(j == 0)
    def _():
        @pl.when(i == 0)
        def _():
            x_copy(0).start()

        x_copy(i).wait()
        x = xbuf[...]
        u = (x * (1.0 + sc_ref[...]) + sh_ref[...]).reshape(tm, x.shape[-1]).astype(BF16)
        u_sc[...] = u

        @pl.when(i + 1 < pl.num_programs(0))
        def _():
            x_copy(i + 1).start()

        z = lax.dot_general(wf_ref[...].astype(BF16), u, _NT,
                            preferred_element_type=F32) + bf_ref[...]
        lf = jnp.minimum(z, 0.0) - jnp.log1p(jnp.exp(-jnp.abs(z)))
        lf_ref[0] = lf
        first = (i % tiles_per_seq) == 0
        prev = jnp.where(first, 0.0, carry_sc[...])
        c, last = _lane_cumsum(lf, prev)
        c_ref[0] = c * LOG2E
        carry_sc[...] = last

    def project():
        return jnp.dot(u_sc[...], w_ref[...], preferred_element_type=F32)

    @pl.when(j == _G_QA)
    def _():
        obf_ref[...] = (project() * qa_scale).astype(BF16)

    def rope(scale):
        acc = project()
        cos = cos_ref[...]
        sin = sin_ref[...]
        for h in range(n_heads):
            sl = slice(h * LANES, (h + 1) * LANES)
            a = acc[:, sl]
            r = a * cos + pltpu.roll(a, LANES // 2, axis=1) * sin
            if scale is not None:
                r = r * scale
            obf_ref[:, sl] = r.astype(BF16)

    @pl.when(j == _G_QR)
    def _():
        rope(None)

    @pl.when(j == _G_KR)
    def _():
        rope(kr_scale)

    @pl.when(j == _G_VR)
    def _():
        obf_ref[...] = project().astype(BF16)

    @pl.when(j == _G_KA)
    def _():
        ok_ref[...] = project()

    @pl.when(j == _G_VA)
    def _():
        ov_ref[...] = project()

    @pl.when(j >= _G_GA0)
    def _():
        og_ref[...] = project()


def _inproj(x3, mod3, mod_group0, w_main, wf_t, bf_col, cos_t, sin_t, *, tm, seq_tiled):
    g, r, d = x3.shape
    m = g * r
    n_tiles = m // tm
    if seq_tiled:
        tps = r // tm
        gb = 1
        x_block = (1, tm, d)
        mod_idx = lambda k: (lambda i, j: (mod_group0 + i // tps, 0, k))
        tab_spec = pl.BlockSpec((tm, LANES), lambda i, j: (i % tps, 0))
    else:
        tps = 1
        gb = tm // r
        x_block = (gb, r, d)
        mod_idx = lambda k: (lambda i, j: (mod_group0 // gb + i, 0, k))
        tab_spec = pl.BlockSpec((tm, LANES), lambda i, j: (0, 0))
    n_bf = _BF_VR + 1
    kern = functools.partial(_inproj_kernel, tm=tm, tiles_per_seq=tps,
                             qa_scale=LANES ** -0.5 * LOG2E, kr_scale=LANES ** -0.5,
                             n_heads=1024 // LANES)
    return pl.pallas_call(
        kern,
        grid=(n_tiles, _N_GROUPS),
        in_specs=[pl.BlockSpec(memory_space=pl.ANY),
                  pl.BlockSpec((gb, 1, d), mod_idx(1)),
                  pl.BlockSpec((gb, 1, d), mod_idx(0)),
                  pl.BlockSpec((d, 1024), lambda i, j: (0, _w_block(j))),
                  pl.BlockSpec((16, d), lambda i, j: (0, 0)),
                  pl.BlockSpec((16, 1), lambda i, j: (0, 0)),
                  tab_spec, tab_spec],
        out_specs=[pl.BlockSpec((tm, 1024), lambda i, j: (i, _bf_block(j))),
                   pl.BlockSpec((tm, 1024), lambda i, j: (i, 0)),
                   pl.BlockSpec((tm, 1024), lambda i, j: (i, 0)),
                   pl.BlockSpec((tm, 1024), lambda i, j: (i, _gate_block(j))),
                   pl.BlockSpec((1, 16, tm), lambda i, j: (i, 0, 0)),
                   pl.BlockSpec((1, 16, tm), lambda i, j: (i, 0, 0))],
        out_shape=[jax.ShapeDtypeStruct((m, 1024 * n_bf), BF16),
                   jax.ShapeDtypeStruct((m, 1024), F32),
                   jax.ShapeDtypeStruct((m, 1024), F32),
                   jax.ShapeDtypeStruct((m, 1024 * (_GATE_GR + 1)), F32),
                   jax.ShapeDtypeStruct((n_tiles, 16, tm), F32),
                   jax.ShapeDtypeStruct((n_tiles, 16, tm), F32)],
        scratch_shapes=[pltpu.VMEM((tm, d), BF16), pltpu.VMEM((16, LANES), F32),
                        pltpu.VMEM(x_block, F32), pltpu.SemaphoreType.DMA(())],
        compiler_params=_params("arbitrary", "arbitrary"),
        name="inproj",
    )(x3, mod3, mod3, w_main, wf_t, bf_col, cos_t, sin_t)


def _fox_prefill_body(q_ref, k_ref, v_ref, c_ref, o_ref, hg, qi, *, tq, hpb):
    dh = LANES

    def chunk(kc, carry, diagonal):
        start = pl.multiple_of(kc * tq, tq)
        out = []
        for hh in range(hpb):
            m, l, acc = carry[hh]
            sl = slice(hh * dh, (hh + 1) * dh)
            kk = k_ref[0, pl.ds(start, tq), sl].astype(BF16)
            vv = v_ref[0, pl.ds(start, tq), sl].astype(BF16)
            s = lax.dot_general(q_ref[0, :, sl], kk, _NT, preferred_element_type=F32)
            s = s - c_ref[0, kc, pl.ds(hg * hpb + hh, 1), :]
            if diagonal:
                row = lax.broadcasted_iota(jnp.int32, (tq, tq), 0)
                col = lax.broadcasted_iota(jnp.int32, (tq, tq), 1)
                s = jnp.where(row >= col, s, NEG_BIG)
            m_new = jnp.maximum(m, jnp.max(s, axis=-1, keepdims=True))
            alpha = jnp.exp2(m - m_new)
            p = jnp.exp2(s - m_new)
            l = alpha * l + jnp.sum(p, axis=-1, keepdims=True)
            acc = alpha * acc + jnp.dot(p.astype(BF16), vv, preferred_element_type=F32)
            out.append((m_new, l, acc))
        return tuple(out)

    init = tuple((jnp.full((tq, 1), NEG_BIG, F32), jnp.zeros((tq, 1), F32),
                  jnp.zeros((tq, dh), F32)) for _ in range(hpb))
    carry = lax.fori_loop(0, qi, lambda kc, c: chunk(kc, c, False), init)
    carry = chunk(qi, carry, True)
    for hh in range(hpb):
        _, l, acc = carry[hh]
        o_ref[0, :, hh * dh:(hh + 1) * dh] = (acc / l).astype(BF16)


def _periodic_tail(v, lane, n_heads):
    y = jnp.where(lane >= LANES - n_heads, v, 0.0)
    s = n_heads
    while s < LANES:
        y = y + pltpu.roll(y, LANES - s, axis=1)
        s *= 2
    return y


def _page_copies(pt_ref, ck_hbm, cv_hbm, clf_hbm, kbuf, vbuf, lfbuf, sems, seq, slot, n_pages):
    copies = []
    for p in range(n_pages):
        page = pt_ref[seq * n_pages + p]
        copies.append(pltpu.make_async_copy(ck_hbm.at[page], kbuf.at[slot, p], sems.at[slot, 0]))
        copies.append(pltpu.make_async_copy(cv_hbm.at[page], vbuf.at[slot, p], sems.at[slot, 1]))
        copies.append(pltpu.make_async_copy(clf_hbm.at[page], lfbuf.at[slot, p], sems.at[slot, 2]))
    return copies


def _fox_decode_issue(b, copies):
    @pl.when(b == 0)
    def _():
        for cp in copies(seq=0, slot=0):
            cp.start()

    @pl.when(b + 1 < pl.num_programs(0))
    def _():
        for cp in copies(seq=b + 1, slot=1 - b % 2):
            cp.start()


def _fox_decode_attend(b, copies, q_ref, kn_ref, vn_ref, lfn_ref, o_ref, kbuf, vbuf, lfbuf,
                       lf_sc, s_sc, *, n_pages, n_heads, n_new):
    slot = b % 2
    for cp in copies(seq=b, slot=slot):
        cp.wait()

    k_refs = [kbuf.at[slot, p] for p in range(n_pages)]
    v_refs = [vbuf.at[slot, p] for p in range(n_pages)]
    lf_refs = [lfbuf.at[slot, p] for p in range(n_pages)]

    page_w = lfbuf.shape[-1]
    ppr = n_pages // SUBLANES
    n_chunks = ppr * page_w // LANES
    rows_q = n_new * n_heads

    for p in range(n_pages):
        r, part = divmod(p, ppr)
        lf_sc[r:r + 1, part * page_w:(part + 1) * page_w] = lf_refs[p][...] * LOG2E
    lane = lax.broadcasted_iota(jnp.int32, (SUBLANES, LANES), 1)
    sub = lax.broadcasted_iota(jnp.int32, (SUBLANES, LANES), 0)
    chunks = []
    carry = jnp.zeros((SUBLANES, LANES), F32)
    for c in range(n_chunks):
        v = lf_sc[:, c * LANES:(c + 1) * LANES]
        s = n_heads
        while s < LANES:
            v = v + jnp.where(lane >= s, pltpu.roll(v, s, axis=1), 0.0)
            s *= 2
        chunks.append(v + carry)
        carry = carry + _periodic_tail(v, lane, n_heads)
    inc = carry
    s = 1
    while s < SUBLANES:
        inc = inc + jnp.where(sub >= s, pltpu.roll(inc, s, axis=0), 0.0)
        s *= 2
    exc = jnp.where(sub >= 1, pltpu.roll(inc, 1, axis=0), 0.0)
    chunks = [v + exc for v in chunks]
    past_total = inc[SUBLANES - 1:SUBLANES, :]

    q = q_ref[0]
    dh = q.shape[-1]
    row_h = lax.broadcasted_iota(jnp.int32, (rows_q, page_w), 0) % n_heads
    col_h = lax.broadcasted_iota(jnp.int32, (rows_q, page_w), 1) % n_heads
    same_head = row_h == col_h

    cpp = page_w // LANES
    mx = jnp.full((rows_q, LANES), NEG_BIG, F32)
    for p in range(n_pages):
        r, part = divmod(p, ppr)
        ck = jnp.concatenate([chunks[part * cpp + c][r:r + 1, :] for c in range(cpp)], axis=1)
        kk = k_refs[p][...].astype(BF16)
        s = lax.dot_general(q, kk, _NT, preferred_element_type=F32)
        s = jnp.where(same_head, s - ck, NEG_BIG)
        s_sc[p] = s
        for c in range(cpp):
            mx = jnp.maximum(mx, s[:, c * LANES:(c + 1) * LANES])

    lane1 = lax.broadcasted_iota(jnp.int32, (1, LANES), 1)
    cn = lfn_ref[0] * LOG2E
    s = n_heads
    while s < rows_q:
        cn = cn + jnp.where(lane1 >= s, pltpu.roll(cn, s, axis=1), 0.0)
        s *= 2
    cn = cn + past_total
    s_new = lax.dot_general(q, kn_ref[0].astype(BF16), _NT, preferred_element_type=F32)
    row = lax.broadcasted_iota(jnp.int32, (rows_q, rows_q), 0)
    col = lax.broadcasted_iota(jnp.int32, (rows_q, rows_q), 1)
    ok = ((row % n_heads) == (col % n_heads)) & (col <= row)
    s_new = jnp.where(ok, s_new - cn[:, :rows_q], NEG_BIG)
    m = jnp.maximum(jnp.max(mx, axis=-1, keepdims=True), jnp.max(s_new, axis=-1, keepdims=True))

    lsum = jnp.zeros((rows_q, LANES), F32)
    acc = jnp.zeros((rows_q, dh), F32)
    for p in range(n_pages):
        pr = jnp.exp2(s_sc[p] - m)
        for c in range(cpp):
            lsum = lsum + pr[:, c * LANES:(c + 1) * LANES]
        acc = acc + jnp.dot(pr.astype(BF16), v_refs[p][...].astype(BF16),
                            preferred_element_type=F32)
    pr = jnp.exp2(s_new - m)
    l = jnp.sum(lsum, axis=-1, keepdims=True) + jnp.sum(pr, axis=-1, keepdims=True)
    acc = acc + jnp.dot(pr.astype(BF16), vn_ref[0].astype(BF16), preferred_element_type=F32)
    o_ref[0] = (acc / l).astype(BF16)


def _fox_kernel(pt_ref, qp_ref, kp_ref, vp_ref, cp_ref, q_ref, kn_ref, vn_ref, lfn_ref,
                ck_hbm, cv_hbm, clf_hbm, op_ref, o_ref, kbuf, vbuf, lfbuf, sems, lf_sc, s_sc,
                *, tq, hpb, n_hg, n_q, n_pages, n_heads, n_new):
    step = pl.program_id(0)
    copies = functools.partial(_page_copies, pt_ref, ck_hbm, cv_hbm, clf_hbm, kbuf, vbuf, lfbuf,
                               sems, n_pages=n_pages)
    _fox_decode_issue(step, copies)
    _fox_prefill_body(qp_ref, kp_ref, vp_ref, cp_ref, op_ref, (step // n_q) % n_hg, step % n_q,
                      tq=tq, hpb=hpb)
    _fox_decode_attend(step, copies, q_ref, kn_ref, vn_ref, lfn_ref, o_ref, kbuf, vbuf, lfbuf,
                       lf_sc, s_sc, n_pages=n_pages, n_heads=n_heads, n_new=n_new)


def _fox_attention(pbf3, k3, v3, c4, page_table, q3, kn3, vn3, lfn3, ck3, cv3, clf3,
                   *, n_heads, tq, hpb, n_new):
    b, t, _ = pbf3.shape
    w = hpb * LANES
    n_hg = n_heads // hpb
    n_q = t // tq
    bs, n_pages = page_table.shape
    rows_q, dh = q3.shape[1], q3.shape[2]
    page_rows = ck3.shape[1]
    page_w = clf3.shape[2]
    assert b * n_hg * n_q == bs, "prefill steps and decode sequences must pair up"

    def pre(col0):
        return lambda s, pt: (s // (n_hg * n_q), s % n_q, col0 + (s // n_q) % n_hg)

    def pre_kv(s, pt):
        return (s // (n_hg * n_q), 0, (s // n_q) % n_hg)

    def same(s, pt):
        return (s, 0, 0)

    hbm = pl.BlockSpec(memory_space=pl.ANY)
    grid_spec = pltpu.PrefetchScalarGridSpec(
        num_scalar_prefetch=1,
        grid=(bs,),
        in_specs=[pl.BlockSpec((1, tq, w), pre(_BF_QA * n_hg)),
                  pl.BlockSpec((1, t, w), pre_kv),
                  pl.BlockSpec((1, t, w), pre_kv),
                  pl.BlockSpec((1, n_q, 16, tq), lambda s, pt: (s // (n_hg * n_q), 0, 0, 0)),
                  pl.BlockSpec((1, rows_q, dh), same),
                  pl.BlockSpec((1, rows_q, dh), same),
                  pl.BlockSpec((1, rows_q, dh), same),
                  pl.BlockSpec((1, 1, LANES), same),
                  hbm, hbm, hbm],
        out_specs=[pl.BlockSpec((1, tq, w), pre(0)),
                   pl.BlockSpec((1, rows_q, dh), same)],
        scratch_shapes=[pltpu.VMEM((2, n_pages, page_rows, dh), F32),
                        pltpu.VMEM((2, n_pages, page_rows, dh), F32),
                        pltpu.VMEM((2, n_pages, 1, page_w), F32),
                        pltpu.SemaphoreType.DMA((2, 3)),
                        pltpu.VMEM((SUBLANES, n_pages // SUBLANES * page_w), F32),
                        pltpu.VMEM((n_pages, rows_q, page_w), F32)],
    )
    return pl.pallas_call(
        functools.partial(_fox_kernel, tq=tq, hpb=hpb, n_hg=n_hg, n_q=n_q, n_pages=n_pages,
                          n_heads=n_heads, n_new=n_new),
        grid_spec=grid_spec,
        out_shape=[jax.ShapeDtypeStruct((b, t, n_heads * LANES), BF16),
                   jax.ShapeDtypeStruct((bs, rows_q, dh), BF16)],
        compiler_params=_params("arbitrary"),
        name="fox_attention",
    )(page_table.reshape(-1), pbf3, k3, v3, c4, q3, kn3, vn3, lfn3, ck3, cv3, clf3)


def _ret_kernel(*refs, n_heads, has_state, bpb):
    if has_state:
        (q_ref, k_ref, v_ref, g_ref, dmat_ref, qdec_ref, kdec_ref, sdec_ref, s0_ref,
         o_ref, sout_ref, s_sc) = refs
    else:
        (q_ref, k_ref, v_ref, g_ref, dmat_ref, qdec_ref, kdec_ref, sdec_ref,
         o_ref, sout_ref, s_sc) = refs
    c = pl.program_id(1)

    @pl.when(c == 0)
    def _():
        if has_state:
            s_sc[...] = s0_ref[...]
        else:
            s_sc[...] = jnp.zeros_like(s_sc)

    for bb in range(bpb):
        for h in range(n_heads):
            sl = slice(h * LANES, (h + 1) * LANES)
            q = q_ref[bb, :, sl]
            k = k_ref[bb, :, sl]
            v = v_ref[bb, :, sl]
            st = s_sc[bb, h]
            att = lax.dot_general(q, k, _NT, preferred_element_type=F32) * dmat_ref[h]
            inner = jnp.dot(att.astype(BF16), v, preferred_element_type=F32)
            cross = jnp.dot(q, st.astype(BF16), preferred_element_type=F32) * qdec_ref[h]
            o = inner + cross
            kw = (k.astype(F32) * kdec_ref[h]).astype(BF16)
            s_sc[bb, h] = sdec_ref[h] * st + lax.dot_general(kw, v, _TN,
                                                            preferred_element_type=F32)
            mu = jnp.mean(o, axis=-1, keepdims=True)
            d = o - mu
            var = jnp.mean(d * d, axis=-1, keepdims=True)
            g = g_ref[bb, :, sl]
            o_ref[bb, :, sl] = (d * lax.rsqrt(var + GN_EPS)
                                * (g * jax.nn.sigmoid(g))).astype(BF16)

    @pl.when(c == pl.num_programs(1) - 1)
    def _():
        sout_ref[...] = s_sc[...]


def _retention_tables(n_heads, chunk, dk):
    lg = jnp.log(1.0 - jnp.exp2(-5.0 - jnp.arange(n_heads, dtype=F32)))
    idx = jnp.arange(chunk, dtype=F32)
    diff = idx[:, None] - idx[None, :]
    dmat = jnp.where(diff[None] >= 0, jnp.exp(diff[None] * lg[:, None, None]), 0.0)
    qdec = jnp.exp((idx + 1.0)[None, :] * lg[:, None])
    kdec = jnp.exp((chunk - 1.0 - idx)[None, :] * lg[:, None])
    sdec = jnp.exp(chunk * lg)
    bc = lambda a: jnp.broadcast_to(a[:, :, None], a.shape + (dk,))
    return dmat, bc(qdec), bc(kdec), jnp.broadcast_to(sdec[:, None, None], (n_heads, 1, dk))


def _retention(pbf3, gates3, state, *, n_heads, chunk, bpb):
    b, t, _ = pbf3.shape
    w = n_heads * LANES
    n_chunks = t // chunk
    tabs = _retention_tables(n_heads, chunk, LANES)
    has_state = state is not None
    const3 = lambda bi, c: (0, 0, 0)
    in_specs = [pl.BlockSpec((bpb, chunk, w), lambda bi, c: (bi, c, _BF_QR)),
                pl.BlockSpec((bpb, chunk, w), lambda bi, c: (bi, c, _BF_KR)),
                pl.BlockSpec((bpb, chunk, w), lambda bi, c: (bi, c, _BF_VR)),
                pl.BlockSpec((bpb, chunk, w), lambda bi, c: (bi, c, _GATE_GR)),
                pl.BlockSpec((n_heads, chunk, chunk), const3),
                pl.BlockSpec((n_heads, chunk, LANES), const3),
                pl.BlockSpec((n_heads, chunk, LANES), const3),
                pl.BlockSpec((n_heads, 1, LANES), const3)]
    args = [pbf3, pbf3, pbf3, gates3, *tabs]
    state_spec = pl.BlockSpec((bpb, n_heads, LANES, LANES), lambda bi, c: (bi, 0, 0, 0))
    if has_state:
        in_specs.append(state_spec)
        args.append(state)
    return pl.pallas_call(
        functools.partial(_ret_kernel, n_heads=n_heads, has_state=has_state, bpb=bpb),
        grid=(b // bpb, n_chunks),
        in_specs=in_specs,
        out_specs=[pl.BlockSpec((bpb, chunk, w), lambda bi, c: (bi, c, 0)), state_spec],
        out_shape=[jax.ShapeDtypeStruct((b, t, w), BF16),
                   jax.ShapeDtypeStruct((b, n_heads, LANES, LANES), F32)],
        scratch_shapes=[pltpu.VMEM((bpb, n_heads, LANES, LANES), F32)],
        compiler_params=_params("arbitrary", "arbitrary"),
        name="retention",
    )(*args)


def _merge_kernel(oa_ref, ob_ref, ga_ref, gb_ref, x_ref, g1_ref, wpa_ref, wpb_ref, wo_ref,
                  lng_ref, lnb_ref, o_ref, *, alpha):
    a = jnp.dot(oa_ref[...], wpa_ref[...], preferred_element_type=F32)
    b = jnp.dot(ob_ref[...], wpb_ref[...], preferred_element_type=F32)
    merged = jax.nn.sigmoid(ga_ref[...]) * a + jax.nn.sigmoid(gb_ref[...]) * b
    y = jnp.dot(merged.astype(BF16), wo_ref[...], preferred_element_type=F32)
    x = x_ref[...]
    r = alpha * x + g1_ref[...] * y.reshape(x.shape)
    o_ref[...] = _layer_norm(r, lng_ref[...], lnb_ref[...])


def _row_tiling(x3, mod_group0, tm, n_grid_axes):
    g, r, d = x3.shape
    if r >= tm:
        tps = r // tm
        gb = 1
        if n_grid_axes == 1:
            x_map = lambda i: (i // tps, i % tps, 0)
            mod_idx = lambda k: (lambda i: (mod_group0 + i // tps, 0, k))
        else:
            x_map = lambda i, j: (i // tps, i % tps, 0)
            mod_idx = lambda k: (lambda i, j: (mod_group0 + i // tps, 0, k))
        x_spec = pl.BlockSpec((1, tm, d), x_map)
    else:
        tps = 1
        gb = tm // r
        if n_grid_axes == 1:
            x_map = lambda i: (i, 0, 0)
            mod_idx = lambda k: (lambda i: (mod_group0 // gb + i, 0, k))
        else:
            x_map = lambda i, j: (i, 0, 0)
            mod_idx = lambda k: (lambda i, j: (mod_group0 // gb + i, 0, k))
        x_spec = pl.BlockSpec((gb, r, d), x_map)
    mod_spec = lambda k: pl.BlockSpec((gb, 1, d), mod_idx(k))
    return x_spec, mod_spec, tps, gb


def _merge(oa2, ob2, gates, x3, mod3, mod_group0, wpa, wpb, wo, lng, lnb, *, tm, alpha):
    g, r, d = x3.shape
    m = g * r
    wa = oa2.shape[1]
    x_spec, mod_spec, _, _ = _row_tiling(x3, mod_group0, tm, 1)
    const = lambda i: (0, 0)
    return pl.pallas_call(
        functools.partial(_merge_kernel, alpha=alpha),
        grid=(m // tm,),
        in_specs=[pl.BlockSpec((tm, wa), lambda i: (i, 0)),
                  pl.BlockSpec((tm, wa), lambda i: (i, 0)),
                  pl.BlockSpec((tm, d), lambda i: (i, _GATE_GA // 2)),
                  pl.BlockSpec((tm, d), lambda i: (i, _GATE_GB // 2)),
                  x_spec,
                  mod_spec(2),
                  pl.BlockSpec(wpa.shape, const, pipeline_mode=pl.Buffered(1)),
                  pl.BlockSpec(wpb.shape, const, pipeline_mode=pl.Buffered(1)),
                  pl.BlockSpec(wo.shape, const, pipeline_mode=pl.Buffered(1)),
                  pl.BlockSpec((1, d), const),
                  pl.BlockSpec((1, d), const)],
        out_specs=x_spec,
        out_shape=jax.ShapeDtypeStruct(x3.shape, F32),
        compiler_params=_params("arbitrary"),
        name="merge_outproj",
    )(oa2, ob2, gates, gates, x3, mod3, wpa, wpb, wo, lng, lnb)


def _ffn_kernel(*refs, tm, tiles_per_seq, alpha, with_state, conv_w):
    if with_state:
        (x_hbm, sc_ref, sh_ref, g2_ref, wa_ref, wb_ref, wd_ref, cw_ref, cb_ref, lng_ref, lnb_ref,
         st_ref, o_ref, tail_ref, u_sc, xbuf, xsem, abuf) = refs
    else:
        (x_hbm, sc_ref, sh_ref, g2_ref, wa_ref, wb_ref, wd_ref, cw_ref, cb_ref, lng_ref, lnb_ref,
         o_ref, tail_ref, u_sc, xbuf, xsem, abuf, carry_sc) = refs
    i = pl.program_id(0)
    j = pl.program_id(1)
    hist = conv_w - 1
    x_copy = functools.partial(_x_tile_copy, x_hbm, xbuf, xsem, tm=tm,
                               tiles_per_seq=tiles_per_seq)

    @pl.when(j == 0)
    def _():
        @pl.when(i == 0)
        def _():
            x_copy(0).start()

        x_copy(i).wait()
        x = xbuf[...]
        u_sc[...] = (x * (1.0 + sc_ref[...]) + sh_ref[...]).reshape(tm, x.shape[-1]).astype(BF16)
        o_ref[...] = alpha * x

        @pl.when(i + 1 < pl.num_programs(0))
        def _():
            x_copy(i + 1).start()

    u = u_sc[...]
    a = jnp.dot(u, wa_ref[...], preferred_element_type=F32)
    b = jnp.dot(u, wb_ref[...], preferred_element_type=F32)
    tf = a.shape[-1]
    cw = cw_ref[...]
    if with_state:
        gb = tm // SUBLANES
        a3 = a.reshape(gb, SUBLANES, tf)
        abuf[:, SUBLANES:2 * SUBLANES, :] = a3
        abuf[:, SUBLANES - hist:SUBLANES, :] = st_ref[...]
        shifted = [abuf[:, SUBLANES - hist + w:2 * SUBLANES - hist + w, :].reshape(tm, tf)
                   for w in range(hist)]
        tail_ref[...] = a3[:, SUBLANES - hist:, :]
    else:
        first = (i % tiles_per_seq) == 0
        abuf[0:SUBLANES, :] = jnp.where(first, 0.0, carry_sc[j])
        abuf[SUBLANES:SUBLANES + tm, :] = a
        shifted = [abuf[SUBLANES - hist + w:SUBLANES - hist + w + tm, :] for w in range(hist)]
        carry_sc[j] = a[tm - SUBLANES:, :]
        tail_ref[0] = a[tm - hist:, :]
    conv = a * cw[hist:hist + 1, :]
    for w in range(hist):
        conv = conv + shifted[w] * cw[w:w + 1, :]
    conv = conv + cb_ref[...]
    hg = (jax.nn.gelu(conv) * b).astype(BF16)
    y = jnp.dot(hg, wd_ref[...], preferred_element_type=F32).reshape(o_ref.shape)
    o_ref[...] += g2_ref[...] * y

    @pl.when(j == pl.num_programs(1) - 1)
    def _():
        o_ref[...] = _layer_norm(o_ref[...], lng_ref[...], lnb_ref[...])


def _ffn(x3, mod3, mod_group0, wup, wdown, conv_w, conv_b, lng, lnb, conv_state,
         *, tm, tf, alpha):
    g, r, d = x3.shape
    m = g * r
    dff = wdown.shape[0]
    nj = dff // tf
    cwid = conv_w.shape[0]
    hist = cwid - 1
    x_spec, mod_spec, tps, gb = _row_tiling(x3, mod_group0, tm, 2)
    with_state = conv_state is not None
    const = lambda i, j: (0, 0)
    in_specs = [pl.BlockSpec(memory_space=pl.ANY),
                mod_spec(4), mod_spec(3), mod_spec(5),
                pl.BlockSpec((d, tf), lambda i, j: (0, j)),
                pl.BlockSpec((d, tf), lambda i, j: (0, nj + j)),
                pl.BlockSpec((tf, d), lambda i, j: (j, 0)),
                pl.BlockSpec((cwid, tf), lambda i, j: (0, j)),
                pl.BlockSpec((1, tf), lambda i, j: (0, j)),
                pl.BlockSpec((1, d), const),
                pl.BlockSpec((1, d), const)]
    args = [x3, mod3, mod3, mod3, wup, wup, wdown, conv_w, conv_b.reshape(1, dff), lng, lnb]
    scratch = [pltpu.VMEM((tm, d), BF16), pltpu.VMEM(x_spec.block_shape, F32),
               pltpu.SemaphoreType.DMA(())]
    if with_state:
        assert r == SUBLANES
        in_specs.append(pl.BlockSpec((gb, hist, tf), lambda i, j: (i, 0, j)))
        args.append(conv_state)
        tail_spec = pl.BlockSpec((gb, hist, tf), lambda i, j: (i, 0, j))
        scratch.append(pltpu.VMEM((gb, 2 * SUBLANES, tf), F32))
    else:
        tail_spec = pl.BlockSpec((1, hist, tf), lambda i, j: (i, 0, j))
        scratch += [pltpu.VMEM((tm + SUBLANES, tf), F32), pltpu.VMEM((nj, SUBLANES, tf), F32)]
    n_tails = g if with_state else m // tm
    kern = functools.partial(_ffn_kernel, tm=tm, tiles_per_seq=tps, alpha=alpha,
                             with_state=with_state, conv_w=cwid)
    y, tails = pl.pallas_call(
        kern,
        grid=(m // tm, nj),
        in_specs=in_specs,
        out_specs=[x_spec, tail_spec],
        out_shape=[jax.ShapeDtypeStruct(x3.shape, F32),
                   jax.ShapeDtypeStruct((n_tails, hist, dff), F32)],
        scratch_shapes=scratch,
        compiler_params=_params("arbitrary", "arbitrary"),
        name="convffn",
    )(*args)
    if not with_state:
        tails = tails.reshape(g, tps, hist, dff)[:, tps - 1]
    return y, tails


def _rope_tables(pos, half):
    inv = ROPE_BASE ** (-jnp.arange(half, dtype=F32) / half)
    ang = pos.astype(F32)[:, None] * inv[None, :]
    cos, sin = jnp.cos(ang), jnp.sin(ang)
    return jnp.concatenate([cos, cos], axis=-1), jnp.concatenate([-sin, sin], axis=-1)


def kernel(x_prompt, x_sample, cache_k, cache_v, cache_logf, state_ret, state_conv, page_table,
           c_prompt, c_sample, w_ada, b_ada, w_in, b_f, w_pa, w_pb, w_o, ln1_g, ln1_b,
           w_up, conv_w, conv_b, w_down, ln2_g, ln2_b):
    depth = w_ada.shape[0]
    b, t, d = x_prompt.shape
    bs, ts, _ = x_sample.shape
    n_pool, page_size, n_heads, dh = cache_k.shape[1:]
    n_pages = page_table.shape[1]
    past = n_pages * page_size
    wa = n_heads * dh
    dff = w_down.shape[1]
    alpha = (2.0 * depth) ** 0.25
    assert wa == 1024 and dh == LANES and ts == SUBLANES and n_pages % SUBLANES == 0
    assert w_in.shape[2] == 7 * wa + n_heads + 2 * d and d == 2 * wa

    tm_p, tm_s = 1024, bs * ts
    tq = 512
    ret_chunk = 128 if t % 128 == 0 else t

    cos_p, sin_p = _rope_tables(jnp.arange(t), dh // 2)
    cos_s, sin_s = _rope_tables(past + jnp.arange(ts), dh // 2)
    cos_s, sin_s = jnp.tile(cos_s, (tm_s // ts, 1)), jnp.tile(sin_s, (tm_s // ts, 1))

    n_mod = -(-(bs + b) // 16) * 16
    c_all = jnp.concatenate([c_sample, c_prompt, jnp.zeros((n_mod - bs - b, d), F32)], axis=0)

    w_in_t = jnp.swapaxes(w_in, 1, 2)

    hp, hs = x_prompt, x_sample
    outs = {k: [] for k in ("kp", "vp", "lp", "rp", "cp", "ks", "vs", "ls", "rs", "cs")}
    for l in range(depth):
        mod3 = _ada(c_all, w_ada[l], b_ada[l]).reshape(n_mod, 1, 6 * d)

        fa0 = 3 * wa
        w_main = _repack_w_in(w_in_t, l, fa0, n_heads)
        wf_t = jnp.zeros((16, d), F32).at[:n_heads].set(w_in_t[l, fa0:fa0 + n_heads, :])
        bf_col = jnp.zeros((16, 1), F32).at[:n_heads, 0].set(b_f[l].astype(F32))
        wpa, wpb, wo = w_pa[l].astype(BF16), w_pb[l].astype(BF16), w_o[l].astype(BF16)
        wup = w_up[l].astype(BF16)
        wdown = w_down[l].astype(BF16)
        lng1, lnb1 = ln1_g[l].reshape(1, d), ln1_b[l].reshape(1, d)
        lng2, lnb2 = ln2_g[l].reshape(1, d), ln2_b[l].reshape(1, d)

        pbf, pk, pv, pg, lf_t, c_t = _inproj(hp, mod3, bs, w_main, wf_t, bf_col, cos_p, sin_p,
                                             tm=tm_p, seq_tiled=True)
        pbf3 = pbf.reshape(b, t, -1)
        nt = t // tm_p
        c4 = (c_t.reshape(b, nt, 16, tm_p // tq, tq).transpose(0, 1, 3, 2, 4)
              .reshape(b, t // tq, 16, tq))
        sbf, sk, sv, sg, lfs_t, _ = _inproj(hs, mod3, 0, w_main, wf_t, bf_col, cos_s, sin_s,
                                            tm=tm_s, seq_tiled=False)
        rows_q = ts * n_heads
        q3 = sbf[:, _BF_QA * wa:(_BF_QA + 1) * wa].reshape(bs, rows_q, dh)
        k_new = sk.reshape(bs, ts, n_heads, dh)
        v_new = sv.reshape(bs, ts, n_heads, dh)
        lf_s = lfs_t[0, :n_heads, :].T.reshape(bs, ts, n_heads)
        lfn3 = jnp.pad(lf_s.reshape(bs, 1, rows_q), ((0, 0), (0, 0), (0, LANES - rows_q)))
        ck3 = cache_k[l].reshape(n_pool, page_size * n_heads, dh)
        cv3 = cache_v[l].reshape(n_pool, page_size * n_heads, dh)
        clf3 = cache_logf[l].astype(F32).reshape(n_pool, 1, page_size * n_heads)
        oa, oa_s = _fox_attention(pbf3, pk.reshape(b, t, wa), pv.reshape(b, t, wa), c4,
                                  page_table, q3, k_new.reshape(bs, rows_q, dh),
                                  v_new.reshape(bs, rows_q, dh), lfn3, ck3, cv3, clf3,
                                  n_heads=n_heads, tq=tq, hpb=2, n_new=ts)

        ob, ret_p = _retention(pbf3, pg.reshape(b, t, -1), None, n_heads=n_heads,
                               chunk=ret_chunk, bpb=4)
        x1 = _merge(oa.reshape(b * t, wa), ob.reshape(b * t, wa), pg, hp, mod3, bs,
                    wpa, wpb, wo, lng1, lnb1, tm=256, alpha=alpha)
        hp, conv_p = _ffn(x1, mod3, bs, wup, wdown, conv_w[l], conv_b[l], lng2, lnb2,
                          None, tm=1024, tf=512, alpha=alpha)
        outs["kp"].append(pk.reshape(b, t, n_heads, dh))
        outs["vp"].append(pv.reshape(b, t, n_heads, dh))
        lf_p = lf_t[:, :n_heads, :].reshape(b, nt, n_heads, tm_p)
        outs["lp"].append(lf_p.transpose(0, 1, 3, 2).reshape(b, t, n_heads))
        outs["rp"].append(ret_p)
        outs["cp"].append(conv_p)

        sbf3 = sbf.reshape(bs, ts, -1)
        ob_s, ret_s = _retention(sbf3, sg.reshape(bs, ts, -1), state_ret[l], n_heads=n_heads,
                                 chunk=ts, bpb=8)
        x1s = _merge(oa_s.reshape(bs * ts, wa), ob_s.reshape(bs * ts, wa), sg, hs, mod3, 0,
                     wpa, wpb, wo, lng1, lnb1, tm=256, alpha=alpha)
        hs, conv_s = _ffn(x1s, mod3, 0, wup, wdown, conv_w[l], conv_b[l], lng2, lnb2,
                          state_conv[l], tm=1024, tf=512, alpha=alpha)
        outs["ks"].append(k_new)
        outs["vs"].append(v_new)
        outs["ls"].append(lf_s)
        outs["rs"].append(ret_s)
        outs["cs"].append(conv_s)

    st = lambda name: jnp.stack(outs[name])
    return (hp, hs, st("kp"), st("vp"), st("lp"), st("rp"), st("cp"),
            st("ks"), st("vs"), st("ls"), st("rs"), st("cs"))
```

```python
import functools
import math

import jax
import jax.numpy as jnp
from jax import lax
from jax.experimental import pallas as pl
from jax.experimental.pallas import tpu as pltpu

F32 = jnp.float32
BF16 = jnp.bfloat16

LANES = 128
SUBLANES = 8
VMEM_LIMIT = 56 * 1024 * 1024

ROPE_BASE = 10000.0
LN_EPS = 1e-5
GN_EPS = 1e-5
NEG_BIG = -1e30
LOG2E = math.log2(math.e)

_NT = (((1,), (1,)), ((), ()))
_TN = (((0,), (0,)), ((), ()))


def _params(*sem):
    return pltpu.CompilerParams(dimension_semantics=sem, vmem_limit_bytes=VMEM_LIMIT)


def _layer_norm(r, g, b):
    mu = jnp.mean(r, axis=-1, keepdims=True)
    d = r - mu
    var = jnp.mean(d * d, axis=-1, keepdims=True)
    return d * lax.rsqrt(var + LN_EPS) * g + b


def _ada_kernel(c_ref, w_ref, b_ref, o_ref):
    c = c_ref[...]
    s = (c * jax.nn.sigmoid(c)).astype(BF16)
    o_ref[...] = jnp.dot(s, w_ref[...].astype(BF16), preferred_element_type=F32) + b_ref[...]


def _ada(c_all, w_ada, b_ada):
    rows, d = c_all.shape
    n = w_ada.shape[1]
    tn = 512
    return pl.pallas_call(
        _ada_kernel,
        grid=(n // tn,),
        in_specs=[pl.BlockSpec((rows, d), lambda j: (0, 0)),
                  pl.BlockSpec((d, tn), lambda j: (0, j)),
                  pl.BlockSpec((1, tn), lambda j: (0, j))],
        out_specs=pl.BlockSpec((rows, tn), lambda j: (0, j)),
        out_shape=jax.ShapeDtypeStruct((rows, n), F32),
        compiler_params=_params("arbitrary"),
        name="ada_mod",
    )(c_all, w_ada, b_ada.reshape(1, n))


(_G_QA, _G_KA, _G_VA, _G_QR, _G_KR, _G_VR, _G_GA0, _G_GA1, _G_GB0, _G_GB1, _G_GR) = range(11)
_N_GROUPS = 11
_W_BLOCK_OF_GR = 6
_BF_QA, _BF_QR, _BF_KR, _BF_VR = 0, 1, 2, 3
_GATE_GA, _GATE_GB, _GATE_GR = 0, 2, 4


def _w_block(j):
    return jnp.where(j < _G_GA0, j, jnp.where(j < _G_GR, j + 1, _W_BLOCK_OF_GR))


def _bf_block(j):
    return jnp.clip(j - (_G_QR - _BF_QR), _BF_QA, _BF_VR)


def _gate_block(j):
    return jnp.clip(j - _G_GA0, _GATE_GA, _GATE_GR)


def _repack_kernel(wt_ref, o_ref):
    o_ref[...] = wt_ref[0].T.astype(BF16)


def _repack_w_in(w_in_t, layer, fa0, n_f):
    _, n, d = w_in_t.shape
    tc = 1024
    assert fa0 % tc == 0 and (n - n_f) % tc == 0

    def rows(j):
        return pl.multiple_of(jnp.where(j < fa0 // tc, j * tc, j * tc + n_f), SUBLANES)

    return pl.pallas_call(
        _repack_kernel,
        grid=((n - n_f) // tc,),
        in_specs=[pl.BlockSpec((pl.Element(1), pl.Element(tc), pl.Element(d)),
                               lambda j: (layer, rows(j), 0))],
        out_specs=pl.BlockSpec((d, tc), lambda j: (0, j)),
        out_shape=jax.ShapeDtypeStruct((d, n - n_f), BF16),
        compiler_params=_params("arbitrary"),
        name="repack_w_in",
    )(w_in_t)


def _lane_cumsum(x, carry):
    rows, width = x.shape
    lane = lax.broadcasted_iota(jnp.int32, (rows, LANES), 1)
    out = []
    for c in range(width // LANES):
        v = x[:, c * LANES:(c + 1) * LANES]
        s = 1
        while s < LANES:
            v = v + jnp.where(lane >= s, pltpu.roll(v, s, axis=1), 0.0)
            s *= 2
        out.append(v + carry)
        carry = carry + jnp.broadcast_to(v[:, LANES - 1:LANES], (rows, LANES))
    return jnp.concatenate(out, axis=1), carry


def _x_tile_copy(x_hbm, xbuf, sem, tile, *, tm, tiles_per_seq):
    if tiles_per_seq > 1:
        src = x_hbm.at[pl.ds(tile // tiles_per_seq, 1), pl.ds((tile % tiles_per_seq) * tm, tm)]
    else:
        gb = xbuf.shape[0]
        src = x_hbm.at[pl.ds(tile * gb, gb)]
    return pltpu.make_async_copy(src, xbuf, sem)


def _inproj_kernel(x_hbm, sc_ref, sh_ref, w_ref, wf_ref, bf_ref, cos_ref, sin_ref,
                   obf_ref, ok_ref, ov_ref, og_ref, lf_ref, c_ref, u_sc, carry_sc, xbuf, xsem,
                   *, tm, tiles_per_seq, qa_scale, kr_scale, n_heads):
    i = pl.program_id(0)
    j = pl.program_id(1)
    x_copy = functools.partial(_x_tile_copy, x_hbm, xbuf, xsem, tm=tm,
                               tiles_per_seq=tiles_per_seq)

    @pl.when(j == 0)
    def _():
        @pl.when(i == 0)
        def _():
            x_copy(0).start()

        x_copy(i).wait()
        x = xbuf[...]
        u = (x * (1.0 + sc_ref[...]) + sh_ref[...]).reshape(tm, x.shape[-1]).astype(BF16)
        u_sc[...] = u

        @pl.when(i + 1 < pl.num_programs(0))
        def _():
            x_copy(i + 1).start()

        z = lax.dot_general(wf_ref[...].astype(BF16), u, _NT,
                            preferred_element_type=F32) + bf_ref[...]
        lf = jnp.minimum(z, 0.0) - jnp.log1p(jnp.exp(-jnp.abs(z)))
        lf_ref[0] = lf
        first = (i % tiles_per_seq) == 0
        prev = jnp.where(first, 0.0, carry_sc[...])
        c, last = _lane_cumsum(lf, prev)
        c_ref[0] = c * LOG2E
        carry_sc[...] = last

    def project():
        return jnp.dot(u_sc[...], w_ref[...], preferred_element_type=F32)

    @pl.when(j == _G_QA)
    def _():
        obf_ref[...] = (project() * qa_scale).astype(BF16)

    def rope(scale):
        acc = project()
        cos = cos_ref[...]
        sin = sin_ref[...]
        for h in range(n_heads):
            sl = slice(h * LANES, (h + 1) * LANES)
            a = acc[:, sl]
            r = a * cos + pltpu.roll(a, LANES // 2, axis=1) * sin
            if scale is not None:
                r = r * scale
            obf_ref[:, sl] = r.astype(BF16)

    @pl.when(j == _G_QR)
    def _():
        rope(None)

    @pl.when(j == _G_KR)
    def _():
        rope(kr_scale)

    @pl.when(j == _G_VR)
    def _():
        obf_ref[...] = project().astype(BF16)

    @pl.when(j == _G_KA)
    def _():
        ok_ref[...] = project()

    @pl.when(j == _G_VA)
    def _():
        ov_ref[...] = project()

    @pl.when(j >= _G_GA0)
    def _():
        og_ref[...] = project()


def _inproj(x3, mod3, mod_group0, w_main, wf_t, bf_col, cos_t, sin_t, *, tm, seq_tiled):
    g, r, d = x3.shape
    m = g * r
    n_tiles = m // tm
    if seq_tiled:
        tps = r // tm
        gb = 1
        x_block = (1, tm, d)
        mod_idx = lambda k: (lambda i, j: (mod_group0 + i // tps, 0, k))
        tab_spec = pl.BlockSpec((tm, LANES), lambda i, j: (i % tps, 0))
    else:
        tps = 1
        gb = tm // r
        x_block = (gb, r, d)
        mod_idx = lambda k: (lambda i, j: (mod_group0 // gb + i, 0, k))
        tab_spec = pl.BlockSpec((tm, LANES), lambda i, j: (0, 0))
    n_bf = _BF_VR + 1
    kern = functools.partial(_inproj_kernel, tm=tm, tiles_per_seq=tps,
                             qa_scale=LANES ** -0.5 * LOG2E, kr_scale=LANES ** -0.5,
                             n_heads=1024 // LANES)
    return pl.pallas_call(
        kern,
        grid=(n_tiles, _N_GROUPS),
        in_specs=[pl.BlockSpec(memory_space=pl.ANY),
                  pl.BlockSpec((gb, 1, d), mod_idx(1)),
                  pl.BlockSpec((gb, 1, d), mod_idx(0)),
                  pl.BlockSpec((d, 1024), lambda i, j: (0, _w_block(j))),
                  pl.BlockSpec((16, d), lambda i, j: (0, 0)),
                  pl.BlockSpec((16, 1), lambda i, j: (0, 0)),
                  tab_spec, tab_spec],
        out_specs=[pl.BlockSpec((tm, 1024), lambda i, j: (i, _bf_block(j))),
                   pl.BlockSpec((tm, 1024), lambda i, j: (i, 0)),
                   pl.BlockSpec((tm, 1024), lambda i, j: (i, 0)),
                   pl.BlockSpec((tm, 1024), lambda i, j: (i, _gate_block(j))),
                   pl.BlockSpec((1, 16, tm), lambda i, j: (i, 0, 0)),
                   pl.BlockSpec((1, 16, tm), lambda i, j: (i, 0, 0))],
        out_shape=[jax.ShapeDtypeStruct((m, 1024 * n_bf), BF16),
                   jax.ShapeDtypeStruct((m, 1024), F32),
                   jax.ShapeDtypeStruct((m, 1024), F32),
                   jax.ShapeDtypeStruct((m, 1024 * (_GATE_GR + 1)), F32),
                   jax.ShapeDtypeStruct((n_tiles, 16, tm), F32),
                   jax.ShapeDtypeStruct((n_tiles, 16, tm), F32)],
        scratch_shapes=[pltpu.VMEM((tm, d), BF16), pltpu.VMEM((16, LANES), F32),
                        pltpu.VMEM(x_block, F32), pltpu.SemaphoreType.DMA(())],
        compiler_params=_params("arbitrary", "arbitrary"),
        name="inproj",
    )(x3, mod3, mod3, w_main, wf_t, bf_col, cos_t, sin_t)


def _fox_prefill_body(q_ref, k_ref, v_ref, c_ref, o_ref, hg, qi, *, tq, hpb):
    dh = LANES

    def chunk(kc, carry, diagonal):
        start = pl.multiple_of(kc * tq, tq)
        out = []
        for hh in range(hpb):
            m, l, acc = carry[hh]
            sl = slice(hh * dh, (hh + 1) * dh)
            kk = k_ref[0, pl.ds(start, tq), sl].astype(BF16)
            vv = v_ref[0, pl.ds(start, tq), sl].astype(BF16)
            s = lax.dot_general(q_ref[0, :, sl], kk, _NT, preferred_element_type=F32)
            s = s - c_ref[0, kc, pl.ds(hg * hpb + hh, 1), :]
            if diagonal:
                row = lax.broadcasted_iota(jnp.int32, (tq, tq), 0)
                col = lax.broadcasted_iota(jnp.int32, (tq, tq), 1)
                s = jnp.where(row >= col, s, NEG_BIG)
            m_new = jnp.maximum(m, jnp.max(s, axis=-1, keepdims=True))
            alpha = jnp.exp2(m - m_new)
            p = jnp.exp2(s - m_new)
            l = alpha * l + jnp.sum(p, axis=-1, keepdims=True)
            acc = alpha * acc + jnp.dot(p.astype(BF16), vv, preferred_element_type=F32)
            out.append((m_new, l, acc))
        return tuple(out)

    init = tuple((jnp.full((tq, 1), NEG_BIG, F32), jnp.zeros((tq, 1), F32),
                  jnp.zeros((tq, dh), F32)) for _ in range(hpb))
    carry = lax.fori_loop(0, qi, lambda kc, c: chunk(kc, c, False), init)
    carry = chunk(qi, carry, True)
    for hh in range(hpb):
        _, l, acc = carry[hh]
        o_ref[0, :, hh * dh:(hh + 1) * dh] = (acc / l).astype(BF16)


def _periodic_tail(v, lane, n_heads):
    y = jnp.where(lane >= LANES - n_heads, v, 0.0)
    s = n_heads
    while s < LANES:
        y = y + pltpu.roll(y, LANES - s, axis=1)
        s *= 2
    return y


def _page_copies(pt_ref, ck_hbm, cv_hbm, clf_hbm, kbuf, vbuf, lfbuf, sems, seq, slot, n_pages):
    copies = []
    for p in range(n_pages):
        page = pt_ref[seq * n_pages + p]
        copies.append(pltpu.make_async_copy(ck_hbm.at[page], kbuf.at[slot, p], sems.at[slot, 0]))
        copies.append(pltpu.make_async_copy(cv_hbm.at[page], vbuf.at[slot, p], sems.at[slot, 1]))
        copies.append(pltpu.make_async_copy(clf_hbm.at[page], lfbuf.at[slot, p], sems.at[slot, 2]))
    return copies


def _fox_decode_issue(b, copies):
    @pl.when(b == 0)
    def _():
        for cp in copies(seq=0, slot=0):
            cp.start()

    @pl.when(b + 1 < pl.num_programs(0))
    def _():
        for cp in copies(seq=b + 1, slot=1 - b % 2):
            cp.start()


def _fox_decode_attend(b, copies, q_ref, kn_ref, vn_ref, lfn_ref, o_ref, kbuf, vbuf, lfbuf,
                       lf_sc, s_sc, *, n_pages, n_heads, n_new):
    slot = b % 2
    for cp in copies(seq=b, slot=slot):
        cp.wait()

    k_refs = [kbuf.at[slot, p] for p in range(n_pages)]
    v_refs = [vbuf.at[slot, p] for p in range(n_pages)]
    lf_refs = [lfbuf.at[slot, p] for p in range(n_pages)]

    page_w = lfbuf.shape[-1]
    ppr = n_pages // SUBLANES
    n_chunks = ppr * page_w // LANES
    rows_q = n_new * n_heads

    for p in range(n_pages):
        r, part = divmod(p, ppr)
        lf_sc[r:r + 1, part * page_w:(part + 1) * page_w] = lf_refs[p][...] * LOG2E
    lane = lax.broadcasted_iota(jnp.int32, (SUBLANES, LANES), 1)
    sub = lax.broadcasted_iota(jnp.int32, (SUBLANES, LANES), 0)
    chunks = []
    carry = jnp.zeros((SUBLANES, LANES), F32)
    for c in range(n_chunks):
        v = lf_sc[:, c * LANES:(c + 1) * LANES]
        s = n_heads
        while s < LANES:
            v = v + jnp.where(lane >= s, pltpu.roll(v, s, axis=1), 0.0)
            s *= 2
        chunks.append(v + carry)
        carry = carry + _periodic_tail(v, lane, n_heads)
    inc = carry
    s = 1
    while s < SUBLANES:
        inc = inc + jnp.where(sub >= s, pltpu.roll(inc, s, axis=0), 0.0)
        s *= 2
    exc = jnp.where(sub >= 1, pltpu.roll(inc, 1, axis=0), 0.0)
    chunks = [v + exc for v in chunks]
    past_total = inc[SUBLANES - 1:SUBLANES, :]

    q = q_ref[0]
    dh = q.shape[-1]
    row_h = lax.broadcasted_iota(jnp.int32, (rows_q, page_w), 0) % n_heads
    col_h = lax.broadcasted_iota(jnp.int32, (rows_q, page_w), 1) % n_heads
    same_head = row_h == col_h

    cpp = page_w // LANES
    mx = jnp.full((rows_q, LANES), NEG_BIG, F32)
    for p in range(n_pages):
        r, part = divmod(p, ppr)
        ck = jnp.concatenate([chunks[part * cpp + c][r:r + 1, :] for c in range(cpp)], axis=1)
        kk = k_refs[p][...].astype(BF16)
        s = lax.dot_general(q, kk, _NT, preferred_element_type=F32)
        s = jnp.where(same_head, s - ck, NEG_BIG)
        s_sc[p] = s
        for c in range(cpp):
            mx = jnp.maximum(mx, s[:, c * LANES:(c + 1) * LANES])

    lane1 = lax.broadcasted_iota(jnp.int32, (1, LANES), 1)
    cn = lfn_ref[0] * LOG2E
    s = n_heads
    while s < rows_q:
        cn = cn + jnp.where(lane1 >= s, pltpu.roll(cn, s, axis=1), 0.0)
        s *= 2
    cn = cn + past_total
    s_new = lax.dot_general(q, kn_ref[0].astype(BF16), _NT, preferred_element_type=F32)
    row = lax.broadcasted_iota(jnp.int32, (rows_q, rows_q), 0)
    col = lax.broadcasted_iota(jnp.int32, (rows_q, rows_q), 1)
    ok = ((row % n_heads) == (col % n_heads)) & (col <= row)
    s_new = jnp.where(ok, s_new - cn[:, :rows_q], NEG_BIG)
    m = jnp.maximum(jnp.max(mx, axis=-1, keepdims=True), jnp.max(s_new, axis=-1, keepdims=True))

    lsum = jnp.zeros((rows_q, LANES), F32)
    acc = jnp.zeros((rows_q, dh), F32)
    for p in range(n_pages):
        pr = jnp.exp2(s_sc[p] - m)
        for c in range(cpp):
            lsum = lsum + pr[:, c * LANES:(c + 1) * LANES]
        acc = acc + jnp.dot(pr.astype(BF16), v_refs[p][...].astype(BF16),
                            preferred_element_type=F32)
    pr = jnp.exp2(s_new - m)
    l = jnp.sum(lsum, axis=-1, keepdims=True) + jnp.sum(pr, axis=-1, keepdims=True)
    acc = acc + jnp.dot(pr.astype(BF16), vn_ref[0].astype(BF16), preferred_element_type=F32)
    o_ref[0] = (acc / l).astype(BF16)


def _fox_kernel(pt_ref, qp_ref, kp_ref, vp_ref, cp_ref, q_ref, kn_ref, vn_ref, lfn_ref,
                ck_hbm, cv_hbm, clf_hbm, op_ref, o_ref, kbuf, vbuf, lfbuf, sems, lf_sc, s_sc,
                *, tq, hpb, n_hg, n_q, n_pages, n_heads, n_new):
    step = pl.program_id(0)
    copies = functools.partial(_page_copies, pt_ref, ck_hbm, cv_hbm, clf_hbm, kbuf, vbuf, lfbuf,
                               sems, n_pages=n_pages)
    _fox_decode_issue(step, copies)
    _fox_prefill_body(qp_ref, kp_ref, vp_ref, cp_ref, op_ref, (step // n_q) % n_hg, step % n_q,
                      tq=tq, hpb=hpb)
    _fox_decode_attend(step, copies, q_ref, kn_ref, vn_ref, lfn_ref, o_ref, kbuf, vbuf, lfbuf,
                       lf_sc, s_sc, n_pages=n_pages, n_heads=n_heads, n_new=n_new)


def _fox_attention(pbf3, k3, v3, c4, page_table, q3, kn3, vn3, lfn3, ck3, cv3, clf3,
                   *, n_heads, tq, hpb, n_new):
    b, t, _ = pbf3.shape
    w = hpb * LANES
    n_hg = n_heads // hpb
    n_q = t // tq
    bs, n_pages = page_table.shape
    rows_q, dh = q3.shape[1], q3.shape[2]
    page_rows = ck3.shape[1]
    page_w = clf3.shape[2]
    assert b * n_hg * n_q == bs, "prefill steps and decode sequences must pair up"

    def pre(col0):
        return lambda s, pt: (s // (n_hg * n_q), s % n_q, col0 + (s // n_q) % n_hg)

    def pre_kv(s, pt):
        return (s // (n_hg * n_q), 0, (s // n_q) % n_hg)

    def same(s, pt):
        return (s, 0, 0)

    hbm = pl.BlockSpec(memory_space=pl.ANY)
    grid_spec = pltpu.PrefetchScalarGridSpec(
        num_scalar_prefetch=1,
        grid=(bs,),
        in_specs=[pl.BlockSpec((1, tq, w), pre(_BF_QA * n_hg)),
                  pl.BlockSpec((1, t, w), pre_kv),
                  pl.BlockSpec((1, t, w), pre_kv),
                  pl.BlockSpec((1, n_q, 16, tq), lambda s, pt: (s // (n_hg * n_q), 0, 0, 0)),
                  pl.BlockSpec((1, rows_q, dh), same),
                  pl.BlockSpec((1, rows_q, dh), same),
                  pl.BlockSpec((1, rows_q, dh), same),
                  pl.BlockSpec((1, 1, LANES), same),
                  hbm, hbm, hbm],
        out_specs=[pl.BlockSpec((1, tq, w), pre(0)),
                   pl.BlockSpec((1, rows_q, dh), same)],
        scratch_shapes=[pltpu.VMEM((2, n_pages, page_rows, dh), F32),
                        pltpu.VMEM((2, n_pages, page_rows, dh), F32),
                        pltpu.VMEM((2, n_pages, 1, page_w), F32),
                        pltpu.SemaphoreType.DMA((2, 3)),
                        pltpu.VMEM((SUBLANES, n_pages // SUBLANES * page_w), F32),
                        pltpu.VMEM((n_pages, rows_q, page_w), F32)],
    )
    return pl.pallas_call(
        functools.partial(_fox_kernel, tq=tq, hpb=hpb, n_hg=n_hg, n_q=n_q, n_pages=n_pages,
                          n_heads=n_heads, n_new=n_new),
        grid_spec=grid_spec,
        out_shape=[jax.ShapeDtypeStruct((b, t, n_heads * LANES), BF16),
                   jax.ShapeDtypeStruct((bs, rows_q, dh), BF16)],
        compiler_params=_params("arbitrary"),
        name="fox_attention",
    )(page_table.reshape(-1), pbf3, k3, v3, c4, q3, kn3, vn3, lfn3, ck3, cv3, clf3)


def _ret_kernel(*refs, n_heads, has_state, bpb, n_cast):
    n_in = 8 + (1 if has_state else 0)
    cast_in = refs[n_in:n_in + n_cast]
    cast_out = refs[n_in + n_cast + 2:n_in + 2 * n_cast + 2]
    refs = refs[:n_in] + refs[n_in + n_cast:n_in + n_cast + 2] + refs[n_in + 2 * n_cast + 2:]
    for src, dst in zip(cast_in, cast_out):
        dst[...] = src[0].astype(BF16)
    if has_state:
        (q_ref, k_ref, v_ref, g_ref, dmat_ref, qdec_ref, kdec_ref, sdec_ref, s0_ref,
         o_ref, sout_ref, s_sc) = refs
    else:
        (q_ref, k_ref, v_ref, g_ref, dmat_ref, qdec_ref, kdec_ref, sdec_ref,
         o_ref, sout_ref, s_sc) = refs
    c = pl.program_id(1)

    @pl.when(c == 0)
    def _():
        if has_state:
            s_sc[...] = s0_ref[...]
        else:
            s_sc[...] = jnp.zeros_like(s_sc)

    for bb in range(bpb):
        for h in range(n_heads):
            sl = slice(h * LANES, (h + 1) * LANES)
            q = q_ref[bb, :, sl]
            k = k_ref[bb, :, sl]
            v = v_ref[bb, :, sl]
            st = s_sc[bb, h]
            att = lax.dot_general(q, k, _NT, preferred_element_type=F32) * dmat_ref[h]
            inner = jnp.dot(att.astype(BF16), v, preferred_element_type=F32)
            cross = jnp.dot(q, st.astype(BF16), preferred_element_type=F32) * qdec_ref[h]
            o = inner + cross
            kw = (k.astype(F32) * kdec_ref[h]).astype(BF16)
            s_sc[bb, h] = sdec_ref[h] * st + lax.dot_general(kw, v, _TN,
                                                            preferred_element_type=F32)
            mu = jnp.mean(o, axis=-1, keepdims=True)
            d = o - mu
            var = jnp.mean(d * d, axis=-1, keepdims=True)
            g = g_ref[bb, :, sl]
            o_ref[bb, :, sl] = (d * lax.rsqrt(var + GN_EPS)
                                * (g * jax.nn.sigmoid(g))).astype(BF16)

    @pl.when(c == pl.num_programs(1) - 1)
    def _():
        sout_ref[...] = s_sc[...]


def _retention_tables(n_heads, chunk, dk):
    lg = jnp.log(1.0 - jnp.exp2(-5.0 - jnp.arange(n_heads, dtype=F32)))
    idx = jnp.arange(chunk, dtype=F32)
    diff = idx[:, None] - idx[None, :]
    dmat = jnp.where(diff[None] >= 0, jnp.exp(diff[None] * lg[:, None, None]), 0.0)
    qdec = jnp.exp((idx + 1.0)[None, :] * lg[:, None])
    kdec = jnp.exp((chunk - 1.0 - idx)[None, :] * lg[:, None])
    sdec = jnp.exp(chunk * lg)
    bc = lambda a: jnp.broadcast_to(a[:, :, None], a.shape + (dk,))
    return dmat, bc(qdec), bc(kdec), jnp.broadcast_to(sdec[:, None, None], (n_heads, 1, dk))


def _retention(pbf3, gates3, state, *, n_heads, chunk, bpb, cast=(), layer=0):
    b, t, _ = pbf3.shape
    w = n_heads * LANES
    n_chunks = t // chunk
    n_steps = (b // bpb) * n_chunks
    tabs = _retention_tables(n_heads, chunk, LANES)
    has_state = state is not None
    const3 = lambda bi, c: (0, 0, 0)
    cast_in, cast_out, cast_shape = [], [], []
    for wt in cast:
        _, rows, cols = wt.shape
        rb = rows // n_steps
        assert rb * n_steps == rows and rb % 16 == 0, "row block must be whole bf16 tiles"
        cast_in.append(pl.BlockSpec((1, rb, cols), lambda bi, c: (layer, bi * n_chunks + c, 0)))
        cast_out.append(pl.BlockSpec((rb, cols), lambda bi, c: (bi * n_chunks + c, 0)))
        cast_shape.append(jax.ShapeDtypeStruct((rows, cols), BF16))
    in_specs = [pl.BlockSpec((bpb, chunk, w), lambda bi, c: (bi, c, _BF_QR)),
                pl.BlockSpec((bpb, chunk, w), lambda bi, c: (bi, c, _BF_KR)),
                pl.BlockSpec((bpb, chunk, w), lambda bi, c: (bi, c, _BF_VR)),
                pl.BlockSpec((bpb, chunk, w), lambda bi, c: (bi, c, _GATE_GR)),
                pl.BlockSpec((n_heads, chunk, chunk), const3),
                pl.BlockSpec((n_heads, chunk, LANES), const3),
                pl.BlockSpec((n_heads, chunk, LANES), const3),
                pl.BlockSpec((n_heads, 1, LANES), const3)]
    args = [pbf3, pbf3, pbf3, gates3, *tabs]
    state_spec = pl.BlockSpec((bpb, n_heads, LANES, LANES), lambda bi, c: (bi, 0, 0, 0))
    if has_state:
        in_specs.append(state_spec)
        args.append(state)
    return pl.pallas_call(
        functools.partial(_ret_kernel, n_heads=n_heads, has_state=has_state, bpb=bpb,
                          n_cast=len(cast)),
        grid=(b // bpb, n_chunks),
        in_specs=in_specs + cast_in,
        out_specs=[pl.BlockSpec((bpb, chunk, w), lambda bi, c: (bi, c, 0)), state_spec] + cast_out,
        out_shape=[jax.ShapeDtypeStruct((b, t, w), BF16),
                   jax.ShapeDtypeStruct((b, n_heads, LANES, LANES), F32)] + cast_shape,
        scratch_shapes=[pltpu.VMEM((bpb, n_heads, LANES, LANES), F32)],
        compiler_params=_params("arbitrary", "arbitrary"),
        name="retention",
    )(*args, *cast)


def _merge_kernel(oa_ref, ob_ref, ga_ref, gb_ref, x_ref, g1_ref, wpa_ref, wpb_ref, wo_ref,
                  lng_ref, lnb_ref, o_ref, *, alpha):
    a = jnp.dot(oa_ref[...], wpa_ref[...], preferred_element_type=F32)
    b = jnp.dot(ob_ref[...], wpb_ref[...], preferred_element_type=F32)
    merged = jax.nn.sigmoid(ga_ref[...]) * a + jax.nn.sigmoid(gb_ref[...]) * b
    y = jnp.dot(merged.astype(BF16), wo_ref[...], preferred_element_type=F32)
    x = x_ref[...]
    r = alpha * x + g1_ref[...] * y.reshape(x.shape)
    o_ref[...] = _layer_norm(r, lng_ref[...], lnb_ref[...])


def _row_tiling(x3, mod_group0, tm, n_grid_axes):
    g, r, d = x3.shape
    if r >= tm:
        tps = r // tm
        gb = 1
        if n_grid_axes == 1:
            x_map = lambda i: (i // tps, i % tps, 0)
            mod_idx = lambda k: (lambda i: (mod_group0 + i // tps, 0, k))
        else:
            x_map = lambda i, j: (i // tps, i % tps, 0)
            mod_idx = lambda k: (lambda i, j: (mod_group0 + i // tps, 0, k))
        x_spec = pl.BlockSpec((1, tm, d), x_map)
    else:
        tps = 1
        gb = tm // r
        if n_grid_axes == 1:
            x_map = lambda i: (i, 0, 0)
            mod_idx = lambda k: (lambda i: (mod_group0 // gb + i, 0, k))
        else:
            x_map = lambda i, j: (i, 0, 0)
            mod_idx = lambda k: (lambda i, j: (mod_group0 // gb + i, 0, k))
        x_spec = pl.BlockSpec((gb, r, d), x_map)
    mod_spec = lambda k: pl.BlockSpec((gb, 1, d), mod_idx(k))
    return x_spec, mod_spec, tps, gb


def _merge(oa2, ob2, gates, x3, mod3, mod_group0, wpa, wpb, wo, lng, lnb, *, tm, alpha):
    g, r, d = x3.shape
    m = g * r
    wa = oa2.shape[1]
    x_spec, mod_spec, _, _ = _row_tiling(x3, mod_group0, tm, 1)
    const = lambda i: (0, 0)
    return pl.pallas_call(
        functools.partial(_merge_kernel, alpha=alpha),
        grid=(m // tm,),
        in_specs=[pl.BlockSpec((tm, wa), lambda i: (i, 0)),
                  pl.BlockSpec((tm, wa), lambda i: (i, 0)),
                  pl.BlockSpec((tm, d), lambda i: (i, _GATE_GA // 2)),
                  pl.BlockSpec((tm, d), lambda i: (i, _GATE_GB // 2)),
                  x_spec,
                  mod_spec(2),
                  pl.BlockSpec(wpa.shape, const, pipeline_mode=pl.Buffered(1)),
                  pl.BlockSpec(wpb.shape, const, pipeline_mode=pl.Buffered(1)),
                  pl.BlockSpec(wo.shape, const, pipeline_mode=pl.Buffered(1)),
                  pl.BlockSpec((1, d), const),
                  pl.BlockSpec((1, d), const)],
        out_specs=x_spec,
        out_shape=jax.ShapeDtypeStruct(x3.shape, F32),
        compiler_params=_params("arbitrary"),
        name="merge_outproj",
    )(oa2, ob2, gates, gates, x3, mod3, wpa, wpb, wo, lng, lnb)


def _ffn_kernel(*refs, tm, tiles_per_seq, alpha, with_state, conv_w):
    if with_state:
        (x_hbm, sc_ref, sh_ref, g2_ref, wa_ref, wb_ref, wd_ref, cw_ref, cb_ref, lng_ref, lnb_ref,
         st_ref, o_ref, tail_ref, u_sc, xbuf, xsem, abuf) = refs
    else:
        (x_hbm, sc_ref, sh_ref, g2_ref, wa_ref, wb_ref, wd_ref, cw_ref, cb_ref, lng_ref, lnb_ref,
         o_ref, tail_ref, u_sc, xbuf, xsem, abuf, carry_sc) = refs
    i = pl.program_id(0)
    j = pl.program_id(1)
    hist = conv_w - 1
    x_copy = functools.partial(_x_tile_copy, x_hbm, xbuf, xsem, tm=tm,
                               tiles_per_seq=tiles_per_seq)

    @pl.when(j == 0)
    def _():
        @pl.when(i == 0)
        def _():
            x_copy(0).start()

        x_copy(i).wait()
        x = xbuf[...]
        u_sc[...] = (x * (1.0 + sc_ref[...]) + sh_ref[...]).reshape(tm, x.shape[-1]).astype(BF16)
        o_ref[...] = alpha * x

        @pl.when(i + 1 < pl.num_programs(0))
        def _():
            x_copy(i + 1).start()

    u = u_sc[...]
    a = jnp.dot(u, wa_ref[...], preferred_element_type=F32)
    b = jnp.dot(u, wb_ref[...], preferred_element_type=F32)
    tf = a.shape[-1]
    cw = cw_ref[...]
    if with_state:
        gb = tm // SUBLANES
        a3 = a.reshape(gb, SUBLANES, tf)
        abuf[:, SUBLANES:2 * SUBLANES, :] = a3
        abuf[:, SUBLANES - hist:SUBLANES, :] = st_ref[...]
        shifted = [abuf[:, SUBLANES - hist + w:2 * SUBLANES - hist + w, :].reshape(tm, tf)
                   for w in range(hist)]
        tail_ref[...] = a3[:, SUBLANES - hist:, :]
    else:
        first = (i % tiles_per_seq) == 0
        abuf[0:SUBLANES, :] = jnp.where(first, 0.0, carry_sc[j])
        abuf[SUBLANES:SUBLANES + tm, :] = a
        shifted = [abuf[SUBLANES - hist + w:SUBLANES - hist + w + tm, :] for w in range(hist)]
        carry_sc[j] = a[tm - SUBLANES:, :]
        tail_ref[0] = a[tm - hist:, :]
    conv = a * cw[hist:hist + 1, :]
    for w in range(hist):
        conv = conv + shifted[w] * cw[w:w + 1, :]
    conv = conv + cb_ref[...]
    hg = (jax.nn.gelu(conv) * b).astype(BF16)
    y = jnp.dot(hg, wd_ref[...], preferred_element_type=F32).reshape(o_ref.shape)
    o_ref[...] += g2_ref[...] * y

    @pl.when(j == pl.num_programs(1) - 1)
    def _():
        o_ref[...] = _layer_norm(o_ref[...], lng_ref[...], lnb_ref[...])


def _ffn(x3, mod3, mod_group0, wup, wdown, conv_w, conv_b, lng, lnb, conv_state,
         *, tm, tf, alpha):
    g, r, d = x3.shape
    m = g * r
    dff = wdown.shape[0]
    nj = dff // tf
    cwid = conv_w.shape[0]
    hist = cwid - 1
    x_spec, mod_spec, tps, gb = _row_tiling(x3, mod_group0, tm, 2)
    with_state = conv_state is not None
    const = lambda i, j: (0, 0)
    in_specs = [pl.BlockSpec(memory_space=pl.ANY),
                mod_spec(4), mod_spec(3), mod_spec(5),
                pl.BlockSpec((d, tf), lambda i, j: (0, j)),
                pl.BlockSpec((d, tf), lambda i, j: (0, nj + j)),
                pl.BlockSpec((tf, d), lambda i, j: (j, 0)),
                pl.BlockSpec((cwid, tf), lambda i, j: (0, j)),
                pl.BlockSpec((1, tf), lambda i, j: (0, j)),
                pl.BlockSpec((1, d), const),
                pl.BlockSpec((1, d), const)]
    args = [x3, mod3, mod3, mod3, wup, wup, wdown, conv_w, conv_b.reshape(1, dff), lng, lnb]
    scratch = [pltpu.VMEM((tm, d), BF16), pltpu.VMEM(x_spec.block_shape, F32),
               pltpu.SemaphoreType.DMA(())]
    if with_state:
        assert r == SUBLANES
        in_specs.append(pl.BlockSpec((gb, hist, tf), lambda i, j: (i, 0, j)))
        args.append(conv_state)
        tail_spec = pl.BlockSpec((gb, hist, tf), lambda i, j: (i, 0, j))
        scratch.append(pltpu.VMEM((gb, 2 * SUBLANES, tf), F32))
    else:
        tail_spec = pl.BlockSpec((1, hist, tf), lambda i, j: (i, 0, j))
        scratch += [pltpu.VMEM((tm + SUBLANES, tf), F32), pltpu.VMEM((nj, SUBLANES, tf), F32)]
    n_tails = g if with_state else m // tm
    kern = functools.partial(_ffn_kernel, tm=tm, tiles_per_seq=tps, alpha=alpha,
                             with_state=with_state, conv_w=cwid)
    y, tails = pl.pallas_call(
        kern,
        grid=(m // tm, nj),
        in_specs=in_specs,
        out_specs=[x_spec, tail_spec],
        out_shape=[jax.ShapeDtypeStruct(x3.shape, F32),
                   jax.ShapeDtypeStruct((n_tails, hist, dff), F32)],
        scratch_shapes=scratch,
        compiler_params=_params("arbitrary", "arbitrary"),
        name="convffn",
    )(*args)
    if not with_state:
        tails = tails.reshape(g, tps, hist, dff)[:, tps - 1]
    return y, tails


def _rope_tables(pos, half):
    inv = ROPE_BASE ** (-jnp.arange(half, dtype=F32) / half)
    ang = pos.astype(F32)[:, None] * inv[None, :]
    cos, sin = jnp.cos(ang), jnp.sin(ang)
    return jnp.concatenate([cos, cos], axis=-1), jnp.concatenate([-sin, sin], axis=-1)


def kernel(x_prompt, x_sample, cache_k, cache_v, cache_logf, state_ret, state_conv, page_table,
           c_prompt, c_sample, w_ada, b_ada, w_in, b_f, w_pa, w_pb, w_o, ln1_g, ln1_b,
           w_up, conv_w, conv_b, w_down, ln2_g, ln2_b):
    depth = w_ada.shape[0]
    b, t, d = x_prompt.shape
    bs, ts, _ = x_sample.shape
    n_pool, page_size, n_heads, dh = cache_k.shape[1:]
    n_pages = page_table.shape[1]
    past = n_pages * page_size
    wa = n_heads * dh
    dff = w_down.shape[1]
    alpha = (2.0 * depth) ** 0.25
    assert wa == 1024 and dh == LANES and ts == SUBLANES and n_pages % SUBLANES == 0
    assert w_in.shape[2] == 7 * wa + n_heads + 2 * d and d == 2 * wa

    tm_p, tm_s = 1024, bs * ts
    tq = 512
    ret_chunk = 128 if t % 128 == 0 else t

    cos_p, sin_p = _rope_tables(jnp.arange(t), dh // 2)
    cos_s, sin_s = _rope_tables(past + jnp.arange(ts), dh // 2)
    cos_s, sin_s = jnp.tile(cos_s, (tm_s // ts, 1)), jnp.tile(sin_s, (tm_s // ts, 1))

    n_mod = -(-(bs + b) // 16) * 16
    c_all = jnp.concatenate([c_sample, c_prompt, jnp.zeros((n_mod - bs - b, d), F32)], axis=0)

    w_in_t = jnp.swapaxes(w_in, 1, 2)

    hp, hs = x_prompt, x_sample
    outs = {k: [] for k in ("kp", "vp", "lp", "rp", "cp", "ks", "vs", "ls", "rs", "cs")}
    for l in range(depth):
        mod3 = _ada(c_all, w_ada[l], b_ada[l]).reshape(n_mod, 1, 6 * d)

        fa0 = 3 * wa
        w_main = _repack_w_in(w_in_t, l, fa0, n_heads)
        wf_t = jnp.zeros((16, d), F32).at[:n_heads].set(w_in_t[l, fa0:fa0 + n_heads, :])
        bf_col = jnp.zeros((16, 1), F32).at[:n_heads, 0].set(b_f[l].astype(F32))
        lng1, lnb1 = ln1_g[l].reshape(1, d), ln1_b[l].reshape(1, d)
        lng2, lnb2 = ln2_g[l].reshape(1, d), ln2_b[l].reshape(1, d)

        pbf, pk, pv, pg, lf_t, c_t = _inproj(hp, mod3, bs, w_main, wf_t, bf_col, cos_p, sin_p,
                                             tm=tm_p, seq_tiled=True)
        pbf3 = pbf.reshape(b, t, -1)
        nt = t // tm_p
        c4 = (c_t.reshape(b, nt, 16, tm_p // tq, tq).transpose(0, 1, 3, 2, 4)
              .reshape(b, t // tq, 16, tq))
        sbf, sk, sv, sg, lfs_t, _ = _inproj(hs, mod3, 0, w_main, wf_t, bf_col, cos_s, sin_s,
                                            tm=tm_s, seq_tiled=False)
        rows_q = ts * n_heads
        q3 = sbf[:, _BF_QA * wa:(_BF_QA + 1) * wa].reshape(bs, rows_q, dh)
        k_new = sk.reshape(bs, ts, n_heads, dh)
        v_new = sv.reshape(bs, ts, n_heads, dh)
        lf_s = lfs_t[0, :n_heads, :].T.reshape(bs, ts, n_heads)
        lfn3 = jnp.pad(lf_s.reshape(bs, 1, rows_q), ((0, 0), (0, 0), (0, LANES - rows_q)))
        ck3 = cache_k[l].reshape(n_pool, page_size * n_heads, dh)
        cv3 = cache_v[l].reshape(n_pool, page_size * n_heads, dh)
        clf3 = cache_logf[l].astype(F32).reshape(n_pool, 1, page_size * n_heads)
        oa, oa_s = _fox_attention(pbf3, pk.reshape(b, t, wa), pv.reshape(b, t, wa), c4,
                                  page_table, q3, k_new.reshape(bs, rows_q, dh),
                                  v_new.reshape(bs, rows_q, dh), lfn3, ck3, cv3, clf3,
                                  n_heads=n_heads, tq=tq, hpb=2, n_new=ts)

        ob, ret_p, wpa, wpb, wo, wup, wdown = _retention(
            pbf3, pg.reshape(b, t, -1), None, n_heads=n_heads, chunk=ret_chunk, bpb=4,
            cast=(w_pa, w_pb, w_o, w_up, w_down), layer=l)
        x1 = _merge(oa.reshape(b * t, wa), ob.reshape(b * t, wa), pg, hp, mod3, bs,
                    wpa, wpb, wo, lng1, lnb1, tm=256, alpha=alpha)
        hp, conv_p = _ffn(x1, mod3, bs, wup, wdown, conv_w[l], conv_b[l], lng2, lnb2,
                          None, tm=1024, tf=512, alpha=alpha)
        outs["kp"].append(pk.reshape(b, t, n_heads, dh))
        outs["vp"].append(pv.reshape(b, t, n_heads, dh))
        lf_p = lf_t[:, :n_heads, :].reshape(b, nt, n_heads, tm_p)
        outs["lp"].append(lf_p.transpose(0, 1, 3, 2).reshape(b, t, n_heads))
        outs["rp"].append(ret_p)
        outs["cp"].append(conv_p)

        sbf3 = sbf.reshape(bs, ts, -1)
        ob_s, ret_s = _retention(sbf3, sg.reshape(bs, ts, -1), state_ret[l], n_heads=n_heads,
                                 chunk=ts, bpb=8)
        x1s = _merge(oa_s.reshape(bs * ts, wa), ob_s.reshape(bs * ts, wa), sg, hs, mod3, 0,
                     wpa, wpb, wo, lng1, lnb1, tm=256, alpha=alpha)
        hs, conv_s = _ffn(x1s, mod3, 0, wup, wdown, conv_w[l], conv_b[l], lng2, lnb2,
                          state_conv[l], tm=1024, tf=512, alpha=alpha)
        outs["ks"].append(k_new)
        outs["vs"].append(v_new)
        outs["ls"].append(lf_s)
        outs["rs"].append(ret_s)
        outs["cs"].append(conv_s)

    st = lambda name: jnp.stack(outs[name])
    return (hp, hs, st("kp"), st("vp"), st("lp"), st("rp"), st("cp"),
            st("ks"), st("vs"), st("ls"), st("rs"), st("cs"))
```

```python
import functools
import math

import jax
import jax.numpy as jnp
from jax import lax
from jax.experimental import pallas as pl
from jax.experimental.pallas import tpu as pltpu

F32 = jnp.float32
BF16 = jnp.bfloat16

LANES = 128
SUBLANES = 8
VMEM_LIMIT = 56 * 1024 * 1024

ROPE_BASE = 10000.0
LN_EPS = 1e-5
GN_EPS = 1e-5
NEG_BIG = -1e30
LOG2E = math.log2(math.e)

_NT = (((1,), (1,)), ((), ()))
_TN = (((0,), (0,)), ((), ()))


def _params(*sem):
    return pltpu.CompilerParams(dimension_semantics=sem, vmem_limit_bytes=VMEM_LIMIT)


def _layer_norm(r, g, b):
    mu = jnp.mean(r, axis=-1, keepdims=True)
    d = r - mu
    var = jnp.mean(d * d, axis=-1, keepdims=True)
    return d * lax.rsqrt(var + LN_EPS) * g + b


def _ada_kernel(c_ref, w_ref, b_ref, o_ref):
    c = c_ref[...]
    s = (c * jax.nn.sigmoid(c)).astype(BF16)
    o_ref[...] = jnp.dot(s, w_ref[...].astype(BF16), preferred_element_type=F32) + b_ref[...]


def _ada(c_all, w_ada, b_ada):
    rows, d = c_all.shape
    n = w_ada.shape[1]
    tn = 512
    return pl.pallas_call(
        _ada_kernel,
        grid=(n // tn,),
        in_specs=[pl.BlockSpec((rows, d), lambda j: (0, 0)),
                  pl.BlockSpec((d, tn), lambda j: (0, j)),
                  pl.BlockSpec((1, tn), lambda j: (0, j))],
        out_specs=pl.BlockSpec((rows, tn), lambda j: (0, j)),
        out_shape=jax.ShapeDtypeStruct((rows, n), F32),
        compiler_params=_params("arbitrary"),
        name="ada_mod",
    )(c_all, w_ada, b_ada.reshape(1, n))


(_G_QA, _G_KA, _G_VA, _G_QR, _G_KR, _G_VR, _G_GA0, _G_GA1, _G_GB0, _G_GB1, _G_GR) = range(11)
_N_GROUPS = 11
_W_BLOCK_OF_GR = 6
_BF_QA, _BF_QR, _BF_KR, _BF_VR = 0, 1, 2, 3
_GATE_GA, _GATE_GB, _GATE_GR = 0, 2, 4


def _w_block(j):
    return jnp.where(j < _G_GA0, j, jnp.where(j < _G_GR, j + 1, _W_BLOCK_OF_GR))


def _bf_block(j):
    return jnp.clip(j - (_G_QR - _BF_QR), _BF_QA, _BF_VR)


def _gate_block(j):
    return jnp.clip(j - _G_GA0, _GATE_GA, _GATE_GR)


def _repack_kernel(wt_ref, o_ref):
    o_ref[...] = wt_ref[0].T.astype(BF16)


def _repack_w_in(w_in_t, layer, fa0, n_f):
    _, n, d = w_in_t.shape
    tc = 1024
    assert fa0 % tc == 0 and (n - n_f) % tc == 0

    def rows(j):
        return pl.multiple_of(jnp.where(j < fa0 // tc, j * tc, j * tc + n_f), SUBLANES)

    return pl.pallas_call(
        _repack_kernel,
        grid=((n - n_f) // tc,),
        in_specs=[pl.BlockSpec((pl.Element(1), pl.Element(tc), pl.Element(d)),
                               lambda j: (layer, rows(j), 0))],
        out_specs=pl.BlockSpec((d, tc), lambda j: (0, j)),
        out_shape=jax.ShapeDtypeStruct((d, n - n_f), BF16),
        compiler_params=_params("arbitrary"),
        name="repack_w_in",
    )(w_in_t)


def _lane_cumsum(x, carry):
    rows, width = x.shape
    lane = lax.broadcasted_iota(jnp.int32, (rows, LANES), 1)
    out = []
    for c in range(width // LANES):
        v = x[:, c * LANES:(c + 1) * LANES]
        s = 1
        while s < LANES:
            v = v + jnp.where(lane >= s, pltpu.roll(v, s, axis=1), 0.0)
            s *= 2
        out.append(v + carry)
        carry = carry + jnp.broadcast_to(v[:, LANES - 1:LANES], (rows, LANES))
    return jnp.concatenate(out, axis=1), carry


def _x_tile_copy(x_hbm, xbuf, sem, tile, *, tm, tiles_per_seq):
    if tiles_per_seq > 1:
        src = x_hbm.at[pl.ds(tile // tiles_per_seq, 1), pl.ds((tile % tiles_per_seq) * tm, tm)]
    else:
        gb = xbuf.shape[0]
        src = x_hbm.at[pl.ds(tile * gb, gb)]
    return pltpu.make_async_copy(src, xbuf, sem)


def _inproj_kernel(x_hbm, sc_ref, sh_ref, w_ref, wf_ref, bf_ref, cos_ref, sin_ref, after_ref,
                   obf_ref, ok_ref, ov_ref, og_ref, lf_ref, c_ref, u_sc, carry_sc, xbuf, xsem,
                   *, tm, tiles_per_seq, qa_scale, kr_scale, n_heads):
    del after_ref
    i = pl.program_id(0)
    j = pl.program_id(1)
    x_copy = functools.partial(_x_tile_copy, x_hbm, xbuf, xsem, tm=tm,
                               tiles_per_seq=tiles_per_seq)

    @pl.when(j == 0)
    def _():
        @pl.when(i == 0)
        def _():
            x_copy(0).start()

        x_copy(i).wait()
        x = xbuf[...]
        u = (x * (1.0 + sc_ref[...]) + sh_ref[...]).reshape(tm, x.shape[-1]).astype(BF16)
        u_sc[...] = u

        @pl.when(i + 1 < pl.num_programs(0))
        def _():
            x_copy(i + 1).start()

        z = lax.dot_general(wf_ref[...].astype(BF16), u, _NT,
                            preferred_element_type=F32) + bf_ref[...]
        lf = jnp.minimum(z, 0.0) - jnp.log1p(jnp.exp(-jnp.abs(z)))
        lf_ref[0] = lf
        first = (i % tiles_per_seq) == 0
        prev = jnp.where(first, 0.0, carry_sc[...])
        c, last = _lane_cumsum(lf, prev)
        c_ref[0] = c * LOG2E
        carry_sc[...] = last

    def project():
        return jnp.dot(u_sc[...], w_ref[...], preferred_element_type=F32)

    @pl.when(j == _G_QA)
    def _():
        obf_ref[...] = (project() * qa_scale).astype(BF16)

    def rope(scale):
        acc = project()
        cos = cos_ref[...]
        sin = sin_ref[...]
        for h in range(n_heads):
            sl = slice(h * LANES, (h + 1) * LANES)
            a = acc[:, sl]
            r = a * cos + pltpu.roll(a, LANES // 2, axis=1) * sin
            if scale is not None:
                r = r * scale
            obf_ref[:, sl] = r.astype(BF16)

    @pl.when(j == _G_QR)
    def _():
        rope(None)

    @pl.when(j == _G_KR)
    def _():
        rope(kr_scale)

    @pl.when(j == _G_VR)
    def _():
        obf_ref[...] = project().astype(BF16)

    @pl.when(j == _G_KA)
    def _():
        ok_ref[...] = project()

    @pl.when(j == _G_VA)
    def _():
        ov_ref[...] = project()

    @pl.when(j >= _G_GA0)
    def _():
        og_ref[...] = project()


def _inproj(x3, mod3, mod_group0, w_main, wf_t, bf_col, cos_t, sin_t, after, *, tm, seq_tiled):
    g, r, d = x3.shape
    m = g * r
    n_tiles = m // tm
    if seq_tiled:
        tps = r // tm
        gb = 1
        x_block = (1, tm, d)
        mod_idx = lambda k: (lambda i, j: (mod_group0 + i // tps, 0, k))
        tab_spec = pl.BlockSpec((tm, LANES), lambda i, j: (i % tps, 0))
    else:
        tps = 1
        gb = tm // r
        x_block = (gb, r, d)
        mod_idx = lambda k: (lambda i, j: (mod_group0 // gb + i, 0, k))
        tab_spec = pl.BlockSpec((tm, LANES), lambda i, j: (0, 0))
    n_bf = _BF_VR + 1
    kern = functools.partial(_inproj_kernel, tm=tm, tiles_per_seq=tps,
                             qa_scale=LANES ** -0.5 * LOG2E, kr_scale=LANES ** -0.5,
                             n_heads=1024 // LANES)
    return pl.pallas_call(
        kern,
        grid=(n_tiles, _N_GROUPS),
        in_specs=[pl.BlockSpec(memory_space=pl.ANY),
                  pl.BlockSpec((gb, 1, d), mod_idx(1)),
                  pl.BlockSpec((gb, 1, d), mod_idx(0)),
                  pl.BlockSpec((d, 1024), lambda i, j: (0, _w_block(j))),
                  pl.BlockSpec((16, d), lambda i, j: (0, 0)),
                  pl.BlockSpec((16, 1), lambda i, j: (0, 0)),
                  tab_spec, tab_spec,
                  pl.BlockSpec(after.shape, lambda i, j: (0,) * after.ndim)],
        out_specs=[pl.BlockSpec((tm, 1024), lambda i, j: (i, _bf_block(j))),
                   pl.BlockSpec((tm, 1024), lambda i, j: (i, 0)),
                   pl.BlockSpec((tm, 1024), lambda i, j: (i, 0)),
                   pl.BlockSpec((tm, 1024), lambda i, j: (i, _gate_block(j))),
                   pl.BlockSpec((1, 16, tm), lambda i, j: (i, 0, 0)),
                   pl.BlockSpec((1, 16, tm), lambda i, j: (i, 0, 0))],
        out_shape=[jax.ShapeDtypeStruct((m, 1024 * n_bf), BF16),
                   jax.ShapeDtypeStruct((m, 1024), F32),
                   jax.ShapeDtypeStruct((m, 1024), F32),
                   jax.ShapeDtypeStruct((m, 1024 * (_GATE_GR + 1)), F32),
                   jax.ShapeDtypeStruct((n_tiles, 16, tm), F32),
                   jax.ShapeDtypeStruct((n_tiles, 16, tm), F32)],
        scratch_shapes=[pltpu.VMEM((tm, d), BF16), pltpu.VMEM((16, LANES), F32),
                        pltpu.VMEM(x_block, F32), pltpu.SemaphoreType.DMA(())],
        compiler_params=_params("arbitrary", "arbitrary"),
        name="inproj",
    )(x3, mod3, mod3, w_main, wf_t, bf_col, cos_t, sin_t, after)


def _fox_prefill_body(q_ref, k_ref, v_ref, c_ref, o_ref, hg, qi, *, tq, hpb):
    dh = LANES

    def chunk(kc, carry, diagonal):
        start = pl.multiple_of(kc * tq, tq)
        out = []
        for hh in range(hpb):
            m, l, acc = carry[hh]
            sl = slice(hh * dh, (hh + 1) * dh)
            kk = k_ref[0, pl.ds(start, tq), sl].astype(BF16)
            vv = v_ref[0, pl.ds(start, tq), sl].astype(BF16)
            s = lax.dot_general(q_ref[0, :, sl], kk, _NT, preferred_element_type=F32)
            s = s - c_ref[0, kc, pl.ds(hg * hpb + hh, 1), :]
            if diagonal:
                row = lax.broadcasted_iota(jnp.int32, (tq, tq), 0)
                col = lax.broadcasted_iota(jnp.int32, (tq, tq), 1)
                s = jnp.where(row >= col, s, NEG_BIG)
            m_new = jnp.maximum(m, jnp.max(s, axis=-1, keepdims=True))
            alpha = jnp.exp2(m - m_new)
            p = jnp.exp2(s - m_new)
            l = alpha * l + jnp.sum(p, axis=-1, keepdims=True)
            acc = alpha * acc + jnp.dot(p.astype(BF16), vv, preferred_element_type=F32)
            out.append((m_new, l, acc))
        return tuple(out)

    init = tuple((jnp.full((tq, 1), NEG_BIG, F32), jnp.zeros((tq, 1), F32),
                  jnp.zeros((tq, dh), F32)) for _ in range(hpb))
    carry = lax.fori_loop(0, qi, lambda kc, c: chunk(kc, c, False), init)
    carry = chunk(qi, carry, True)
    for hh in range(hpb):
        _, l, acc = carry[hh]
        o_ref[0, :, hh * dh:(hh + 1) * dh] = (acc / l).astype(BF16)


def _periodic_tail(v, lane, n_heads):
    y = jnp.where(lane >= LANES - n_heads, v, 0.0)
    s = n_heads
    while s < LANES:
        y = y + pltpu.roll(y, LANES - s, axis=1)
        s *= 2
    return y


def _page_copies(pt_ref, ck_hbm, cv_hbm, clf_hbm, kbuf, vbuf, lfbuf, sems, seq, slot, n_pages):
    copies = []
    for p in range(n_pages):
        page = pt_ref[seq * n_pages + p]
        copies.append(pltpu.make_async_copy(ck_hbm.at[page], kbuf.at[slot, p], sems.at[slot, 0]))
        copies.append(pltpu.make_async_copy(cv_hbm.at[page], vbuf.at[slot, p], sems.at[slot, 1]))
        copies.append(pltpu.make_async_copy(clf_hbm.at[page], lfbuf.at[slot, p], sems.at[slot, 2]))
    return copies


def _fox_decode_issue(b, copies):
    @pl.when(b == 0)
    def _():
        for cp in copies(seq=0, slot=0):
            cp.start()

    @pl.when(b + 1 < pl.num_programs(0))
    def _():
        for cp in copies(seq=b + 1, slot=1 - b % 2):
            cp.start()


def _fox_decode_attend(b, copies, q_ref, kn_ref, vn_ref, lfn_ref, o_ref, kbuf, vbuf, lfbuf,
                       lf_sc, s_sc, *, n_pages, n_heads, n_new):
    slot = b % 2
    for cp in copies(seq=b, slot=slot):
        cp.wait()

    k_refs = [kbuf.at[slot, p] for p in range(n_pages)]
    v_refs = [vbuf.at[slot, p] for p in range(n_pages)]
    lf_refs = [lfbuf.at[slot, p] for p in range(n_pages)]

    page_w = lfbuf.shape[-1]
    ppr = n_pages // SUBLANES
    n_chunks = ppr * page_w // LANES
    rows_q = n_new * n_heads

    for p in range(n_pages):
        r, part = divmod(p, ppr)
        lf_sc[r:r + 1, part * page_w:(part + 1) * page_w] = lf_refs[p][...] * LOG2E
    lane = lax.broadcasted_iota(jnp.int32, (SUBLANES, LANES), 1)
    sub = lax.broadcasted_iota(jnp.int32, (SUBLANES, LANES), 0)
    chunks = []
    carry = jnp.zeros((SUBLANES, LANES), F32)
    for c in range(n_chunks):
        v = lf_sc[:, c * LANES:(c + 1) * LANES]
        s = n_heads
        while s < LANES:
            v = v + jnp.where(lane >= s, pltpu.roll(v, s, axis=1), 0.0)
            s *= 2
        chunks.append(v + carry)
        carry = carry + _periodic_tail(v, lane, n_heads)
    inc = carry
    s = 1
    while s < SUBLANES:
        inc = inc + jnp.where(sub >= s, pltpu.roll(inc, s, axis=0), 0.0)
        s *= 2
    exc = jnp.where(sub >= 1, pltpu.roll(inc, 1, axis=0), 0.0)
    chunks = [v + exc for v in chunks]
    past_total = inc[SUBLANES - 1:SUBLANES, :]

    q = q_ref[0]
    dh = q.shape[-1]
    row_h = lax.broadcasted_iota(jnp.int32, (rows_q, page_w), 0) % n_heads
    col_h = lax.broadcasted_iota(jnp.int32, (rows_q, page_w), 1) % n_heads
    same_head = row_h == col_h

    cpp = page_w // LANES
    mx = jnp.full((rows_q, LANES), NEG_BIG, F32)
    for p in range(n_pages):
        r, part = divmod(p, ppr)
        ck = jnp.concatenate([chunks[part * cpp + c][r:r + 1, :] for c in range(cpp)], axis=1)
        kk = k_refs[p][...].astype(BF16)
        s = lax.dot_general(q, kk, _NT, preferred_element_type=F32)
        s = jnp.where(same_head, s - ck, NEG_BIG)
        s_sc[p] = s
        for c in range(cpp):
            mx = jnp.maximum(mx, s[:, c * LANES:(c + 1) * LANES])

    lane1 = lax.broadcasted_iota(jnp.int32, (1, LANES), 1)
    cn = lfn_ref[0] * LOG2E
    s = n_heads
    while s < rows_q:
        cn = cn + jnp.where(lane1 >= s, pltpu.roll(cn, s, axis=1), 0.0)
        s *= 2
    cn = cn + past_total
    s_new = lax.dot_general(q, kn_ref[0].astype(BF16), _NT, preferred_element_type=F32)
    row = lax.broadcasted_iota(jnp.int32, (rows_q, rows_q), 0)
    col = lax.broadcasted_iota(jnp.int32, (rows_q, rows_q), 1)
    ok = ((row % n_heads) == (col % n_heads)) & (col <= row)
    s_new = jnp.where(ok, s_new - cn[:, :rows_q], NEG_BIG)
    m = jnp.maximum(jnp.max(mx, axis=-1, keepdims=True), jnp.max(s_new, axis=-1, keepdims=True))

    lsum = jnp.zeros((rows_q, LANES), F32)
    acc = jnp.zeros((rows_q, dh), F32)
    for p in range(n_pages):
        pr = jnp.exp2(s_sc[p] - m)
        for c in range(cpp):
            lsum = lsum + pr[:, c * LANES:(c + 1) * LANES]
        acc = acc + jnp.dot(pr.astype(BF16), v_refs[p][...].astype(BF16),
                            preferred_element_type=F32)
    pr = jnp.exp2(s_new - m)
    l = jnp.sum(lsum, axis=-1, keepdims=True) + jnp.sum(pr, axis=-1, keepdims=True)
    acc = acc + jnp.dot(pr.astype(BF16), vn_ref[0].astype(BF16), preferred_element_type=F32)
    o_ref[0] = (acc / l).astype(BF16)


def _fox_kernel(pt_ref, qp_ref, kp_ref, vp_ref, cp_ref, q_ref, kn_ref, vn_ref, lfn_ref,
                ck_hbm, cv_hbm, clf_hbm, op_ref, o_ref, kbuf, vbuf, lfbuf, sems, lf_sc, s_sc,
                *, tq, hpb, n_hg, n_q, n_pages, n_heads, n_new):
    step = pl.program_id(0)
    copies = functools.partial(_page_copies, pt_ref, ck_hbm, cv_hbm, clf_hbm, kbuf, vbuf, lfbuf,
                               sems, n_pages=n_pages)
    _fox_decode_issue(step, copies)
    _fox_prefill_body(qp_ref, kp_ref, vp_ref, cp_ref, op_ref, (step // n_q) % n_hg, step % n_q,
                      tq=tq, hpb=hpb)
    _fox_decode_attend(step, copies, q_ref, kn_ref, vn_ref, lfn_ref, o_ref, kbuf, vbuf, lfbuf,
                       lf_sc, s_sc, n_pages=n_pages, n_heads=n_heads, n_new=n_new)


def _fox_attention(pbf3, k3, v3, c4, page_table, q3, kn3, vn3, lfn3, ck3, cv3, clf3,
                   *, n_heads, tq, hpb, n_new):
    b, t, _ = pbf3.shape
    w = hpb * LANES
    n_hg = n_heads // hpb
    n_q = t // tq
    bs, n_pages = page_table.shape
    rows_q, dh = q3.shape[1], q3.shape[2]
    page_rows = ck3.shape[1]
    page_w = clf3.shape[2]
    assert b * n_hg * n_q == bs, "prefill steps and decode sequences must pair up"

    def pre(col0):
        return lambda s, pt: (s // (n_hg * n_q), s % n_q, col0 + (s // n_q) % n_hg)

    def pre_kv(s, pt):
        return (s // (n_hg * n_q), 0, (s // n_q) % n_hg)

    def same(s, pt):
        return (s, 0, 0)

    hbm = pl.BlockSpec(memory_space=pl.ANY)
    grid_spec = pltpu.PrefetchScalarGridSpec(
        num_scalar_prefetch=1,
        grid=(bs,),
        in_specs=[pl.BlockSpec((1, tq, w), pre(_BF_QA * n_hg)),
                  pl.BlockSpec((1, t, w), pre_kv),
                  pl.BlockSpec((1, t, w), pre_kv),
                  pl.BlockSpec((1, n_q, 16, tq), lambda s, pt: (s // (n_hg * n_q), 0, 0, 0)),
                  pl.BlockSpec((1, rows_q, dh), same),
                  pl.BlockSpec((1, rows_q, dh), same),
                  pl.BlockSpec((1, rows_q, dh), same),
                  pl.BlockSpec((1, 1, LANES), same),
                  hbm, hbm, hbm],
        out_specs=[pl.BlockSpec((1, tq, w), pre(0)),
                   pl.BlockSpec((1, rows_q, dh), same)],
        scratch_shapes=[pltpu.VMEM((2, n_pages, page_rows, dh), F32),
                        pltpu.VMEM((2, n_pages, page_rows, dh), F32),
                        pltpu.VMEM((2, n_pages, 1, page_w), F32),
                        pltpu.SemaphoreType.DMA((2, 3)),
                        pltpu.VMEM((SUBLANES, n_pages // SUBLANES * page_w), F32),
                        pltpu.VMEM((n_pages, rows_q, page_w), F32)],
    )
    return pl.pallas_call(
        functools.partial(_fox_kernel, tq=tq, hpb=hpb, n_hg=n_hg, n_q=n_q, n_pages=n_pages,
                          n_heads=n_heads, n_new=n_new),
        grid_spec=grid_spec,
        out_shape=[jax.ShapeDtypeStruct((b, t, n_heads * LANES), BF16),
                   jax.ShapeDtypeStruct((bs, rows_q, dh), BF16)],
        compiler_params=_params("arbitrary"),
        name="fox_attention",
    )(page_table.reshape(-1), pbf3, k3, v3, c4, q3, kn3, vn3, lfn3, ck3, cv3, clf3)


def _cast_specs(cast, layer, n_steps, step):
    ins, outs, shapes = [], [], []
    for wt in cast:
        _, rows, cols = wt.shape
        assert rows % 16 == 0
        every = 1
        while rows % ((16 * n_steps) // every):
            every *= 2
        rb = rows // (n_steps // every)
        ins.append(pl.BlockSpec((1, rb, cols), lambda *g, e=every: (layer, step(*g) // e, 0)))
        outs.append(pl.BlockSpec((rb, cols), lambda *g, e=every: (step(*g) // e, 0)))
        shapes.append(jax.ShapeDtypeStruct((rows, cols), BF16))
    return ins, outs, shapes


def _ret_kernel(*refs, n_heads, has_state, bpb, n_cast):
    n_in = 8 + (1 if has_state else 0)
    cast_in = refs[n_in:n_in + n_cast]
    cast_out = refs[n_in + n_cast + 2:n_in + 2 * n_cast + 2]
    refs = refs[:n_in] + refs[n_in + n_cast:n_in + n_cast + 2] + refs[n_in + 2 * n_cast + 2:]
    for src, dst in zip(cast_in, cast_out):
        dst[...] = src[0].astype(BF16)
    if has_state:
        (q_ref, k_ref, v_ref, g_ref, dmat_ref, qdec_ref, kdec_ref, sdec_ref, s0_ref,
         o_ref, sout_ref, s_sc) = refs
    else:
        (q_ref, k_ref, v_ref, g_ref, dmat_ref, qdec_ref, kdec_ref, sdec_ref,
         o_ref, sout_ref, s_sc) = refs
    c = pl.program_id(1)

    @pl.when(c == 0)
    def _():
        if has_state:
            s_sc[...] = s0_ref[...]
        else:
            s_sc[...] = jnp.zeros_like(s_sc)

    for bb in range(bpb):
        for h in range(n_heads):
            sl = slice(h * LANES, (h + 1) * LANES)
            q = q_ref[bb, :, sl]
            k = k_ref[bb, :, sl]
            v = v_ref[bb, :, sl]
            st = s_sc[bb, h]
            att = lax.dot_general(q, k, _NT, preferred_element_type=F32) * dmat_ref[h]
            inner = jnp.dot(att.astype(BF16), v, preferred_element_type=F32)
            cross = jnp.dot(q, st.astype(BF16), preferred_element_type=F32) * qdec_ref[h]
            o = inner + cross
            kw = (k.astype(F32) * kdec_ref[h]).astype(BF16)
            s_sc[bb, h] = sdec_ref[h] * st + lax.dot_general(kw, v, _TN,
                                                            preferred_element_type=F32)
            mu = jnp.mean(o, axis=-1, keepdims=True)
            d = o - mu
            var = jnp.mean(d * d, axis=-1, keepdims=True)
            g = g_ref[bb, :, sl]
            o_ref[bb, :, sl] = (d * lax.rsqrt(var + GN_EPS)
                                * (g * jax.nn.sigmoid(g))).astype(BF16)

    @pl.when(c == pl.num_programs(1) - 1)
    def _():
        sout_ref[...] = s_sc[...]


def _retention_tables(n_heads, chunk, dk):
    lg = jnp.log(1.0 - jnp.exp2(-5.0 - jnp.arange(n_heads, dtype=F32)))
    idx = jnp.arange(chunk, dtype=F32)
    diff = idx[:, None] - idx[None, :]
    dmat = jnp.where(diff[None] >= 0, jnp.exp(diff[None] * lg[:, None, None]), 0.0)
    qdec = jnp.exp((idx + 1.0)[None, :] * lg[:, None])
    kdec = jnp.exp((chunk - 1.0 - idx)[None, :] * lg[:, None])
    sdec = jnp.exp(chunk * lg)
    bc = lambda a: jnp.broadcast_to(a[:, :, None], a.shape + (dk,))
    return dmat, bc(qdec), bc(kdec), jnp.broadcast_to(sdec[:, None, None], (n_heads, 1, dk))


def _retention(pbf3, gates3, state, *, n_heads, chunk, bpb, cast=(), layer=0):
    b, t, _ = pbf3.shape
    w = n_heads * LANES
    n_chunks = t // chunk
    n_steps = (b // bpb) * n_chunks
    tabs = _retention_tables(n_heads, chunk, LANES)
    has_state = state is not None
    const3 = lambda bi, c: (0, 0, 0)
    cast_in, cast_out, cast_shape = _cast_specs(cast, layer, n_steps,
                                                lambda bi, c: bi * n_chunks + c)
    in_specs = [pl.BlockSpec((bpb, chunk, w), lambda bi, c: (bi, c, _BF_QR)),
                pl.BlockSpec((bpb, chunk, w), lambda bi, c: (bi, c, _BF_KR)),
                pl.BlockSpec((bpb, chunk, w), lambda bi, c: (bi, c, _BF_VR)),
                pl.BlockSpec((bpb, chunk, w), lambda bi, c: (bi, c, _GATE_GR)),
                pl.BlockSpec((n_heads, chunk, chunk), const3),
                pl.BlockSpec((n_heads, chunk, LANES), const3),
                pl.BlockSpec((n_heads, chunk, LANES), const3),
                pl.BlockSpec((n_heads, 1, LANES), const3)]
    args = [pbf3, pbf3, pbf3, gates3, *tabs]
    state_spec = pl.BlockSpec((bpb, n_heads, LANES, LANES), lambda bi, c: (bi, 0, 0, 0))
    if has_state:
        in_specs.append(state_spec)
        args.append(state)
    return pl.pallas_call(
        functools.partial(_ret_kernel, n_heads=n_heads, has_state=has_state, bpb=bpb,
                          n_cast=len(cast)),
        grid=(b // bpb, n_chunks),
        in_specs=in_specs + cast_in,
        out_specs=[pl.BlockSpec((bpb, chunk, w), lambda bi, c: (bi, c, 0)), state_spec] + cast_out,
        out_shape=[jax.ShapeDtypeStruct((b, t, w), BF16),
                   jax.ShapeDtypeStruct((b, n_heads, LANES, LANES), F32)] + cast_shape,
        scratch_shapes=[pltpu.VMEM((bpb, n_heads, LANES, LANES), F32)],
        compiler_params=_params("arbitrary", "arbitrary"),
        name="retention",
    )(*args, *cast)


def _merge_kernel(oa_ref, ob_ref, ga_ref, gb_ref, x_ref, g1_ref, wpa_ref, wpb_ref, wo_ref,
                  lng_ref, lnb_ref, *rest, alpha, n_cast):
    o_ref = rest[n_cast]
    for src, dst in zip(rest[:n_cast], rest[n_cast + 1:]):
        dst[...] = src[0].astype(BF16)
    a = jnp.dot(oa_ref[...], wpa_ref[...], preferred_element_type=F32)
    b = jnp.dot(ob_ref[...], wpb_ref[...], preferred_element_type=F32)
    merged = jax.nn.sigmoid(ga_ref[...]) * a + jax.nn.sigmoid(gb_ref[...]) * b
    y = jnp.dot(merged.astype(BF16), wo_ref[...], preferred_element_type=F32)
    x = x_ref[...]
    r = alpha * x + g1_ref[...] * y.reshape(x.shape)
    o_ref[...] = _layer_norm(r, lng_ref[...], lnb_ref[...])


def _row_tiling(x3, mod_group0, tm, n_grid_axes):
    g, r, d = x3.shape
    if r >= tm:
        tps = r // tm
        gb = 1
        if n_grid_axes == 1:
            x_map = lambda i: (i // tps, i % tps, 0)
            mod_idx = lambda k: (lambda i: (mod_group0 + i // tps, 0, k))
        else:
            x_map = lambda i, j: (i // tps, i % tps, 0)
            mod_idx = lambda k: (lambda i, j: (mod_group0 + i // tps, 0, k))
        x_spec = pl.BlockSpec((1, tm, d), x_map)
    else:
        tps = 1
        gb = tm // r
        if n_grid_axes == 1:
            x_map = lambda i: (i, 0, 0)
            mod_idx = lambda k: (lambda i: (mod_group0 // gb + i, 0, k))
        else:
            x_map = lambda i, j: (i, 0, 0)
            mod_idx = lambda k: (lambda i, j: (mod_group0 // gb + i, 0, k))
        x_spec = pl.BlockSpec((gb, r, d), x_map)
    mod_spec = lambda k: pl.BlockSpec((gb, 1, d), mod_idx(k))
    return x_spec, mod_spec, tps, gb


def _merge(oa2, ob2, gates, x3, mod3, mod_group0, wpa, wpb, wo, lng, lnb, *, tm, alpha,
           cast=(), layer=0):
    g, r, d = x3.shape
    m = g * r
    wa = oa2.shape[1]
    x_spec, mod_spec, _, _ = _row_tiling(x3, mod_group0, tm, 1)
    const = lambda i: (0, 0)
    cast_in, cast_out, cast_shape = _cast_specs(cast, layer, m // tm, lambda i: i)
    out = pl.pallas_call(
        functools.partial(_merge_kernel, alpha=alpha, n_cast=len(cast)),
        grid=(m // tm,),
        in_specs=[pl.BlockSpec((tm, wa), lambda i: (i, 0)),
                  pl.BlockSpec((tm, wa), lambda i: (i, 0)),
                  pl.BlockSpec((tm, d), lambda i: (i, _GATE_GA // 2)),
                  pl.BlockSpec((tm, d), lambda i: (i, _GATE_GB // 2)),
                  x_spec,
                  mod_spec(2),
                  pl.BlockSpec(wpa.shape, const, pipeline_mode=pl.Buffered(1)),
                  pl.BlockSpec(wpb.shape, const, pipeline_mode=pl.Buffered(1)),
                  pl.BlockSpec(wo.shape, const, pipeline_mode=pl.Buffered(1)),
                  pl.BlockSpec((1, d), const),
                  pl.BlockSpec((1, d), const)] + cast_in,
        out_specs=[x_spec] + cast_out,
        out_shape=[jax.ShapeDtypeStruct(x3.shape, F32)] + cast_shape,
        compiler_params=_params("arbitrary"),
        name="merge_outproj",
    )(oa2, ob2, gates, gates, x3, mod3, wpa, wpb, wo, lng, lnb, *cast)
    return out if cast else out[0]


def _ffn_kernel(*refs, tm, tiles_per_seq, alpha, with_state, conv_w):
    if with_state:
        (x_hbm, sc_ref, sh_ref, g2_ref, wa_ref, wb_ref, wd_ref, cw_ref, cb_ref, lng_ref, lnb_ref,
         st_ref, o_ref, tail_ref, u_sc, xbuf, xsem, abuf) = refs
    else:
        (x_hbm, sc_ref, sh_ref, g2_ref, wa_ref, wb_ref, wd_ref, cw_ref, cb_ref, lng_ref, lnb_ref,
         o_ref, tail_ref, u_sc, xbuf, xsem, abuf, carry_sc) = refs
    i = pl.program_id(0)
    j = pl.program_id(1)
    hist = conv_w - 1
    x_copy = functools.partial(_x_tile_copy, x_hbm, xbuf, xsem, tm=tm,
                               tiles_per_seq=tiles_per_seq)

    @pl.when(j == 0)
    def _():
        @pl.when(i == 0)
        def _():
            x_copy(0).start()

        x_copy(i).wait()
        x = xbuf[...]
        u_sc[...] = (x * (1.0 + sc_ref[...]) + sh_ref[...]).reshape(tm, x.shape[-1]).astype(BF16)
        o_ref[...] = alpha * x

        @pl.when(i + 1 < pl.num_programs(0))
        def _():
            x_copy(i + 1).start()

    u = u_sc[...]
    a = jnp.dot(u, wa_ref[...], preferred_element_type=F32)
    b = jnp.dot(u, wb_ref[...], preferred_element_type=F32)
    tf = a.shape[-1]
    cw = cw_ref[...]
    if with_state:
        gb = tm // SUBLANES
        a3 = a.reshape(gb, SUBLANES, tf)
        abuf[:, SUBLANES:2 * SUBLANES, :] = a3
        abuf[:, SUBLANES - hist:SUBLANES, :] = st_ref[...]
        shifted = [abuf[:, SUBLANES - hist + w:2 * SUBLANES - hist + w, :].reshape(tm, tf)
                   for w in range(hist)]
        tail_ref[...] = a3[:, SUBLANES - hist:, :]
    else:
        first = (i % tiles_per_seq) == 0
        abuf[0:SUBLANES, :] = jnp.where(first, 0.0, carry_sc[j])
        abuf[SUBLANES:SUBLANES + tm, :] = a
        shifted = [abuf[SUBLANES - hist + w:SUBLANES - hist + w + tm, :] for w in range(hist)]
        carry_sc[j] = a[tm - SUBLANES:, :]
        tail_ref[0] = a[tm - hist:, :]
    conv = a * cw[hist:hist + 1, :]
    for w in range(hist):
        conv = conv + shifted[w] * cw[w:w + 1, :]
    conv = conv + cb_ref[...]
    hg = (jax.nn.gelu(conv) * b).astype(BF16)
    y = jnp.dot(hg, wd_ref[...], preferred_element_type=F32).reshape(o_ref.shape)
    o_ref[...] += g2_ref[...] * y

    @pl.when(j == pl.num_programs(1) - 1)
    def _():
        o_ref[...] = _layer_norm(o_ref[...], lng_ref[...], lnb_ref[...])


def _ffn(x3, mod3, mod_group0, wup, wdown, conv_w, conv_b, lng, lnb, conv_state,
         *, tm, tf, alpha):
    g, r, d = x3.shape
    m = g * r
    dff = wdown.shape[0]
    nj = dff // tf
    cwid = conv_w.shape[0]
    hist = cwid - 1
    x_spec, mod_spec, tps, gb = _row_tiling(x3, mod_group0, tm, 2)
    with_state = conv_state is not None
    const = lambda i, j: (0, 0)
    in_specs = [pl.BlockSpec(memory_space=pl.ANY),
                mod_spec(4), mod_spec(3), mod_spec(5),
                pl.BlockSpec((d, tf), lambda i, j: (0, j)),
                pl.BlockSpec((d, tf), lambda i, j: (0, nj + j)),
                pl.BlockSpec((tf, d), lambda i, j: (j, 0)),
                pl.BlockSpec((cwid, tf), lambda i, j: (0, j)),
                pl.BlockSpec((1, tf), lambda i, j: (0, j)),
                pl.BlockSpec((1, d), const),
                pl.BlockSpec((1, d), const)]
    args = [x3, mod3, mod3, mod3, wup, wup, wdown, conv_w, conv_b.reshape(1, dff), lng, lnb]
    scratch = [pltpu.VMEM((tm, d), BF16), pltpu.VMEM(x_spec.block_shape, F32),
               pltpu.SemaphoreType.DMA(())]
    if with_state:
        assert r == SUBLANES
        in_specs.append(pl.BlockSpec((gb, hist, tf), lambda i, j: (i, 0, j)))
        args.append(conv_state)
        tail_spec = pl.BlockSpec((gb, hist, tf), lambda i, j: (i, 0, j))
        scratch.append(pltpu.VMEM((gb, 2 * SUBLANES, tf), F32))
    else:
        tail_spec = pl.BlockSpec((1, hist, tf), lambda i, j: (i, 0, j))
        scratch += [pltpu.VMEM((tm + SUBLANES, tf), F32), pltpu.VMEM((nj, SUBLANES, tf), F32)]
    n_tails = g if with_state else m // tm
    kern = functools.partial(_ffn_kernel, tm=tm, tiles_per_seq=tps, alpha=alpha,
                             with_state=with_state, conv_w=cwid)
    y, tails = pl.pallas_call(
        kern,
        grid=(m // tm, nj),
        in_specs=in_specs,
        out_specs=[x_spec, tail_spec],
        out_shape=[jax.ShapeDtypeStruct(x3.shape, F32),
                   jax.ShapeDtypeStruct((n_tails, hist, dff), F32)],
        scratch_shapes=scratch,
        compiler_params=_params("arbitrary", "arbitrary"),
        name="convffn",
    )(*args)
    if not with_state:
        tails = tails.reshape(g, tps, hist, dff)[:, tps - 1]
    return y, tails


def _rope_tables(pos, half):
    inv = ROPE_BASE ** (-jnp.arange(half, dtype=F32) / half)
    ang = pos.astype(F32)[:, None] * inv[None, :]
    cos, sin = jnp.cos(ang), jnp.sin(ang)
    return jnp.concatenate([cos, cos], axis=-1), jnp.concatenate([-sin, sin], axis=-1)


def kernel(x_prompt, x_sample, cache_k, cache_v, cache_logf, state_ret, state_conv, page_table,
           c_prompt, c_sample, w_ada, b_ada, w_in, b_f, w_pa, w_pb, w_o, ln1_g, ln1_b,
           w_up, conv_w, conv_b, w_down, ln2_g, ln2_b):
    depth = w_ada.shape[0]
    b, t, d = x_prompt.shape
    bs, ts, _ = x_sample.shape
    n_pool, page_size, n_heads, dh = cache_k.shape[1:]
    n_pages = page_table.shape[1]
    past = n_pages * page_size
    wa = n_heads * dh
    dff = w_down.shape[1]
    alpha = (2.0 * depth) ** 0.25
    assert wa == 1024 and dh == LANES and ts == SUBLANES and n_pages % SUBLANES == 0
    assert w_in.shape[2] == 7 * wa + n_heads + 2 * d and d == 2 * wa

    tm_p, tm_s = 1024, bs * ts
    tq = 512
    ret_chunk = 128 if t % 128 == 0 else t

    cos_p, sin_p = _rope_tables(jnp.arange(t), dh // 2)
    cos_s, sin_s = _rope_tables(past + jnp.arange(ts), dh // 2)
    cos_s, sin_s = jnp.tile(cos_s, (tm_s // ts, 1)), jnp.tile(sin_s, (tm_s // ts, 1))

    n_mod = -(-(bs + b) // 16) * 16
    c_all = jnp.concatenate([c_sample, c_prompt, jnp.zeros((n_mod - bs - b, d), F32)], axis=0)

    w_in_t = jnp.swapaxes(w_in, 1, 2)

    hp, hs = x_prompt, x_sample
    outs = {k: [] for k in ("kp", "vp", "lp", "rp", "cp", "ks", "vs", "ls", "rs", "cs")}
    for l in range(depth):
        mod3 = _ada(c_all, w_ada[l], b_ada[l]).reshape(n_mod, 1, 6 * d)

        fa0 = 3 * wa
        w_main = _repack_w_in(w_in_t, l, fa0, n_heads)
        wf_t = jnp.zeros((16, d), F32).at[:n_heads].set(w_in_t[l, fa0:fa0 + n_heads, :])
        bf_col = jnp.zeros((16, 1), F32).at[:n_heads, 0].set(b_f[l].astype(F32))
        lng1, lnb1 = ln1_g[l].reshape(1, d), ln1_b[l].reshape(1, d)
        lng2, lnb2 = ln2_g[l].reshape(1, d), ln2_b[l].reshape(1, d)

        sbf, sk, sv, sg, lfs_t, _ = _inproj(hs, mod3, 0, w_main, wf_t, bf_col, cos_s, sin_s,
                                            bf_col, tm=tm_s, seq_tiled=False)
        pbf, pk, pv, pg, lf_t, c_t = _inproj(hp, mod3, bs, w_main, wf_t, bf_col, cos_p, sin_p,
                                             lfs_t, tm=tm_p, seq_tiled=True)
        pbf3 = pbf.reshape(b, t, -1)
        nt = t // tm_p
        ob, ret_p, wpa, wpb, wo = _retention(
            pbf3, pg.reshape(b, t, -1), None, n_heads=n_heads, chunk=ret_chunk, bpb=4,
            cast=(w_pa, w_pb, w_o), layer=l)
        c4 = (c_t.reshape(b, nt, 16, tm_p // tq, tq).transpose(0, 1, 3, 2, 4)
              .reshape(b, t // tq, 16, tq))
        rows_q = ts * n_heads
        q3 = sbf[:, _BF_QA * wa:(_BF_QA + 1) * wa].reshape(bs, rows_q, dh)
        k_new = sk.reshape(bs, ts, n_heads, dh)
        v_new = sv.reshape(bs, ts, n_heads, dh)
        lf_s = lfs_t[0, :n_heads, :].T.reshape(bs, ts, n_heads)
        lfn3 = jnp.pad(lf_s.reshape(bs, 1, rows_q), ((0, 0), (0, 0), (0, LANES - rows_q)))
        ck3 = cache_k[l].reshape(n_pool, page_size * n_heads, dh)
        cv3 = cache_v[l].reshape(n_pool, page_size * n_heads, dh)
        clf3 = cache_logf[l].astype(F32).reshape(n_pool, 1, page_size * n_heads)
        oa, oa_s = _fox_attention(pbf3, pk.reshape(b, t, wa), pv.reshape(b, t, wa), c4,
                                  page_table, q3, k_new.reshape(bs, rows_q, dh),
                                  v_new.reshape(bs, rows_q, dh), lfn3, ck3, cv3, clf3,
                                  n_heads=n_heads, tq=tq, hpb=2, n_new=ts)

        x1, wup, wdown = _merge(oa.reshape(b * t, wa), ob.reshape(b * t, wa), pg, hp, mod3, bs,
                                wpa, wpb, wo, lng1, lnb1, tm=256, alpha=alpha,
                                cast=(w_up, w_down), layer=l)
        hp, conv_p = _ffn(x1, mod3, bs, wup, wdown, conv_w[l], conv_b[l], lng2, lnb2,
                          None, tm=1024, tf=512, alpha=alpha)
        outs["kp"].append(pk.reshape(b, t, n_heads, dh))
        outs["vp"].append(pv.reshape(b, t, n_heads, dh))
        lf_p = lf_t[:, :n_heads, :].reshape(b, nt, n_heads, tm_p)
        outs["lp"].append(lf_p.transpose(0, 1, 3, 2).reshape(b, t, n_heads))
        outs["rp"].append(ret_p)
        outs["cp"].append(conv_p)

        sbf3 = sbf.reshape(bs, ts, -1)
        ob_s, ret_s = _retention(sbf3, sg.reshape(bs, ts, -1), state_ret[l], n_heads=n_heads,
                                 chunk=ts, bpb=8)
        x1s = _merge(oa_s.reshape(bs * ts, wa), ob_s.reshape(bs * ts, wa), sg, hs, mod3, 0,
                     wpa, wpb, wo, lng1, lnb1, tm=256, alpha=alpha)
        hs, conv_s = _ffn(x1s, mod3, 0, wup, wdown, conv_w[l], conv_b[l], lng2, lnb2,
                          state_conv[l], tm=1024, tf=512, alpha=alpha)
        outs["ks"].append(k_new)
        outs["vs"].append(v_new)
        outs["ls"].append(lf_s)
        outs["rs"].append(ret_s)
        outs["cs"].append(conv_s)

    st = lambda name: jnp.stack(outs[name])
    return (hp, hs, st("kp"), st("vp"), st("lp"), st("rp"), st("cp"),
            st("ks"), st("vs"), st("ls"), st("rs"), st("cs"))
```

```python
import functools
import math

import jax
import jax.numpy as jnp
from jax import lax
from jax.experimental import pallas as pl
from jax.experimental.pallas import tpu as pltpu

F32 = jnp.float32
BF16 = jnp.bfloat16

LANES = 128
SUBLANES = 8
VMEM_LIMIT = 56 * 1024 * 1024

ROPE_BASE = 10000.0
LN_EPS = 1e-5
GN_EPS = 1e-5
NEG_BIG = -1e30
LOG2E = math.log2(math.e)

_NT = (((1,), (1,)), ((), ()))
_TN = (((0,), (0,)), ((), ()))


def _params(*sem):
    return pltpu.CompilerParams(dimension_semantics=sem, vmem_limit_bytes=VMEM_LIMIT)


def _layer_norm(r, g, b):
    mu = jnp.mean(r, axis=-1, keepdims=True)
    d = r - mu
    var = jnp.mean(d * d, axis=-1, keepdims=True)
    return d * lax.rsqrt(var + LN_EPS) * g + b


def _ada_kernel(c_ref, w_ref, b_ref, o_ref):
    c = c_ref[...]
    s = (c * jax.nn.sigmoid(c)).astype(BF16)
    o_ref[...] = jnp.dot(s, w_ref[...].astype(BF16), preferred_element_type=F32) + b_ref[...]


def _ada(c_all, w_ada, b_ada):
    rows, d = c_all.shape
    n = w_ada.shape[1]
    tn = 512
    return pl.pallas_call(
        _ada_kernel,
        grid=(n // tn,),
        in_specs=[pl.BlockSpec((rows, d), lambda j: (0, 0)),
                  pl.BlockSpec((d, tn), lambda j: (0, j)),
                  pl.BlockSpec((1, tn), lambda j: (0, j))],
        out_specs=pl.BlockSpec((rows, tn), lambda j: (0, j)),
        out_shape=jax.ShapeDtypeStruct((rows, n), F32),
        compiler_params=_params("arbitrary"),
        name="ada_mod",
    )(c_all, w_ada, b_ada.reshape(1, n))


(_G_QA, _G_KA, _G_VA, _G_QR, _G_KR, _G_VR, _G_GA0, _G_GA1, _G_GB0, _G_GB1, _G_GR) = range(11)
_N_GROUPS = 11
_W_BLOCK_OF_GR = 6
_BF_QA, _BF_QR, _BF_KR, _BF_VR = 0, 1, 2, 3
_GATE_GA, _GATE_GB, _GATE_GR = 0, 2, 4


def _w_block(j):
    return jnp.where(j < _G_GA0, j, jnp.where(j < _G_GR, j + 1, _W_BLOCK_OF_GR))


def _bf_block(j):
    return jnp.clip(j - (_G_QR - _BF_QR), _BF_QA, _BF_VR)


def _gate_block(j):
    return jnp.clip(j - _G_GA0, _GATE_GA, _GATE_GR)


def _repack_kernel(wt_ref, o_ref):
    o_ref[...] = wt_ref[0].T.astype(BF16)


def _repack_w_in(w_in_t, layer, fa0, n_f):
    _, n, d = w_in_t.shape
    tc = 1024
    assert fa0 % tc == 0 and (n - n_f) % tc == 0

    def rows(j):
        return pl.multiple_of(jnp.where(j < fa0 // tc, j * tc, j * tc + n_f), SUBLANES)

    return pl.pallas_call(
        _repack_kernel,
        grid=((n - n_f) // tc,),
        in_specs=[pl.BlockSpec((pl.Element(1), pl.Element(tc), pl.Element(d)),
                               lambda j: (layer, rows(j), 0))],
        out_specs=pl.BlockSpec((d, tc), lambda j: (0, j)),
        out_shape=jax.ShapeDtypeStruct((d, n - n_f), BF16),
        compiler_params=_params("arbitrary"),
        name="repack_w_in",
    )(w_in_t)


def _lane_cumsum(x, carry):
    rows, width = x.shape
    lane = lax.broadcasted_iota(jnp.int32, (rows, LANES), 1)
    out = []
    for c in range(width // LANES):
        v = x[:, c * LANES:(c + 1) * LANES]
        s = 1
        while s < LANES:
            v = v + jnp.where(lane >= s, pltpu.roll(v, s, axis=1), 0.0)
            s *= 2
        out.append(v + carry)
        carry = carry + jnp.broadcast_to(v[:, LANES - 1:LANES], (rows, LANES))
    return jnp.concatenate(out, axis=1), carry


def _x_tile_copy(x_hbm, xbuf, sem, tile, *, tm, tiles_per_seq):
    if tiles_per_seq > 1:
        src = x_hbm.at[pl.ds(tile // tiles_per_seq, 1), pl.ds((tile % tiles_per_seq) * tm, tm)]
    else:
        gb = xbuf.shape[0]
        src = x_hbm.at[pl.ds(tile * gb, gb)]
    return pltpu.make_async_copy(src, xbuf, sem)


def _head_major_copies(src_ref, dst_hbm, sem, tile, *, tm, n_heads):
    return [pltpu.make_async_copy(src_ref.at[:, h * LANES:(h + 1) * LANES],
                                  dst_hbm.at[pl.ds(tile * tm, tm), h, :], sem)
            for h in range(n_heads)]


def _inproj_kernel(x_hbm, sc_ref, sh_ref, w_ref, wf_ref, bf_ref, cos_ref, sin_ref,
                   obf_ref, ok_ref, ov_ref, og_ref, lf_ref, c_ref, kh_hbm, vh_hbm,
                   u_sc, carry_sc, xbuf, xsem, osem,
                   *, tm, tiles_per_seq, qa_scale, kr_scale, n_heads):
    i = pl.program_id(0)
    j = pl.program_id(1)
    x_copy = functools.partial(_x_tile_copy, x_hbm, xbuf, xsem, tm=tm,
                               tiles_per_seq=tiles_per_seq)

    @pl.when(j == 0)
    def _():
        @pl.when(i == 0)
        def _():
            x_copy(0).start()

        x_copy(i).wait()
        x = xbuf[...]
        u = (x * (1.0 + sc_ref[...]) + sh_ref[...]).reshape(tm, x.shape[-1]).astype(BF16)
        u_sc[...] = u

        @pl.when(i + 1 < pl.num_programs(0))
        def _():
            x_copy(i + 1).start()

        z = lax.dot_general(wf_ref[...].astype(BF16), u, _NT,
                            preferred_element_type=F32) + bf_ref[...]
        lf = jnp.minimum(z, 0.0) - jnp.log1p(jnp.exp(-jnp.abs(z)))
        lf_ref[0] = lf
        first = (i % tiles_per_seq) == 0
        prev = jnp.where(first, 0.0, carry_sc[...])
        c, last = _lane_cumsum(lf, prev)
        c_ref[0] = c * LOG2E
        carry_sc[...] = last

    def project():
        return jnp.dot(u_sc[...], w_ref[...], preferred_element_type=F32)

    @pl.when(j == _G_QA)
    def _():
        obf_ref[...] = (project() * qa_scale).astype(BF16)

    def rope(scale):
        acc = project()
        cos = cos_ref[...]
        sin = sin_ref[...]
        for h in range(n_heads):
            sl = slice(h * LANES, (h + 1) * LANES)
            a = acc[:, sl]
            r = a * cos + pltpu.roll(a, LANES // 2, axis=1) * sin
            if scale is not None:
                r = r * scale
            obf_ref[:, sl] = r.astype(BF16)

    @pl.when(j == _G_QR)
    def _():
        rope(None)

    @pl.when(j == _G_KR)
    def _():
        rope(kr_scale)

    @pl.when(j == _G_VR)
    def _():
        obf_ref[...] = project().astype(BF16)

    k_out = functools.partial(_head_major_copies, ok_ref, kh_hbm, osem.at[0], tm=tm, n_heads=n_heads)
    v_out = functools.partial(_head_major_copies, ov_ref, vh_hbm, osem.at[1], tm=tm, n_heads=n_heads)

    def emit(dst_ref, copies):
        @pl.when(i > 0)
        def _():
            for cp in copies(i - 1):
                cp.wait()

        dst_ref[...] = project()
        for cp in copies(i):
            cp.start()

    @pl.when(j == _G_KA)
    def _():
        emit(ok_ref, k_out)

    @pl.when(j == _G_VA)
    def _():
        emit(ov_ref, v_out)

    @pl.when((i == pl.num_programs(0) - 1) & (j == pl.num_programs(1) - 1))
    def _():
        for cp in k_out(i) + v_out(i):
            cp.wait()

    @pl.when(j >= _G_GA0)
    def _():
        og_ref[...] = project()


def _inproj(x3, mod3, mod_group0, w_main, wf_t, bf_col, cos_t, sin_t, *, tm, seq_tiled):
    g, r, d = x3.shape
    m = g * r
    n_tiles = m // tm
    if seq_tiled:
        tps = r // tm
        gb = 1
        x_block = (1, tm, d)
        mod_idx = lambda k: (lambda i, j: (mod_group0 + i // tps, 0, k))
        tab_spec = pl.BlockSpec((tm, LANES), lambda i, j: (i % tps, 0))
    else:
        tps = 1
        gb = tm // r
        x_block = (gb, r, d)
        mod_idx = lambda k: (lambda i, j: (mod_group0 // gb + i, 0, k))
        tab_spec = pl.BlockSpec((tm, LANES), lambda i, j: (0, 0))
    n_bf = _BF_VR + 1
    kern = functools.partial(_inproj_kernel, tm=tm, tiles_per_seq=tps,
                             qa_scale=LANES ** -0.5 * LOG2E, kr_scale=LANES ** -0.5,
                             n_heads=1024 // LANES)
    return pl.pallas_call(
        kern,
        grid=(n_tiles, _N_GROUPS),
        in_specs=[pl.BlockSpec(memory_space=pl.ANY),
                  pl.BlockSpec((gb, 1, d), mod_idx(1)),
                  pl.BlockSpec((gb, 1, d), mod_idx(0)),
                  pl.BlockSpec((d, 1024), lambda i, j: (0, _w_block(j))),
                  pl.BlockSpec((16, d), lambda i, j: (0, 0)),
                  pl.BlockSpec((16, 1), lambda i, j: (0, 0)),
                  tab_spec, tab_spec],
        out_specs=[pl.BlockSpec((tm, 1024), lambda i, j: (i, _bf_block(j))),
                   pl.BlockSpec((tm, 1024), lambda i, j: (i, 0)),
                   pl.BlockSpec((tm, 1024), lambda i, j: (i, 0)),
                   pl.BlockSpec((tm, 1024), lambda i, j: (i, _gate_block(j))),
                   pl.BlockSpec((1, 16, tm), lambda i, j: (i, 0, 0)),
                   pl.BlockSpec((1, 16, tm), lambda i, j: (i, 0, 0)),
                   pl.BlockSpec(memory_space=pl.ANY),
                   pl.BlockSpec(memory_space=pl.ANY)],
        out_shape=[jax.ShapeDtypeStruct((m, 1024 * n_bf), BF16),
                   jax.ShapeDtypeStruct((m, 1024), F32),
                   jax.ShapeDtypeStruct((m, 1024), F32),
                   jax.ShapeDtypeStruct((m, 1024 * (_GATE_GR + 1)), F32),
                   jax.ShapeDtypeStruct((n_tiles, 16, tm), F32),
                   jax.ShapeDtypeStruct((n_tiles, 16, tm), F32),
                   jax.ShapeDtypeStruct((m, 1024 // LANES, LANES), F32),
                   jax.ShapeDtypeStruct((m, 1024 // LANES, LANES), F32)],
        scratch_shapes=[pltpu.VMEM((tm, d), BF16), pltpu.VMEM((16, LANES), F32),
                        pltpu.VMEM(x_block, F32), pltpu.SemaphoreType.DMA(()),
                        pltpu.SemaphoreType.DMA((2,))],
        compiler_params=_params("arbitrary", "arbitrary"),
        name="inproj",
    )(x3, mod3, mod3, w_main, wf_t, bf_col, cos_t, sin_t)


def _fox_prefill_body(q_ref, k_ref, v_ref, c_ref, o_ref, hg, qi, *, tq, hpb):
    dh = LANES

    def chunk(kc, carry, diagonal):
        start = pl.multiple_of(kc * tq, tq)
        out = []
        for hh in range(hpb):
            m, l, acc = carry[hh]
            sl = slice(hh * dh, (hh + 1) * dh)
            kk = k_ref[0, pl.ds(start, tq), sl].astype(BF16)
            vv = v_ref[0, pl.ds(start, tq), sl].astype(BF16)
            s = lax.dot_general(q_ref[0, :, sl], kk, _NT, preferred_element_type=F32)
            s = s - c_ref[0, kc, pl.ds(hg * hpb + hh, 1), :]
            if diagonal:
                row = lax.broadcasted_iota(jnp.int32, (tq, tq), 0)
                col = lax.broadcasted_iota(jnp.int32, (tq, tq), 1)
                s = jnp.where(row >= col, s, NEG_BIG)
            m_new = jnp.maximum(m, jnp.max(s, axis=-1, keepdims=True))
            alpha = jnp.exp2(m - m_new)
            p = jnp.exp2(s - m_new)
            l = alpha * l + jnp.sum(p, axis=-1, keepdims=True)
            acc = alpha * acc + jnp.dot(p.astype(BF16), vv, preferred_element_type=F32)
            out.append((m_new, l, acc))
        return tuple(out)

    init = tuple((jnp.full((tq, 1), NEG_BIG, F32), jnp.zeros((tq, 1), F32),
                  jnp.zeros((tq, dh), F32)) for _ in range(hpb))
    carry = lax.fori_loop(0, qi, lambda kc, c: chunk(kc, c, False), init)
    carry = chunk(qi, carry, True)
    for hh in range(hpb):
        _, l, acc = carry[hh]
        o_ref[0, :, hh * dh:(hh + 1) * dh] = (acc / l).astype(BF16)


def _periodic_tail(v, lane, n_heads):
    y = jnp.where(lane >= LANES - n_heads, v, 0.0)
    s = n_heads
    while s < LANES:
        y = y + pltpu.roll(y, LANES - s, axis=1)
        s *= 2
    return y


def _page_copies(pt_ref, ck_hbm, cv_hbm, clf_hbm, kbuf, vbuf, lfbuf, sems, seq, slot, n_pages):
    copies = []
    for p in range(n_pages):
        page = pt_ref[seq * n_pages + p]
        copies.append(pltpu.make_async_copy(ck_hbm.at[page], kbuf.at[slot, p], sems.at[slot, 0]))
        copies.append(pltpu.make_async_copy(cv_hbm.at[page], vbuf.at[slot, p], sems.at[slot, 1]))
        copies.append(pltpu.make_async_copy(clf_hbm.at[page], lfbuf.at[slot, p], sems.at[slot, 2]))
    return copies


def _fox_decode_issue(b, copies):
    @pl.when(b == 0)
    def _():
        for cp in copies(seq=0, slot=0):
            cp.start()

    @pl.when(b + 1 < pl.num_programs(0))
    def _():
        for cp in copies(seq=b + 1, slot=1 - b % 2):
            cp.start()


def _fox_decode_attend(b, copies, q_ref, kn_ref, vn_ref, lfn_ref, o_ref, kbuf, vbuf, lfbuf,
                       lf_sc, s_sc, *, n_pages, n_heads, n_new):
    slot = b % 2
    for cp in copies(seq=b, slot=slot):
        cp.wait()

    k_refs = [kbuf.at[slot, p] for p in range(n_pages)]
    v_refs = [vbuf.at[slot, p] for p in range(n_pages)]
    lf_refs = [lfbuf.at[slot, p] for p in range(n_pages)]

    page_w = lfbuf.shape[-1]
    ppr = n_pages // SUBLANES
    n_chunks = ppr * page_w // LANES
    rows_q = n_new * n_heads

    for p in range(n_pages):
        r, part = divmod(p, ppr)
        lf_sc[r:r + 1, part * page_w:(part + 1) * page_w] = lf_refs[p][...] * LOG2E
    lane = lax.broadcasted_iota(jnp.int32, (SUBLANES, LANES), 1)
    sub = lax.broadcasted_iota(jnp.int32, (SUBLANES, LANES), 0)
    chunks = []
    carry = jnp.zeros((SUBLANES, LANES), F32)
    for c in range(n_chunks):
        v = lf_sc[:, c * LANES:(c + 1) * LANES]
        s = n_heads
        while s < LANES:
            v = v + jnp.where(lane >= s, pltpu.roll(v, s, axis=1), 0.0)
            s *= 2
        chunks.append(v + carry)
        carry = carry + _periodic_tail(v, lane, n_heads)
    inc = carry
    s = 1
    while s < SUBLANES:
        inc = inc + jnp.where(sub >= s, pltpu.roll(inc, s, axis=0), 0.0)
        s *= 2
    exc = jnp.where(sub >= 1, pltpu.roll(inc, 1, axis=0), 0.0)
    chunks = [v + exc for v in chunks]
    past_total = inc[SUBLANES - 1:SUBLANES, :]

    q = q_ref[0]
    dh = q.shape[-1]
    row_h = lax.broadcasted_iota(jnp.int32, (rows_q, page_w), 0) % n_heads
    col_h = lax.broadcasted_iota(jnp.int32, (rows_q, page_w), 1) % n_heads
    same_head = row_h == col_h

    cpp = page_w // LANES
    mx = jnp.full((rows_q, LANES), NEG_BIG, F32)
    for p in range(n_pages):
        r, part = divmod(p, ppr)
        ck = jnp.concatenate([chunks[part * cpp + c][r:r + 1, :] for c in range(cpp)], axis=1)
        kk = k_refs[p][...].astype(BF16)
        s = lax.dot_general(q, kk, _NT, preferred_element_type=F32)
        s = jnp.where(same_head, s - ck, NEG_BIG)
        s_sc[p] = s
        for c in range(cpp):
            mx = jnp.maximum(mx, s[:, c * LANES:(c + 1) * LANES])

    lane1 = lax.broadcasted_iota(jnp.int32, (1, LANES), 1)
    cn = lfn_ref[0] * LOG2E
    s = n_heads
    while s < rows_q:
        cn = cn + jnp.where(lane1 >= s, pltpu.roll(cn, s, axis=1), 0.0)
        s *= 2
    cn = cn + past_total
    s_new = lax.dot_general(q, kn_ref[0].astype(BF16), _NT, preferred_element_type=F32)
    row = lax.broadcasted_iota(jnp.int32, (rows_q, rows_q), 0)
    col = lax.broadcasted_iota(jnp.int32, (rows_q, rows_q), 1)
    ok = ((row % n_heads) == (col % n_heads)) & (col <= row)
    s_new = jnp.where(ok, s_new - cn[:, :rows_q], NEG_BIG)
    m = jnp.maximum(jnp.max(mx, axis=-1, keepdims=True), jnp.max(s_new, axis=-1, keepdims=True))

    lsum = jnp.zeros((rows_q, LANES), F32)
    acc = jnp.zeros((rows_q, dh), F32)
    for p in range(n_pages):
        pr = jnp.exp2(s_sc[p] - m)
        for c in range(cpp):
            lsum = lsum + pr[:, c * LANES:(c + 1) * LANES]
        acc = acc + jnp.dot(pr.astype(BF16), v_refs[p][...].astype(BF16),
                            preferred_element_type=F32)
    pr = jnp.exp2(s_new - m)
    l = jnp.sum(lsum, axis=-1, keepdims=True) + jnp.sum(pr, axis=-1, keepdims=True)
    acc = acc + jnp.dot(pr.astype(BF16), vn_ref[0].astype(BF16), preferred_element_type=F32)
    o_ref[0] = (acc / l).astype(BF16)


def _fox_kernel(pt_ref, qp_ref, kp_ref, vp_ref, cp_ref, q_ref, kn_ref, vn_ref, lfn_ref,
                ck_hbm, cv_hbm, clf_hbm, op_ref, o_ref, kbuf, vbuf, lfbuf, sems, lf_sc, s_sc,
                *, tq, hpb, n_hg, n_q, n_pages, n_heads, n_new):
    step = pl.program_id(0)
    copies = functools.partial(_page_copies, pt_ref, ck_hbm, cv_hbm, clf_hbm, kbuf, vbuf, lfbuf,
                               sems, n_pages=n_pages)
    _fox_decode_issue(step, copies)
    _fox_prefill_body(qp_ref, kp_ref, vp_ref, cp_ref, op_ref, (step // n_q) % n_hg, step % n_q,
                      tq=tq, hpb=hpb)
    _fox_decode_attend(step, copies, q_ref, kn_ref, vn_ref, lfn_ref, o_ref, kbuf, vbuf, lfbuf,
                       lf_sc, s_sc, n_pages=n_pages, n_heads=n_heads, n_new=n_new)


def _fox_attention(pbf3, k3, v3, c4, page_table, q3, kn3, vn3, lfn3, ck3, cv3, clf3,
                   *, n_heads, tq, hpb, n_new):
    b, t, _ = pbf3.shape
    w = hpb * LANES
    n_hg = n_heads // hpb
    n_q = t // tq
    bs, n_pages = page_table.shape
    rows_q, dh = q3.shape[1], q3.shape[2]
    page_rows = ck3.shape[1]
    page_w = clf3.shape[2]
    assert b * n_hg * n_q == bs, "prefill steps and decode sequences must pair up"

    def pre(col0):
        return lambda s, pt: (s // (n_hg * n_q), s % n_q, col0 + (s // n_q) % n_hg)

    def pre_kv(s, pt):
        return (s // (n_hg * n_q), 0, (s // n_q) % n_hg)

    def same(s, pt):
        return (s, 0, 0)

    hbm = pl.BlockSpec(memory_space=pl.ANY)
    grid_spec = pltpu.PrefetchScalarGridSpec(
        num_scalar_prefetch=1,
        grid=(bs,),
        in_specs=[pl.BlockSpec((1, tq, w), pre(_BF_QA * n_hg)),
                  pl.BlockSpec((1, t, w), pre_kv),
                  pl.BlockSpec((1, t, w), pre_kv),
                  pl.BlockSpec((1, n_q, 16, tq), lambda s, pt: (s // (n_hg * n_q), 0, 0, 0)),
                  pl.BlockSpec((1, rows_q, dh), same),
                  pl.BlockSpec((1, rows_q, dh), same),
                  pl.BlockSpec((1, rows_q, dh), same),
                  pl.BlockSpec((1, 1, LANES), same),
                  hbm, hbm, hbm],
        out_specs=[pl.BlockSpec((1, tq, w), pre(0)),
                   pl.BlockSpec((1, rows_q, dh), same)],
        scratch_shapes=[pltpu.VMEM((2, n_pages, page_rows, dh), F32),
                        pltpu.VMEM((2, n_pages, page_rows, dh), F32),
                        pltpu.VMEM((2, n_pages, 1, page_w), F32),
                        pltpu.SemaphoreType.DMA((2, 3)),
                        pltpu.VMEM((SUBLANES, n_pages // SUBLANES * page_w), F32),
                        pltpu.VMEM((n_pages, rows_q, page_w), F32)],
    )
    return pl.pallas_call(
        functools.partial(_fox_kernel, tq=tq, hpb=hpb, n_hg=n_hg, n_q=n_q, n_pages=n_pages,
                          n_heads=n_heads, n_new=n_new),
        grid_spec=grid_spec,
        out_shape=[jax.ShapeDtypeStruct((b, t, n_heads * LANES), BF16),
                   jax.ShapeDtypeStruct((bs, rows_q, dh), BF16)],
        compiler_params=_params("arbitrary"),
        name="fox_attention",
    )(page_table.reshape(-1), pbf3, k3, v3, c4, q3, kn3, vn3, lfn3, ck3, cv3, clf3)


def _ret_kernel(*refs, n_heads, has_state, bpb, n_cast):
    n_in = 8 + (1 if has_state else 0)
    cast_in = refs[n_in:n_in + n_cast]
    cast_out = refs[n_in + n_cast + 2:n_in + 2 * n_cast + 2]
    refs = refs[:n_in] + refs[n_in + n_cast:n_in + n_cast + 2] + refs[n_in + 2 * n_cast + 2:]
    for src, dst in zip(cast_in, cast_out):
        dst[...] = src[0].astype(BF16)
    if has_state:
        (q_ref, k_ref, v_ref, g_ref, dmat_ref, qdec_ref, kdec_ref, sdec_ref, s0_ref,
         o_ref, sout_ref, s_sc) = refs
    else:
        (q_ref, k_ref, v_ref, g_ref, dmat_ref, qdec_ref, kdec_ref, sdec_ref,
         o_ref, sout_ref, s_sc) = refs
    c = pl.program_id(1)

    @pl.when(c == 0)
    def _():
        if has_state:
            s_sc[...] = s0_ref[...]
        else:
            s_sc[...] = jnp.zeros_like(s_sc)

    for bb in range(bpb):
        for h in range(n_heads):
            sl = slice(h * LANES, (h + 1) * LANES)
            q = q_ref[bb, :, sl]
            k = k_ref[bb, :, sl]
            v = v_ref[bb, :, sl]
            st = s_sc[bb, h]
            att = lax.dot_general(q, k, _NT, preferred_element_type=F32) * dmat_ref[h]
            inner = jnp.dot(att.astype(BF16), v, preferred_element_type=F32)
            cross = jnp.dot(q, st.astype(BF16), preferred_element_type=F32) * qdec_ref[h]
            o = inner + cross
            kw = (k.astype(F32) * kdec_ref[h]).astype(BF16)
            s_sc[bb, h] = sdec_ref[h] * st + lax.dot_general(kw, v, _TN,
                                                            preferred_element_type=F32)
            mu = jnp.mean(o, axis=-1, keepdims=True)
            d = o - mu
            var = jnp.mean(d * d, axis=-1, keepdims=True)
            g = g_ref[bb, :, sl]
            o_ref[bb, :, sl] = (d * lax.rsqrt(var + GN_EPS)
                                * (g * jax.nn.sigmoid(g))).astype(BF16)

    @pl.when(c == pl.num_programs(1) - 1)
    def _():
        sout_ref[...] = s_sc[...]


def _retention_tables(n_heads, chunk, dk):
    lg = jnp.log(1.0 - jnp.exp2(-5.0 - jnp.arange(n_heads, dtype=F32)))
    idx = jnp.arange(chunk, dtype=F32)
    diff = idx[:, None] - idx[None, :]
    dmat = jnp.where(diff[None] >= 0, jnp.exp(diff[None] * lg[:, None, None]), 0.0)
    qdec = jnp.exp((idx + 1.0)[None, :] * lg[:, None])
    kdec = jnp.exp((chunk - 1.0 - idx)[None, :] * lg[:, None])
    sdec = jnp.exp(chunk * lg)
    bc = lambda a: jnp.broadcast_to(a[:, :, None], a.shape + (dk,))
    return dmat, bc(qdec), bc(kdec), jnp.broadcast_to(sdec[:, None, None], (n_heads, 1, dk))


def _retention(pbf3, gates3, state, *, n_heads, chunk, bpb, cast=(), layer=0):
    b, t, _ = pbf3.shape
    w = n_heads * LANES
    n_chunks = t // chunk
    n_steps = (b // bpb) * n_chunks
    tabs = _retention_tables(n_heads, chunk, LANES)
    has_state = state is not None
    const3 = lambda bi, c: (0, 0, 0)
    cast_in, cast_out, cast_shape = [], [], []
    for wt in cast:
        _, rows, cols = wt.shape
        rb = rows // n_steps
        assert rb * n_steps == rows and rb % 16 == 0, "row block must be whole bf16 tiles"
        cast_in.append(pl.BlockSpec((1, rb, cols), lambda bi, c: (layer, bi * n_chunks + c, 0)))
        cast_out.append(pl.BlockSpec((rb, cols), lambda bi, c: (bi * n_chunks + c, 0)))
        cast_shape.append(jax.ShapeDtypeStruct((rows, cols), BF16))
    in_specs = [pl.BlockSpec((bpb, chunk, w), lambda bi, c: (bi, c, _BF_QR)),
                pl.BlockSpec((bpb, chunk, w), lambda bi, c: (bi, c, _BF_KR)),
                pl.BlockSpec((bpb, chunk, w), lambda bi, c: (bi, c, _BF_VR)),
                pl.BlockSpec((bpb, chunk, w), lambda bi, c: (bi, c, _GATE_GR)),
                pl.BlockSpec((n_heads, chunk, chunk), const3),
                pl.BlockSpec((n_heads, chunk, LANES), const3),
                pl.BlockSpec((n_heads, chunk, LANES), const3),
                pl.BlockSpec((n_heads, 1, LANES), const3)]
    args = [pbf3, pbf3, pbf3, gates3, *tabs]
    state_spec = pl.BlockSpec((bpb, n_heads, LANES, LANES), lambda bi, c: (bi, 0, 0, 0))
    if has_state:
        in_specs.append(state_spec)
        args.append(state)
    return pl.pallas_call(
        functools.partial(_ret_kernel, n_heads=n_heads, has_state=has_state, bpb=bpb,
                          n_cast=len(cast)),
        grid=(b // bpb, n_chunks),
        in_specs=in_specs + cast_in,
        out_specs=[pl.BlockSpec((bpb, chunk, w), lambda bi, c: (bi, c, 0)), state_spec] + cast_out,
        out_shape=[jax.ShapeDtypeStruct((b, t, w), BF16),
                   jax.ShapeDtypeStruct((b, n_heads, LANES, LANES), F32)] + cast_shape,
        scratch_shapes=[pltpu.VMEM((bpb, n_heads, LANES, LANES), F32)],
        compiler_params=_params("arbitrary", "arbitrary"),
        name="retention",
    )(*args, *cast)


def _merge_kernel(oa_ref, ob_ref, ga_ref, gb_ref, x_ref, g1_ref, wpa_ref, wpb_ref, wo_ref,
                  lng_ref, lnb_ref, o_ref, *, alpha):
    a = jnp.dot(oa_ref[...], wpa_ref[...], preferred_element_type=F32)
    b = jnp.dot(ob_ref[...], wpb_ref[...], preferred_element_type=F32)
    merged = jax.nn.sigmoid(ga_ref[...]) * a + jax.nn.sigmoid(gb_ref[...]) * b
    y = jnp.dot(merged.astype(BF16), wo_ref[...], preferred_element_type=F32)
    x = x_ref[...]
    r = alpha * x + g1_ref[...] * y.reshape(x.shape)
    o_ref[...] = _layer_norm(r, lng_ref[...], lnb_ref[...])


def _row_tiling(x3, mod_group0, tm, n_grid_axes):
    g, r, d = x3.shape
    if r >= tm:
        tps = r // tm
        gb = 1
        if n_grid_axes == 1:
            x_map = lambda i: (i // tps, i % tps, 0)
            mod_idx = lambda k: (lambda i: (mod_group0 + i // tps, 0, k))
        else:
            x_map = lambda i, j: (i // tps, i % tps, 0)
            mod_idx = lambda k: (lambda i, j: (mod_group0 + i // tps, 0, k))
        x_spec = pl.BlockSpec((1, tm, d), x_map)
    else:
        tps = 1
        gb = tm // r
        if n_grid_axes == 1:
            x_map = lambda i: (i, 0, 0)
            mod_idx = lambda k: (lambda i: (mod_group0 // gb + i, 0, k))
        else:
            x_map = lambda i, j: (i, 0, 0)
            mod_idx = lambda k: (lambda i, j: (mod_group0 // gb + i, 0, k))
        x_spec = pl.BlockSpec((gb, r, d), x_map)
    mod_spec = lambda k: pl.BlockSpec((gb, 1, d), mod_idx(k))
    return x_spec, mod_spec, tps, gb


def _merge(oa2, ob2, gates, x3, mod3, mod_group0, wpa, wpb, wo, lng, lnb, *, tm, alpha):
    g, r, d = x3.shape
    m = g * r
    wa = oa2.shape[1]
    x_spec, mod_spec, _, _ = _row_tiling(x3, mod_group0, tm, 1)
    const = lambda i: (0, 0)
    return pl.pallas_call(
        functools.partial(_merge_kernel, alpha=alpha),
        grid=(m // tm,),
        in_specs=[pl.BlockSpec((tm, wa), lambda i: (i, 0)),
                  pl.BlockSpec((tm, wa), lambda i: (i, 0)),
                  pl.BlockSpec((tm, d), lambda i: (i, _GATE_GA // 2)),
                  pl.BlockSpec((tm, d), lambda i: (i, _GATE_GB // 2)),
                  x_spec,
                  mod_spec(2),
                  pl.BlockSpec(wpa.shape, const, pipeline_mode=pl.Buffered(1)),
                  pl.BlockSpec(wpb.shape, const, pipeline_mode=pl.Buffered(1)),
                  pl.BlockSpec(wo.shape, const, pipeline_mode=pl.Buffered(1)),
                  pl.BlockSpec((1, d), const),
                  pl.BlockSpec((1, d), const)],
        out_specs=x_spec,
        out_shape=jax.ShapeDtypeStruct(x3.shape, F32),
        compiler_params=_params("arbitrary"),
        name="merge_outproj",
    )(oa2, ob2, gates, gates, x3, mod3, wpa, wpb, wo, lng, lnb)


def _ffn_kernel(*refs, tm, tiles_per_seq, alpha, with_state, conv_w):
    if with_state:
        (x_hbm, sc_ref, sh_ref, g2_ref, wa_ref, wb_ref, wd_ref, cw_ref, cb_ref, lng_ref, lnb_ref,
         st_ref, o_ref, tail_ref, u_sc, xbuf, xsem, abuf) = refs
    else:
        (x_hbm, sc_ref, sh_ref, g2_ref, wa_ref, wb_ref, wd_ref, cw_ref, cb_ref, lng_ref, lnb_ref,
         o_ref, tail_ref, u_sc, xbuf, xsem, abuf, carry_sc) = refs
    i = pl.program_id(0)
    j = pl.program_id(1)
    hist = conv_w - 1
    x_copy = functools.partial(_x_tile_copy, x_hbm, xbuf, xsem, tm=tm,
                               tiles_per_seq=tiles_per_seq)

    @pl.when(j == 0)
    def _():
        @pl.when(i == 0)
        def _():
            x_copy(0).start()

        x_copy(i).wait()
        x = xbuf[...]
        u_sc[...] = (x * (1.0 + sc_ref[...]) + sh_ref[...]).reshape(tm, x.shape[-1]).astype(BF16)
        o_ref[...] = alpha * x

        @pl.when(i + 1 < pl.num_programs(0))
        def _():
            x_copy(i + 1).start()

    u = u_sc[...]
    a = jnp.dot(u, wa_ref[...], preferred_element_type=F32)
    b = jnp.dot(u, wb_ref[...], preferred_element_type=F32)
    tf = a.shape[-1]
    cw = cw_ref[...]
    if with_state:
        gb = tm // SUBLANES
        a3 = a.reshape(gb, SUBLANES, tf)
        abuf[:, SUBLANES:2 * SUBLANES, :] = a3
        abuf[:, SUBLANES - hist:SUBLANES, :] = st_ref[...]
        shifted = [abuf[:, SUBLANES - hist + w:2 * SUBLANES - hist + w, :].reshape(tm, tf)
                   for w in range(hist)]
        tail_ref[...] = a3[:, SUBLANES - hist:, :]
    else:
        first = (i % tiles_per_seq) == 0
        abuf[0:SUBLANES, :] = jnp.where(first, 0.0, carry_sc[j])
        abuf[SUBLANES:SUBLANES + tm, :] = a
        shifted = [abuf[SUBLANES - hist + w:SUBLANES - hist + w + tm, :] for w in range(hist)]
        carry_sc[j] = a[tm - SUBLANES:, :]
        tail_ref[0] = a[tm - hist:, :]
    conv = a * cw[hist:hist + 1, :]
    for w in range(hist):
        conv = conv + shifted[w] * cw[w:w + 1, :]
    conv = conv + cb_ref[...]
    hg = (jax.nn.gelu(conv) * b).astype(BF16)
    y = jnp.dot(hg, wd_ref[...], preferred_element_type=F32).reshape(o_ref.shape)
    o_ref[...] += g2_ref[...] * y

    @pl.when(j == pl.num_programs(1) - 1)
    def _():
        o_ref[...] = _layer_norm(o_ref[...], lng_ref[...], lnb_ref[...])


def _ffn(x3, mod3, mod_group0, wup, wdown, conv_w, conv_b, lng, lnb, conv_state,
         *, tm, tf, alpha):
    g, r, d = x3.shape
    m = g * r
    dff = wdown.shape[0]
    nj = dff // tf
    cwid = conv_w.shape[0]
    hist = cwid - 1
    x_spec, mod_spec, tps, gb = _row_tiling(x3, mod_group0, tm, 2)
    with_state = conv_state is not None
    const = lambda i, j: (0, 0)
    in_specs = [pl.BlockSpec(memory_space=pl.ANY),
                mod_spec(4), mod_spec(3), mod_spec(5),
                pl.BlockSpec((d, tf), lambda i, j: (0, j)),
                pl.BlockSpec((d, tf), lambda i, j: (0, nj + j)),
                pl.BlockSpec((tf, d), lambda i, j: (j, 0)),
                pl.BlockSpec((cwid, tf), lambda i, j: (0, j)),
                pl.BlockSpec((1, tf), lambda i, j: (0, j)),
                pl.BlockSpec((1, d), const),
                pl.BlockSpec((1, d), const)]
    args = [x3, mod3, mod3, mod3, wup, wup, wdown, conv_w, conv_b.reshape(1, dff), lng, lnb]
    scratch = [pltpu.VMEM((tm, d), BF16), pltpu.VMEM(x_spec.block_shape, F32),
               pltpu.SemaphoreType.DMA(())]
    if with_state:
        assert r == SUBLANES
        in_specs.append(pl.BlockSpec((gb, hist, tf), lambda i, j: (i, 0, j)))
        args.append(conv_state)
        tail_spec = pl.BlockSpec((gb, hist, tf), lambda i, j: (i, 0, j))
        scratch.append(pltpu.VMEM((gb, 2 * SUBLANES, tf), F32))
    else:
        tail_spec = pl.BlockSpec((1, hist, tf), lambda i, j: (i, 0, j))
        scratch += [pltpu.VMEM((tm + SUBLANES, tf), F32), pltpu.VMEM((nj, SUBLANES, tf), F32)]
    n_tails = g if with_state else m // tm
    kern = functools.partial(_ffn_kernel, tm=tm, tiles_per_seq=tps, alpha=alpha,
                             with_state=with_state, conv_w=cwid)
    y, tails = pl.pallas_call(
        kern,
        grid=(m // tm, nj),
        in_specs=in_specs,
        out_specs=[x_spec, tail_spec],
        out_shape=[jax.ShapeDtypeStruct(x3.shape, F32),
                   jax.ShapeDtypeStruct((n_tails, hist, dff), F32)],
        scratch_shapes=scratch,
        compiler_params=_params("arbitrary", "arbitrary"),
        name="convffn",
    )(*args)
    if not with_state:
        tails = tails.reshape(g, tps, hist, dff)[:, tps - 1]
    return y, tails


def _rope_tables(pos, half):
    inv = ROPE_BASE ** (-jnp.arange(half, dtype=F32) / half)
    ang = pos.astype(F32)[:, None] * inv[None, :]
    cos, sin = jnp.cos(ang), jnp.sin(ang)
    return jnp.concatenate([cos, cos], axis=-1), jnp.concatenate([-sin, sin], axis=-1)


def kernel(x_prompt, x_sample, cache_k, cache_v, cache_logf, state_ret, state_conv, page_table,
           c_prompt, c_sample, w_ada, b_ada, w_in, b_f, w_pa, w_pb, w_o, ln1_g, ln1_b,
           w_up, conv_w, conv_b, w_down, ln2_g, ln2_b):
    depth = w_ada.shape[0]
    b, t, d = x_prompt.shape
    bs, ts, _ = x_sample.shape
    n_pool, page_size, n_heads, dh = cache_k.shape[1:]
    n_pages = page_table.shape[1]
    past = n_pages * page_size
    wa = n_heads * dh
    dff = w_down.shape[1]
    alpha = (2.0 * depth) ** 0.25
    assert wa == 1024 and dh == LANES and ts == SUBLANES and n_pages % SUBLANES == 0
    assert w_in.shape[2] == 7 * wa + n_heads + 2 * d and d == 2 * wa

    tm_p, tm_s = 1024, bs * ts
    tq = 512
    ret_chunk = 128 if t % 128 == 0 else t

    cos_p, sin_p = _rope_tables(jnp.arange(t), dh // 2)
    cos_s, sin_s = _rope_tables(past + jnp.arange(ts), dh // 2)
    cos_s, sin_s = jnp.tile(cos_s, (tm_s // ts, 1)), jnp.tile(sin_s, (tm_s // ts, 1))

    n_mod = -(-(bs + b) // 16) * 16
    c_all = jnp.concatenate([c_sample, c_prompt, jnp.zeros((n_mod - bs - b, d), F32)], axis=0)

    w_in_t = jnp.swapaxes(w_in, 1, 2)

    hp, hs = x_prompt, x_sample
    outs = {k: [] for k in ("kp", "vp", "lp", "rp", "cp", "ks", "vs", "ls", "rs", "cs")}
    for l in range(depth):
        mod3 = _ada(c_all, w_ada[l], b_ada[l]).reshape(n_mod, 1, 6 * d)

        fa0 = 3 * wa
        w_main = _repack_w_in(w_in_t, l, fa0, n_heads)
        wf_t = jnp.zeros((16, d), F32).at[:n_heads].set(w_in_t[l, fa0:fa0 + n_heads, :])
        bf_col = jnp.zeros((16, 1), F32).at[:n_heads, 0].set(b_f[l].astype(F32))
        lng1, lnb1 = ln1_g[l].reshape(1, d), ln1_b[l].reshape(1, d)
        lng2, lnb2 = ln2_g[l].reshape(1, d), ln2_b[l].reshape(1, d)

        pbf, pk, pv, pg, lf_t, c_t, pkh, pvh = _inproj(hp, mod3, bs, w_main, wf_t, bf_col,
                                                       cos_p, sin_p, tm=tm_p, seq_tiled=True)
        pbf3 = pbf.reshape(b, t, -1)
        nt = t // tm_p
        c4 = (c_t.reshape(b, nt, 16, tm_p // tq, tq).transpose(0, 1, 3, 2, 4)
              .reshape(b, t // tq, 16, tq))
        sbf, _, _, sg, lfs_t, _, skh, svh = _inproj(hs, mod3, 0, w_main, wf_t, bf_col, cos_s, sin_s,
                                                    tm=tm_s, seq_tiled=False)
        rows_q = ts * n_heads
        q3 = sbf[:, _BF_QA * wa:(_BF_QA + 1) * wa].reshape(bs, rows_q, dh)
        k_new = skh.reshape(bs, ts, n_heads, dh)
        v_new = svh.reshape(bs, ts, n_heads, dh)
        lf_s = lfs_t[0, :n_heads, :].T.reshape(bs, ts, n_heads)
        lfn3 = jnp.pad(lf_s.reshape(bs, 1, rows_q), ((0, 0), (0, 0), (0, LANES - rows_q)))
        ck3 = cache_k[l].reshape(n_pool, page_size * n_heads, dh)
        cv3 = cache_v[l].reshape(n_pool, page_size * n_heads, dh)
        clf3 = cache_logf[l].astype(F32).reshape(n_pool, 1, page_size * n_heads)
        oa, oa_s = _fox_attention(pbf3, pk.reshape(b, t, wa), pv.reshape(b, t, wa), c4,
                                  page_table, q3, k_new.reshape(bs, rows_q, dh),
                                  v_new.reshape(bs, rows_q, dh), lfn3, ck3, cv3, clf3,
                                  n_heads=n_heads, tq=tq, hpb=2, n_new=ts)

        ob, ret_p, wpa, wpb, wo, wup, wdown = _retention(
            pbf3, pg.reshape(b, t, -1), None, n_heads=n_heads, chunk=ret_chunk, bpb=4,
            cast=(w_pa, w_pb, w_o, w_up, w_down), layer=l)
        x1 = _merge(oa.reshape(b * t, wa), ob.reshape(b * t, wa), pg, hp, mod3, bs,
                    wpa, wpb, wo, lng1, lnb1, tm=256, alpha=alpha)
        hp, conv_p = _ffn(x1, mod3, bs, wup, wdown, conv_w[l], conv_b[l], lng2, lnb2,
                          None, tm=1024, tf=512, alpha=alpha)
        outs["kp"].append(pkh.reshape(b, t, n_heads, dh))
        outs["vp"].append(pvh.reshape(b, t, n_heads, dh))
        lf_p = lf_t[:, :n_heads, :].reshape(b, nt, n_heads, tm_p)
        outs["lp"].append(lf_p.transpose(0, 1, 3, 2).reshape(b, t, n_heads))
        outs["rp"].append(ret_p)
        outs["cp"].append(conv_p)

        sbf3 = sbf.reshape(bs, ts, -1)
        ob_s, ret_s = _retention(sbf3, sg.reshape(bs, ts, -1), state_ret[l], n_heads=n_heads,
                                 chunk=ts, bpb=8)
        x1s = _merge(oa_s.reshape(bs * ts, wa), ob_s.reshape(bs * ts, wa), sg, hs, mod3, 0,
                     wpa, wpb, wo, lng1, lnb1, tm=256, alpha=alpha)
        hs, conv_s = _ffn(x1s, mod3, 0, wup, wdown, conv_w[l], conv_b[l], lng2, lnb2,
                          state_conv[l], tm=1024, tf=512, alpha=alpha)
        outs["ks"].append(k_new)
        outs["vs"].append(v_new)
        outs["ls"].append(lf_s)
        outs["rs"].append(ret_s)
        outs["cs"].append(conv_s)

    st = lambda name: jnp.stack(outs[name])
    return (hp, hs, st("kp"), st("vp"), st("lp"), st("rp"), st("cp"),
            st("ks"), st("vs"), st("ls"), st("rs"), st("cs"))
```

```python
import functools
import math

import jax
import jax.numpy as jnp
from jax import lax
from jax.experimental import pallas as pl
from jax.experimental.pallas import tpu as pltpu

F32 = jnp.float32
BF16 = jnp.bfloat16

LANES = 128
SUBLANES = 8
VMEM_LIMIT = 56 * 1024 * 1024

ROPE_BASE = 10000.0
LN_EPS = 1e-5
GN_EPS = 1e-5
NEG_BIG = -1e30
LOG2E = math.log2(math.e)

_NT = (((1,), (1,)), ((), ()))
_TN = (((0,), (0,)), ((), ()))


def _params(*sem):
    return pltpu.CompilerParams(dimension_semantics=sem, vmem_limit_bytes=VMEM_LIMIT)


def _layer_norm(r, g, b):
    mu = jnp.mean(r, axis=-1, keepdims=True)
    d = r - mu
    var = jnp.mean(d * d, axis=-1, keepdims=True)
    return d * lax.rsqrt(var + LN_EPS) * g + b


def _ada_kernel(c_ref, w_ref, b_ref, o_ref):
    c = c_ref[...]
    s = (c * jax.nn.sigmoid(c)).astype(BF16)
    o_ref[...] = jnp.dot(s, w_ref[...].astype(BF16), preferred_element_type=F32) + b_ref[...]


def _ada(c_all, w_ada, b_ada):
    rows, d = c_all.shape
    n = w_ada.shape[1]
    tn = 512
    return pl.pallas_call(
        _ada_kernel,
        grid=(n // tn,),
        in_specs=[pl.BlockSpec((rows, d), lambda j: (0, 0)),
                  pl.BlockSpec((d, tn), lambda j: (0, j)),
                  pl.BlockSpec((1, tn), lambda j: (0, j))],
        out_specs=pl.BlockSpec((rows, tn), lambda j: (0, j)),
        out_shape=jax.ShapeDtypeStruct((rows, n), F32),
        compiler_params=_params("arbitrary"),
        name="ada_mod",
    )(c_all, w_ada, b_ada.reshape(1, n))


(_G_QA, _G_KA, _G_VA, _G_QR, _G_KR, _G_VR, _G_GA0, _G_GA1, _G_GB0, _G_GB1, _G_GR) = range(11)
_N_GROUPS = 11
_W_BLOCK_OF_GR = 6
_BF_QA, _BF_QR, _BF_KR, _BF_VR = 0, 1, 2, 3
_GATE_GA, _GATE_GB, _GATE_GR = 0, 2, 4


def _w_block(j):
    return jnp.where(j < _G_GA0, j, jnp.where(j < _G_GR, j + 1, _W_BLOCK_OF_GR))


def _bf_block(j):
    return jnp.clip(j - (_G_QR - _BF_QR), _BF_QA, _BF_VR)


def _gate_block(j):
    return jnp.clip(j - _G_GA0, _GATE_GA, _GATE_GR)


def _repack_kernel(wt_ref, o_ref):
    o_ref[...] = wt_ref[0].T.astype(BF16)


def _repack_w_in(w_in_t, layer, fa0, n_f):
    _, n, d = w_in_t.shape
    tc = 1024
    assert fa0 % tc == 0 and (n - n_f) % tc == 0

    def rows(j):
        return pl.multiple_of(jnp.where(j < fa0 // tc, j * tc, j * tc + n_f), SUBLANES)

    return pl.pallas_call(
        _repack_kernel,
        grid=((n - n_f) // tc,),
        in_specs=[pl.BlockSpec((pl.Element(1), pl.Element(tc), pl.Element(d)),
                               lambda j: (layer, rows(j), 0))],
        out_specs=pl.BlockSpec((d, tc), lambda j: (0, j)),
        out_shape=jax.ShapeDtypeStruct((d, n - n_f), BF16),
        compiler_params=_params("arbitrary"),
        name="repack_w_in",
    )(w_in_t)


def _lane_cumsum(x, carry):
    rows, width = x.shape
    lane = lax.broadcasted_iota(jnp.int32, (rows, LANES), 1)
    out = []
    for c in range(width // LANES):
        v = x[:, c * LANES:(c + 1) * LANES]
        s = 1
        while s < LANES:
            v = v + jnp.where(lane >= s, pltpu.roll(v, s, axis=1), 0.0)
            s *= 2
        out.append(v + carry)
        carry = carry + jnp.broadcast_to(v[:, LANES - 1:LANES], (rows, LANES))
    return jnp.concatenate(out, axis=1), carry


def _x_tile_copy(x_hbm, xbuf, sem, tile, *, tm, tiles_per_seq):
    if tiles_per_seq > 1:
        src = x_hbm.at[pl.ds(tile // tiles_per_seq, 1), pl.ds((tile % tiles_per_seq) * tm, tm)]
    else:
        gb = xbuf.shape[0]
        src = x_hbm.at[pl.ds(tile * gb, gb)]
    return pltpu.make_async_copy(src, xbuf, sem)


def _head_major_copies(src_ref, dst_hbm, sem, tile, *, tm, n_heads):
    return [pltpu.make_async_copy(src_ref.at[:, h * LANES:(h + 1) * LANES],
                                  dst_hbm.at[pl.ds(tile * tm, tm), h, :], sem)
            for h in range(n_heads)]


def _inproj_kernel(x_hbm, sc_ref, sh_ref, w_ref, wf_ref, bf_ref, cos_ref, sin_ref,
                   obf_ref, ok_ref, ov_ref, og_ref, lf_ref, c_ref, kh_hbm, vh_hbm,
                   u_sc, carry_sc, xbuf, xsem, osem,
                   *, tm, tiles_per_seq, qa_scale, kr_scale, n_heads):
    i = pl.program_id(0)
    j = pl.program_id(1)
    x_copy = functools.partial(_x_tile_copy, x_hbm, xbuf, xsem, tm=tm,
                               tiles_per_seq=tiles_per_seq)

    @pl.when(j == 0)
    def _():
        @pl.when(i == 0)
        def _():
            x_copy(0).start()

        x_copy(i).wait()
        x = xbuf[...]
        u = (x * (1.0 + sc_ref[...]) + sh_ref[...]).reshape(tm, x.shape[-1]).astype(BF16)
        u_sc[...] = u

        @pl.when(i + 1 < pl.num_programs(0))
        def _():
            x_copy(i + 1).start()

        z = lax.dot_general(wf_ref[...].astype(BF16), u, _NT,
                            preferred_element_type=F32) + bf_ref[...]
        lf = jnp.minimum(z, 0.0) - jnp.log1p(jnp.exp(-jnp.abs(z)))
        lf_ref[0] = lf
        first = (i % tiles_per_seq) == 0
        prev = jnp.where(first, 0.0, carry_sc[...])
        c, last = _lane_cumsum(lf, prev)
        c_ref[0] = c * LOG2E
        carry_sc[...] = last

    def project():
        return jnp.dot(u_sc[...], w_ref[...], preferred_element_type=F32)

    @pl.when(j == _G_QA)
    def _():
        obf_ref[...] = (project() * qa_scale).astype(BF16)

    def rope(scale):
        acc = project()
        cos = cos_ref[...]
        sin = sin_ref[...]
        for h in range(n_heads):
            sl = slice(h * LANES, (h + 1) * LANES)
            a = acc[:, sl]
            r = a * cos + pltpu.roll(a, LANES // 2, axis=1) * sin
            if scale is not None:
                r = r * scale
            obf_ref[:, sl] = r.astype(BF16)

    @pl.when(j == _G_QR)
    def _():
        rope(None)

    @pl.when(j == _G_KR)
    def _():
        rope(kr_scale)

    @pl.when(j == _G_VR)
    def _():
        obf_ref[...] = project().astype(BF16)

    k_out = functools.partial(_head_major_copies, ok_ref, kh_hbm, osem.at[0], tm=tm, n_heads=n_heads)
    v_out = functools.partial(_head_major_copies, ov_ref, vh_hbm, osem.at[1], tm=tm, n_heads=n_heads)

    @pl.when(j == _G_KA)
    def _():
        ok_ref[...] = project()
        for cp in k_out(i):
            cp.start()

    @pl.when(j == _G_VA)
    def _():
        ov_ref[...] = project()
        for cp in v_out(i):
            cp.start()

    @pl.when(j == pl.num_programs(1) - 1)
    def _():
        for cp in k_out(i) + v_out(i):
            cp.wait()

    @pl.when(j >= _G_GA0)
    def _():
        og_ref[...] = project()


def _inproj(x3, mod3, mod_group0, w_main, wf_t, bf_col, cos_t, sin_t, *, tm, seq_tiled):
    g, r, d = x3.shape
    m = g * r
    n_tiles = m // tm
    if seq_tiled:
        tps = r // tm
        gb = 1
        x_block = (1, tm, d)
        mod_idx = lambda k: (lambda i, j: (mod_group0 + i // tps, 0, k))
        tab_spec = pl.BlockSpec((tm, LANES), lambda i, j: (i % tps, 0))
    else:
        tps = 1
        gb = tm // r
        x_block = (gb, r, d)
        mod_idx = lambda k: (lambda i, j: (mod_group0 // gb + i, 0, k))
        tab_spec = pl.BlockSpec((tm, LANES), lambda i, j: (0, 0))
    n_bf = _BF_VR + 1
    kern = functools.partial(_inproj_kernel, tm=tm, tiles_per_seq=tps,
                             qa_scale=LANES ** -0.5 * LOG2E, kr_scale=LANES ** -0.5,
                             n_heads=1024 // LANES)
    return pl.pallas_call(
        kern,
        grid=(n_tiles, _N_GROUPS),
        in_specs=[pl.BlockSpec(memory_space=pl.ANY),
                  pl.BlockSpec((gb, 1, d), mod_idx(1)),
                  pl.BlockSpec((gb, 1, d), mod_idx(0)),
                  pl.BlockSpec((d, 1024), lambda i, j: (0, _w_block(j))),
                  pl.BlockSpec((16, d), lambda i, j: (0, 0)),
                  pl.BlockSpec((16, 1), lambda i, j: (0, 0)),
                  tab_spec, tab_spec],
        out_specs=[pl.BlockSpec((tm, 1024), lambda i, j: (i, _bf_block(j))),
                   pl.BlockSpec((tm, 1024), lambda i, j: (i, 0)),
                   pl.BlockSpec((tm, 1024), lambda i, j: (i, 0)),
                   pl.BlockSpec((tm, 1024), lambda i, j: (i, _gate_block(j))),
                   pl.BlockSpec((1, 16, tm), lambda i, j: (i, 0, 0)),
                   pl.BlockSpec((1, 16, tm), lambda i, j: (i, 0, 0)),
                   pl.BlockSpec(memory_space=pl.ANY),
                   pl.BlockSpec(memory_space=pl.ANY)],
        out_shape=[jax.ShapeDtypeStruct((m, 1024 * n_bf), BF16),
                   jax.ShapeDtypeStruct((m, 1024), F32),
                   jax.ShapeDtypeStruct((m, 1024), F32),
                   jax.ShapeDtypeStruct((m, 1024 * (_GATE_GR + 1)), F32),
                   jax.ShapeDtypeStruct((n_tiles, 16, tm), F32),
                   jax.ShapeDtypeStruct((n_tiles, 16, tm), F32),
                   jax.ShapeDtypeStruct((m, 1024 // LANES, LANES), F32),
                   jax.ShapeDtypeStruct((m, 1024 // LANES, LANES), F32)],
        scratch_shapes=[pltpu.VMEM((tm, d), BF16), pltpu.VMEM((16, LANES), F32),
                        pltpu.VMEM(x_block, F32), pltpu.SemaphoreType.DMA(()),
                        pltpu.SemaphoreType.DMA((2,))],
        compiler_params=_params("arbitrary", "arbitrary"),
        name="inproj",
    )(x3, mod3, mod3, w_main, wf_t, bf_col, cos_t, sin_t)


def _fox_prefill_body(q_ref, k_ref, v_ref, c_ref, o_ref, hg, qi, *, tq, hpb):
    dh = LANES

    def chunk(kc, carry, diagonal):
        start = pl.multiple_of(kc * tq, tq)
        out = []
        for hh in range(hpb):
            m, l, acc = carry[hh]
            sl = slice(hh * dh, (hh + 1) * dh)
            kk = k_ref[0, pl.ds(start, tq), sl].astype(BF16)
            vv = v_ref[0, pl.ds(start, tq), sl].astype(BF16)
            s = lax.dot_general(q_ref[0, :, sl], kk, _NT, preferred_element_type=F32)
            s = s - c_ref[0, kc, pl.ds(hg * hpb + hh, 1), :]
            if diagonal:
                row = lax.broadcasted_iota(jnp.int32, (tq, tq), 0)
                col = lax.broadcasted_iota(jnp.int32, (tq, tq), 1)
                s = jnp.where(row >= col, s, NEG_BIG)
            m_new = jnp.maximum(m, jnp.max(s, axis=-1, keepdims=True))
            alpha = jnp.exp2(m - m_new)
            p = jnp.exp2(s - m_new)
            l = alpha * l + jnp.sum(p, axis=-1, keepdims=True)
            acc = alpha * acc + jnp.dot(p.astype(BF16), vv, preferred_element_type=F32)
            out.append((m_new, l, acc))
        return tuple(out)

    init = tuple((jnp.full((tq, 1), NEG_BIG, F32), jnp.zeros((tq, 1), F32),
                  jnp.zeros((tq, dh), F32)) for _ in range(hpb))
    carry = lax.fori_loop(0, qi, lambda kc, c: chunk(kc, c, False), init)
    carry = chunk(qi, carry, True)
    for hh in range(hpb):
        _, l, acc = carry[hh]
        o_ref[0, :, hh * dh:(hh + 1) * dh] = (acc / l).astype(BF16)


def _periodic_tail(v, lane, n_heads):
    y = jnp.where(lane >= LANES - n_heads, v, 0.0)
    s = n_heads
    while s < LANES:
        y = y + pltpu.roll(y, LANES - s, axis=1)
        s *= 2
    return y


def _page_copies(pt_ref, ck_hbm, cv_hbm, clf_hbm, kbuf, vbuf, lfbuf, sems, seq, slot, n_pages):
    copies = []
    for p in range(n_pages):
        page = pt_ref[seq * n_pages + p]
        copies.append(pltpu.make_async_copy(ck_hbm.at[page], kbuf.at[slot, p], sems.at[slot, 0]))
        copies.append(pltpu.make_async_copy(cv_hbm.at[page], vbuf.at[slot, p], sems.at[slot, 1]))
        copies.append(pltpu.make_async_copy(clf_hbm.at[page], lfbuf.at[slot, p], sems.at[slot, 2]))
    return copies


def _fox_decode_issue(b, copies):
    @pl.when(b == 0)
    def _():
        for cp in copies(seq=0, slot=0):
            cp.start()

    @pl.when(b + 1 < pl.num_programs(0))
    def _():
        for cp in copies(seq=b + 1, slot=1 - b % 2):
            cp.start()


def _fox_decode_attend(b, copies, q_ref, kn_ref, vn_ref, lfn_ref, o_ref, kbuf, vbuf, lfbuf,
                       lf_sc, s_sc, *, n_pages, n_heads, n_new):
    slot = b % 2
    for cp in copies(seq=b, slot=slot):
        cp.wait()

    k_refs = [kbuf.at[slot, p] for p in range(n_pages)]
    v_refs = [vbuf.at[slot, p] for p in range(n_pages)]
    lf_refs = [lfbuf.at[slot, p] for p in range(n_pages)]

    page_w = lfbuf.shape[-1]
    ppr = n_pages // SUBLANES
    n_chunks = ppr * page_w // LANES
    rows_q = n_new * n_heads

    for p in range(n_pages):
        r, part = divmod(p, ppr)
        lf_sc[r:r + 1, part * page_w:(part + 1) * page_w] = lf_refs[p][...] * LOG2E
    lane = lax.broadcasted_iota(jnp.int32, (SUBLANES, LANES), 1)
    sub = lax.broadcasted_iota(jnp.int32, (SUBLANES, LANES), 0)
    chunks = []
    carry = jnp.zeros((SUBLANES, LANES), F32)
    for c in range(n_chunks):
        v = lf_sc[:, c * LANES:(c + 1) * LANES]
        s = n_heads
        while s < LANES:
            v = v + jnp.where(lane >= s, pltpu.roll(v, s, axis=1), 0.0)
            s *= 2
        chunks.append(v + carry)
        carry = carry + _periodic_tail(v, lane, n_heads)
    inc = carry
    s = 1
    while s < SUBLANES:
        inc = inc + jnp.where(sub >= s, pltpu.roll(inc, s, axis=0), 0.0)
        s *= 2
    exc = jnp.where(sub >= 1, pltpu.roll(inc, 1, axis=0), 0.0)
    chunks = [v + exc for v in chunks]
    past_total = inc[SUBLANES - 1:SUBLANES, :]

    q = q_ref[0]
    dh = q.shape[-1]
    row_h = lax.broadcasted_iota(jnp.int32, (rows_q, page_w), 0) % n_heads
    col_h = lax.broadcasted_iota(jnp.int32, (rows_q, page_w), 1) % n_heads
    same_head = row_h == col_h

    cpp = page_w // LANES
    mx = jnp.full((rows_q, LANES), NEG_BIG, F32)
    for p in range(n_pages):
        r, part = divmod(p, ppr)
        ck = jnp.concatenate([chunks[part * cpp + c][r:r + 1, :] for c in range(cpp)], axis=1)
        kk = k_refs[p][...].astype(BF16)
        s = lax.dot_general(q, kk, _NT, preferred_element_type=F32)
        s = jnp.where(same_head, s - ck, NEG_BIG)
        s_sc[p] = s
        for c in range(cpp):
            mx = jnp.maximum(mx, s[:, c * LANES:(c + 1) * LANES])

    lane1 = lax.broadcasted_iota(jnp.int32, (1, LANES), 1)
    cn = lfn_ref[0] * LOG2E
    s = n_heads
    while s < rows_q:
        cn = cn + jnp.where(lane1 >= s, pltpu.roll(cn, s, axis=1), 0.0)
        s *= 2
    cn = cn + past_total
    s_new = lax.dot_general(q, kn_ref[0].astype(BF16), _NT, preferred_element_type=F32)
    row = lax.broadcasted_iota(jnp.int32, (rows_q, rows_q), 0)
    col = lax.broadcasted_iota(jnp.int32, (rows_q, rows_q), 1)
    ok = ((row % n_heads) == (col % n_heads)) & (col <= row)
    s_new = jnp.where(ok, s_new - cn[:, :rows_q], NEG_BIG)
    m = jnp.maximum(jnp.max(mx, axis=-1, keepdims=True), jnp.max(s_new, axis=-1, keepdims=True))

    lsum = jnp.zeros((rows_q, LANES), F32)
    acc = jnp.zeros((rows_q, dh), F32)
    for p in range(n_pages):
        pr = jnp.exp2(s_sc[p] - m)
        for c in range(cpp):
            lsum = lsum + pr[:, c * LANES:(c + 1) * LANES]
        acc = acc + jnp.dot(pr.astype(BF16), v_refs[p][...].astype(BF16),
                            preferred_element_type=F32)
    pr = jnp.exp2(s_new - m)
    l = jnp.sum(lsum, axis=-1, keepdims=True) + jnp.sum(pr, axis=-1, keepdims=True)
    acc = acc + jnp.dot(pr.astype(BF16), vn_ref[0].astype(BF16), preferred_element_type=F32)
    o_ref[0] = (acc / l).astype(BF16)


def _fox_kernel(pt_ref, qp_ref, kp_ref, vp_ref, cp_ref, q_ref, kn_ref, vn_ref, lfn_ref,
                ck_hbm, cv_hbm, clf_hbm, op_ref, o_ref, kbuf, vbuf, lfbuf, sems, lf_sc, s_sc,
                *, tq, hpb, n_hg, n_q, n_pages, n_heads, n_new):
    step = pl.program_id(0)
    copies = functools.partial(_page_copies, pt_ref, ck_hbm, cv_hbm, clf_hbm, kbuf, vbuf, lfbuf,
                               sems, n_pages=n_pages)
    _fox_decode_issue(step, copies)
    _fox_prefill_body(qp_ref, kp_ref, vp_ref, cp_ref, op_ref, (step // n_q) % n_hg, step % n_q,
                      tq=tq, hpb=hpb)
    _fox_decode_attend(step, copies, q_ref, kn_ref, vn_ref, lfn_ref, o_ref, kbuf, vbuf, lfbuf,
                       lf_sc, s_sc, n_pages=n_pages, n_heads=n_heads, n_new=n_new)


def _fox_attention(pbf3, k3, v3, c4, page_table, q3, kn3, vn3, lfn3, ck3, cv3, clf3,
                   *, n_heads, tq, hpb, n_new):
    b, t, _ = pbf3.shape
    w = hpb * LANES
    n_hg = n_heads // hpb
    n_q = t // tq
    bs, n_pages = page_table.shape
    rows_q, dh = q3.shape[1], q3.shape[2]
    page_rows = ck3.shape[1]
    page_w = clf3.shape[2]
    assert b * n_hg * n_q == bs, "prefill steps and decode sequences must pair up"

    def pre(col0):
        return lambda s, pt: (s // (n_hg * n_q), s % n_q, col0 + (s // n_q) % n_hg)

    def pre_kv(s, pt):
        return (s // (n_hg * n_q), 0, (s // n_q) % n_hg)

    def same(s, pt):
        return (s, 0, 0)

    hbm = pl.BlockSpec(memory_space=pl.ANY)
    grid_spec = pltpu.PrefetchScalarGridSpec(
        num_scalar_prefetch=1,
        grid=(bs,),
        in_specs=[pl.BlockSpec((1, tq, w), pre(_BF_QA * n_hg)),
                  pl.BlockSpec((1, t, w), pre_kv),
                  pl.BlockSpec((1, t, w), pre_kv),
                  pl.BlockSpec((1, n_q, 16, tq), lambda s, pt: (s // (n_hg * n_q), 0, 0, 0)),
                  pl.BlockSpec((1, rows_q, dh), same),
                  pl.BlockSpec((1, rows_q, dh), same),
                  pl.BlockSpec((1, rows_q, dh), same),
                  pl.BlockSpec((1, 1, LANES), same),
                  hbm, hbm, hbm],
        out_specs=[pl.BlockSpec((1, tq, w), pre(0)),
                   pl.BlockSpec((1, rows_q, dh), same)],
        scratch_shapes=[pltpu.VMEM((2, n_pages, page_rows, dh), F32),
                        pltpu.VMEM((2, n_pages, page_rows, dh), F32),
                        pltpu.VMEM((2, n_pages, 1, page_w), F32),
                        pltpu.SemaphoreType.DMA((2, 3)),
                        pltpu.VMEM((SUBLANES, n_pages // SUBLANES * page_w), F32),
                        pltpu.VMEM((n_pages, rows_q, page_w), F32)],
    )
    return pl.pallas_call(
        functools.partial(_fox_kernel, tq=tq, hpb=hpb, n_hg=n_hg, n_q=n_q, n_pages=n_pages,
                          n_heads=n_heads, n_new=n_new),
        grid_spec=grid_spec,
        out_shape=[jax.ShapeDtypeStruct((b, t, n_heads * LANES), BF16),
                   jax.ShapeDtypeStruct((bs, rows_q, dh), BF16)],
        compiler_params=_params("arbitrary"),
        name="fox_attention",
    )(page_table.reshape(-1), pbf3, k3, v3, c4, q3, kn3, vn3, lfn3, ck3, cv3, clf3)


def _ret_kernel(*refs, n_heads, has_state, bpb, n_cast):
    n_in = 8 + (1 if has_state else 0)
    cast_in = refs[n_in:n_in + n_cast]
    cast_out = refs[n_in + n_cast + 2:n_in + 2 * n_cast + 2]
    refs = refs[:n_in] + refs[n_in + n_cast:n_in + n_cast + 2] + refs[n_in + 2 * n_cast + 2:]
    for src, dst in zip(cast_in, cast_out):
        dst[...] = src[0].astype(BF16)
    if has_state:
        (q_ref, k_ref, v_ref, g_ref, dmat_ref, qdec_ref, kdec_ref, sdec_ref, s0_ref,
         o_ref, sout_ref, s_sc) = refs
    else:
        (q_ref, k_ref, v_ref, g_ref, dmat_ref, qdec_ref, kdec_ref, sdec_ref,
         o_ref, sout_ref, s_sc) = refs
    c = pl.program_id(1)

    @pl.when(c == 0)
    def _():
        if has_state:
            s_sc[...] = s0_ref[...]
        else:
            s_sc[...] = jnp.zeros_like(s_sc)

    for bb in range(bpb):
        for h in range(n_heads):
            sl = slice(h * LANES, (h + 1) * LANES)
            q = q_ref[bb, :, sl]
            k = k_ref[bb, :, sl]
            v = v_ref[bb, :, sl]
            st = s_sc[bb, h]
            att = lax.dot_general(q, k, _NT, preferred_element_type=F32) * dmat_ref[h]
            inner = jnp.dot(att.astype(BF16), v, preferred_element_type=F32)
            cross = jnp.dot(q, st.astype(BF16), preferred_element_type=F32) * qdec_ref[h]
            o = inner + cross
            kw = (k.astype(F32) * kdec_ref[h]).astype(BF16)
            s_sc[bb, h] = sdec_ref[h] * st + lax.dot_general(kw, v, _TN,
                                                            preferred_element_type=F32)
            mu = jnp.mean(o, axis=-1, keepdims=True)
            d = o - mu
            var = jnp.mean(d * d, axis=-1, keepdims=True)
            g = g_ref[bb, :, sl]
            o_ref[bb, :, sl] = (d * lax.rsqrt(var + GN_EPS)
                                * (g * jax.nn.sigmoid(g))).astype(BF16)

    @pl.when(c == pl.num_programs(1) - 1)
    def _():
        sout_ref[...] = s_sc[...]


def _retention_tables(n_heads, chunk, dk):
    lg = jnp.log(1.0 - jnp.exp2(-5.0 - jnp.arange(n_heads, dtype=F32)))
    idx = jnp.arange(chunk, dtype=F32)
    diff = idx[:, None] - idx[None, :]
    dmat = jnp.where(diff[None] >= 0, jnp.exp(diff[None] * lg[:, None, None]), 0.0)
    qdec = jnp.exp((idx + 1.0)[None, :] * lg[:, None])
    kdec = jnp.exp((chunk - 1.0 - idx)[None, :] * lg[:, None])
    sdec = jnp.exp(chunk * lg)
    bc = lambda a: jnp.broadcast_to(a[:, :, None], a.shape + (dk,))
    return dmat, bc(qdec), bc(kdec), jnp.broadcast_to(sdec[:, None, None], (n_heads, 1, dk))


def _retention(pbf3, gates3, state, *, n_heads, chunk, bpb, cast=(), layer=0):
    b, t, _ = pbf3.shape
    w = n_heads * LANES
    n_chunks = t // chunk
    n_steps = (b // bpb) * n_chunks
    tabs = _retention_tables(n_heads, chunk, LANES)
    has_state = state is not None
    const3 = lambda bi, c: (0, 0, 0)
    cast_in, cast_out, cast_shape = [], [], []
    for wt in cast:
        _, rows, cols = wt.shape
        rb = rows // n_steps
        assert rb * n_steps == rows and rb % 16 == 0, "row block must be whole bf16 tiles"
        cast_in.append(pl.BlockSpec((1, rb, cols), lambda bi, c: (layer, bi * n_chunks + c, 0)))
        cast_out.append(pl.BlockSpec((rb, cols), lambda bi, c: (bi * n_chunks + c, 0)))
        cast_shape.append(jax.ShapeDtypeStruct((rows, cols), BF16))
    in_specs = [pl.BlockSpec((bpb, chunk, w), lambda bi, c: (bi, c, _BF_QR)),
                pl.BlockSpec((bpb, chunk, w), lambda bi, c: (bi, c, _BF_KR)),
                pl.BlockSpec((bpb, chunk, w), lambda bi, c: (bi, c, _BF_VR)),
                pl.BlockSpec((bpb, chunk, w), lambda bi, c: (bi, c, _GATE_GR)),
                pl.BlockSpec((n_heads, chunk, chunk), const3),
                pl.BlockSpec((n_heads, chunk, LANES), const3),
                pl.BlockSpec((n_heads, chunk, LANES), const3),
                pl.BlockSpec((n_heads, 1, LANES), const3)]
    args = [pbf3, pbf3, pbf3, gates3, *tabs]
    state_spec = pl.BlockSpec((bpb, n_heads, LANES, LANES), lambda bi, c: (bi, 0, 0, 0))
    if has_state:
        in_specs.append(state_spec)
        args.append(state)
    return pl.pallas_call(
        functools.partial(_ret_kernel, n_heads=n_heads, has_state=has_state, bpb=bpb,
                          n_cast=len(cast)),
        grid=(b // bpb, n_chunks),
        in_specs=in_specs + cast_in,
        out_specs=[pl.BlockSpec((bpb, chunk, w), lambda bi, c: (bi, c, 0)), state_spec] + cast_out,
        out_shape=[jax.ShapeDtypeStruct((b, t, w), BF16),
                   jax.ShapeDtypeStruct((b, n_heads, LANES, LANES), F32)] + cast_shape,
        scratch_shapes=[pltpu.VMEM((bpb, n_heads, LANES, LANES), F32)],
        compiler_params=_params("arbitrary", "arbitrary"),
        name="retention",
    )(*args, *cast)


def _merge_kernel(oa_ref, ob_ref, ga_ref, gb_ref, x_ref, g1_ref, wpa_ref, wpb_ref, wo_ref,
                  lng_ref, lnb_ref, o_ref, *, alpha):
    a = jnp.dot(oa_ref[...], wpa_ref[...], preferred_element_type=F32)
    b = jnp.dot(ob_ref[...], wpb_ref[...], preferred_element_type=F32)
    merged = jax.nn.sigmoid(ga_ref[...]) * a + jax.nn.sigmoid(gb_ref[...]) * b
    y = jnp.dot(merged.astype(BF16), wo_ref[...], preferred_element_type=F32)
    x = x_ref[...]
    r = alpha * x + g1_ref[...] * y.reshape(x.shape)
    o_ref[...] = _layer_norm(r, lng_ref[...], lnb_ref[...])


def _row_tiling(x3, mod_group0, tm, n_grid_axes):
    g, r, d = x3.shape
    if r >= tm:
        tps = r // tm
        gb = 1
        if n_grid_axes == 1:
            x_map = lambda i: (i // tps, i % tps, 0)
            mod_idx = lambda k: (lambda i: (mod_group0 + i // tps, 0, k))
        else:
            x_map = lambda i, j: (i // tps, i % tps, 0)
            mod_idx = lambda k: (lambda i, j: (mod_group0 + i // tps, 0, k))
        x_spec = pl.BlockSpec((1, tm, d), x_map)
    else:
        tps = 1
        gb = tm // r
        if n_grid_axes == 1:
            x_map = lambda i: (i, 0, 0)
            mod_idx = lambda k: (lambda i: (mod_group0 // gb + i, 0, k))
        else:
            x_map = lambda i, j: (i, 0, 0)
            mod_idx = lambda k: (lambda i, j: (mod_group0 // gb + i, 0, k))
        x_spec = pl.BlockSpec((gb, r, d), x_map)
    mod_spec = lambda k: pl.BlockSpec((gb, 1, d), mod_idx(k))
    return x_spec, mod_spec, tps, gb


def _merge(oa2, ob2, gates, x3, mod3, mod_group0, wpa, wpb, wo, lng, lnb, *, tm, alpha):
    g, r, d = x3.shape
    m = g * r
    wa = oa2.shape[1]
    x_spec, mod_spec, _, _ = _row_tiling(x3, mod_group0, tm, 1)
    const = lambda i: (0, 0)
    return pl.pallas_call(
        functools.partial(_merge_kernel, alpha=alpha),
        grid=(m // tm,),
        in_specs=[pl.BlockSpec((tm, wa), lambda i: (i, 0)),
                  pl.BlockSpec((tm, wa), lambda i: (i, 0)),
                  pl.BlockSpec((tm, d), lambda i: (i, _GATE_GA // 2)),
                  pl.BlockSpec((tm, d), lambda i: (i, _GATE_GB // 2)),
                  x_spec,
                  mod_spec(2),
                  pl.BlockSpec(wpa.shape, const, pipeline_mode=pl.Buffered(1)),
                  pl.BlockSpec(wpb.shape, const, pipeline_mode=pl.Buffered(1)),
                  pl.BlockSpec(wo.shape, const, pipeline_mode=pl.Buffered(1)),
                  pl.BlockSpec((1, d), const),
                  pl.BlockSpec((1, d), const)],
        out_specs=x_spec,
        out_shape=jax.ShapeDtypeStruct(x3.shape, F32),
        compiler_params=_params("arbitrary"),
        name="merge_outproj",
    )(oa2, ob2, gates, gates, x3, mod3, wpa, wpb, wo, lng, lnb)


def _ffn_kernel(*refs, tm, tiles_per_seq, alpha, with_state, conv_w):
    if with_state:
        (x_hbm, sc_ref, sh_ref, g2_ref, wa_ref, wb_ref, wd_ref, cw_ref, cb_ref, lng_ref, lnb_ref,
         st_ref, o_ref, tail_ref, u_sc, xbuf, xsem, abuf) = refs
    else:
        (x_hbm, sc_ref, sh_ref, g2_ref, wa_ref, wb_ref, wd_ref, cw_ref, cb_ref, lng_ref, lnb_ref,
         o_ref, tail_ref, u_sc, xbuf, xsem, abuf, carry_sc) = refs
    i = pl.program_id(0)
    j = pl.program_id(1)
    hist = conv_w - 1
    x_copy = functools.partial(_x_tile_copy, x_hbm, xbuf, xsem, tm=tm,
                               tiles_per_seq=tiles_per_seq)

    @pl.when(j == 0)
    def _():
        @pl.when(i == 0)
        def _():
            x_copy(0).start()

        x_copy(i).wait()
        x = xbuf[...]
        u_sc[...] = (x * (1.0 + sc_ref[...]) + sh_ref[...]).reshape(tm, x.shape[-1]).astype(BF16)
        o_ref[...] = alpha * x

        @pl.when(i + 1 < pl.num_programs(0))
        def _():
            x_copy(i + 1).start()

    u = u_sc[...]
    a = jnp.dot(u, wa_ref[...], preferred_element_type=F32)
    b = jnp.dot(u, wb_ref[...], preferred_element_type=F32)
    tf = a.shape[-1]
    cw = cw_ref[...]
    if with_state:
        gb = tm // SUBLANES
        a3 = a.reshape(gb, SUBLANES, tf)
        abuf[:, SUBLANES:2 * SUBLANES, :] = a3
        abuf[:, SUBLANES - hist:SUBLANES, :] = st_ref[...]
        shifted = [abuf[:, SUBLANES - hist + w:2 * SUBLANES - hist + w, :].reshape(tm, tf)
                   for w in range(hist)]
        tail_ref[...] = a3[:, SUBLANES - hist:, :]
    else:
        first = (i % tiles_per_seq) == 0
        abuf[0:SUBLANES, :] = jnp.where(first, 0.0, carry_sc[j])
        abuf[SUBLANES:SUBLANES + tm, :] = a
        shifted = [abuf[SUBLANES - hist + w:SUBLANES - hist + w + tm, :] for w in range(hist)]
        carry_sc[j] = a[tm - SUBLANES:, :]
        tail_ref[0] = a[tm - hist:, :]
    conv = a * cw[hist:hist + 1, :]
    for w in range(hist):
        conv = conv + shifted[w] * cw[w:w + 1, :]
    conv = conv + cb_ref[...]
    hg = (jax.nn.gelu(conv) * b).astype(BF16)
    y = jnp.dot(hg, wd_ref[...], preferred_element_type=F32).reshape(o_ref.shape)
    o_ref[...] += g2_ref[...] * y

    @pl.when(j == pl.num_programs(1) - 1)
    def _():
        o_ref[...] = _layer_norm(o_ref[...], lng_ref[...], lnb_ref[...])


def _ffn(x3, mod3, mod_group0, wup, wdown, conv_w, conv_b, lng, lnb, conv_state,
         *, tm, tf, alpha):
    g, r, d = x3.shape
    m = g * r
    dff = wdown.shape[0]
    nj = dff // tf
    cwid = conv_w.shape[0]
    hist = cwid - 1
    x_spec, mod_spec, tps, gb = _row_tiling(x3, mod_group0, tm, 2)
    with_state = conv_state is not None
    const = lambda i, j: (0, 0)
    in_specs = [pl.BlockSpec(memory_space=pl.ANY),
                mod_spec(4), mod_spec(3), mod_spec(5),
                pl.BlockSpec((d, tf), lambda i, j: (0, j)),
                pl.BlockSpec((d, tf), lambda i, j: (0, nj + j)),
                pl.BlockSpec((tf, d), lambda i, j: (j, 0)),
                pl.BlockSpec((cwid, tf), lambda i, j: (0, j)),
                pl.BlockSpec((1, tf), lambda i, j: (0, j)),
                pl.BlockSpec((1, d), const),
                pl.BlockSpec((1, d), const)]
    args = [x3, mod3, mod3, mod3, wup, wup, wdown, conv_w, conv_b.reshape(1, dff), lng, lnb]
    scratch = [pltpu.VMEM((tm, d), BF16), pltpu.VMEM(x_spec.block_shape, F32),
               pltpu.SemaphoreType.DMA(())]
    if with_state:
        assert r == SUBLANES
        in_specs.append(pl.BlockSpec((gb, hist, tf), lambda i, j: (i, 0, j)))
        args.append(conv_state)
        tail_spec = pl.BlockSpec((gb, hist, tf), lambda i, j: (i, 0, j))
        scratch.append(pltpu.VMEM((gb, 2 * SUBLANES, tf), F32))
    else:
        tail_spec = pl.BlockSpec((1, hist, tf), lambda i, j: (i, 0, j))
        scratch += [pltpu.VMEM((tm + SUBLANES, tf), F32), pltpu.VMEM((nj, SUBLANES, tf), F32)]
    n_tails = g if with_state else m // tm
    kern = functools.partial(_ffn_kernel, tm=tm, tiles_per_seq=tps, alpha=alpha,
                             with_state=with_state, conv_w=cwid)
    y, tails = pl.pallas_call(
        kern,
        grid=(m // tm, nj),
        in_specs=in_specs,
        out_specs=[x_spec, tail_spec],
        out_shape=[jax.ShapeDtypeStruct(x3.shape, F32),
                   jax.ShapeDtypeStruct((n_tails, hist, dff), F32)],
        scratch_shapes=scratch,
        compiler_params=_params("arbitrary", "arbitrary"),
        name="convffn",
    )(*args)
    if not with_state:
        tails = tails.reshape(g, tps, hist, dff)[:, tps - 1]
    return y, tails


def _rope_tables(pos, half):
    inv = ROPE_BASE ** (-jnp.arange(half, dtype=F32) / half)
    ang = pos.astype(F32)[:, None] * inv[None, :]
    cos, sin = jnp.cos(ang), jnp.sin(ang)
    return jnp.concatenate([cos, cos], axis=-1), jnp.concatenate([-sin, sin], axis=-1)


def kernel(x_prompt, x_sample, cache_k, cache_v, cache_logf, state_ret, state_conv, page_table,
           c_prompt, c_sample, w_ada, b_ada, w_in, b_f, w_pa, w_pb, w_o, ln1_g, ln1_b,
           w_up, conv_w, conv_b, w_down, ln2_g, ln2_b):
    depth = w_ada.shape[0]
    b, t, d = x_prompt.shape
    bs, ts, _ = x_sample.shape
    n_pool, page_size, n_heads, dh = cache_k.shape[1:]
    n_pages = page_table.shape[1]
    past = n_pages * page_size
    wa = n_heads * dh
    dff = w_down.shape[1]
    alpha = (2.0 * depth) ** 0.25
    assert wa == 1024 and dh == LANES and ts == SUBLANES and n_pages % SUBLANES == 0
    assert w_in.shape[2] == 7 * wa + n_heads + 2 * d and d == 2 * wa

    tm_p, tm_s = 1024, bs * ts
    tq = 512
    ret_chunk = 128 if t % 128 == 0 else t

    cos_p, sin_p = _rope_tables(jnp.arange(t), dh // 2)
    cos_s, sin_s = _rope_tables(past + jnp.arange(ts), dh // 2)
    cos_s, sin_s = jnp.tile(cos_s, (tm_s // ts, 1)), jnp.tile(sin_s, (tm_s // ts, 1))

    n_mod = -(-(bs + b) // 16) * 16
    c_all = jnp.concatenate([c_sample, c_prompt, jnp.zeros((n_mod - bs - b, d), F32)], axis=0)

    w_in_t = jnp.swapaxes(w_in, 1, 2)

    hp, hs = x_prompt, x_sample
    outs = {k: [] for k in ("kp", "vp", "lp", "rp", "cp", "ks", "vs", "ls", "rs", "cs")}
    for l in range(depth):
        mod3 = _ada(c_all, w_ada[l], b_ada[l]).reshape(n_mod, 1, 6 * d)

        fa0 = 3 * wa
        w_main = _repack_w_in(w_in_t, l, fa0, n_heads)
        wf_t = jnp.zeros((16, d), F32).at[:n_heads].set(w_in_t[l, fa0:fa0 + n_heads, :])
        bf_col = jnp.zeros((16, 1), F32).at[:n_heads, 0].set(b_f[l].astype(F32))
        lng1, lnb1 = ln1_g[l].reshape(1, d), ln1_b[l].reshape(1, d)
        lng2, lnb2 = ln2_g[l].reshape(1, d), ln2_b[l].reshape(1, d)

        pbf, pk, pv, pg, lf_t, c_t, pkh, pvh = _inproj(hp, mod3, bs, w_main, wf_t, bf_col,
                                                       cos_p, sin_p, tm=tm_p, seq_tiled=True)
        pbf3 = pbf.reshape(b, t, -1)
        nt = t // tm_p
        c4 = (c_t.reshape(b, nt, 16, tm_p // tq, tq).transpose(0, 1, 3, 2, 4)
              .reshape(b, t // tq, 16, tq))
        sbf, _, _, sg, lfs_t, _, skh, svh = _inproj(hs, mod3, 0, w_main, wf_t, bf_col, cos_s, sin_s,
                                                    tm=tm_s, seq_tiled=False)
        rows_q = ts * n_heads
        q3 = sbf[:, _BF_QA * wa:(_BF_QA + 1) * wa].reshape(bs, rows_q, dh)
        k_new = skh.reshape(bs, ts, n_heads, dh)
        v_new = svh.reshape(bs, ts, n_heads, dh)
        lf_s = lfs_t[0, :n_heads, :].T.reshape(bs, ts, n_heads)
        lfn3 = jnp.pad(lf_s.reshape(bs, 1, rows_q), ((0, 0), (0, 0), (0, LANES - rows_q)))
        ck3 = cache_k[l].reshape(n_pool, page_size * n_heads, dh)
        cv3 = cache_v[l].reshape(n_pool, page_size * n_heads, dh)
        clf3 = cache_logf[l].astype(F32).reshape(n_pool, 1, page_size * n_heads)
        oa, oa_s = _fox_attention(pbf3, pk.reshape(b, t, wa), pv.reshape(b, t, wa), c4,
                                  page_table, q3, k_new.reshape(bs, rows_q, dh),
                                  v_new.reshape(bs, rows_q, dh), lfn3, ck3, cv3, clf3,
                                  n_heads=n_heads, tq=tq, hpb=2, n_new=ts)

        ob, ret_p, wpa, wpb, wo, wup, wdown = _retention(
            pbf3, pg.reshape(b, t, -1), None, n_heads=n_heads, chunk=ret_chunk, bpb=4,
            cast=(w_pa, w_pb, w_o, w_up, w_down), layer=l)
        x1 = _merge(oa.reshape(b * t, wa), ob.reshape(b * t, wa), pg, hp, mod3, bs,
                    wpa, wpb, wo, lng1, lnb1, tm=256, alpha=alpha)
        hp, conv_p = _ffn(x1, mod3, bs, wup, wdown, conv_w[l], conv_b[l], lng2, lnb2,
                          None, tm=1024, tf=512, alpha=alpha)
        outs["kp"].append(pkh.reshape(b, t, n_heads, dh))
        outs["vp"].append(pvh.reshape(b, t, n_heads, dh))
        lf_p = lf_t[:, :n_heads, :].reshape(b, nt, n_heads, tm_p)
        outs["lp"].append(lf_p.transpose(0, 1, 3, 2).reshape(b, t, n_heads))
        outs["rp"].append(ret_p)
        outs["cp"].append(conv_p)

        sbf3 = sbf.reshape(bs, ts, -1)
        ob_s, ret_s = _retention(sbf3, sg.reshape(bs, ts, -1), state_ret[l], n_heads=n_heads,
                                 chunk=ts, bpb=8)
        x1s = _merge(oa_s.reshape(bs * ts, wa), ob_s.reshape(bs * ts, wa), sg, hs, mod3, 0,
                     wpa, wpb, wo, lng1, lnb1, tm=256, alpha=alpha)
        hs, conv_s = _ffn(x1s, mod3, 0, wup, wdown, conv_w[l], conv_b[l], lng2, lnb2,
                          state_conv[l], tm=1024, tf=512, alpha=alpha)
        outs["ks"].append(k_new)
        outs["vs"].append(v_new)
        outs["ls"].append(lf_s)
        outs["rs"].append(ret_s)
        outs["cs"].append(conv_s)

    st = lambda name: jnp.stack(outs[name])
    return (hp, hs, st("kp"), st("vp"), st("lp"), st("rp"), st("cp"),
            st("ks"), st("vs"), st("ls"), st("rs"), st("cs"))
```
